```python
import math
import jax
import jax.numpy as jnp
from jax import lax
import numpy as np

D_MODEL = 2048
BATCH = 8
SEQ = 2048
DEPTH = 1
DEC_BATCH = 16
DEC_SEQ = 32
PAST_LEN = 2048

CHUNK = 64
WINDOW = 128
WINDOW_CHUNKS = WINDOW // CHUNK
HEAD_DIM = 64
D_MIX = D_MODEL
D_ATTN = D_MIX // 2
D_LRU = D_MIX - D_ATTN
N_Q_HEADS = D_ATTN // HEAD_DIM
N_KV_HEADS = 4
GQA_GROUP = N_Q_HEADS // N_KV_HEADS
D_KV = N_KV_HEADS * HEAD_DIM
N_LRU_BLOCKS = 16
LRU_BLOCK = D_LRU // N_LRU_BLOCKS
CONV_WIDTH = 4
LRU_C = 8.0
D_IN = 2 * D_ATTN + 2 * D_KV + 2 * D_LRU
SPLIT_POINTS = (D_ATTN, D_ATTN + D_KV, D_ATTN + 2 * D_KV, 2 * D_ATTN + 2 * D_KV,
                2 * D_ATTN + 2 * D_KV + D_LRU)
DEEPNORM_ALPHA = (2.0 * DEPTH) ** 0.25
DEEPNORM_BETA = (8.0 * DEPTH) ** -0.25
LN_EPS = 1e-5
NEG_INF = -1e30

kernel_name = "hymba_swa_sink_alibi_rglru_deepnorm_step"


def _alibi_slopes():
    h = jnp.arange(1, N_Q_HEADS + 1, dtype=jnp.float32)
    return jnp.exp2(-8.0 * h / N_Q_HEADS).reshape(N_KV_HEADS, GQA_GROUP)


def _band_attention(q, k, v, qpos, kpos, sinks):
    qc = qpos // CHUNK
    kc = kpos // CHUNK
    valid = ((kpos[:, None, :] >= 0)
             & (kc[:, None, :] <= qc[:, :, None])
             & (kc[:, None, :] >= qc[:, :, None] - WINDOW_CHUNKS))
    dist = jnp.abs(qpos[:, :, None] - kpos[:, None, :]).astype(jnp.float32)
    slopes = _alibi_slopes()
    s = jnp.einsum('bnqhgd,bnkhd->bnhgqk', q.astype(jnp.float32), k.astype(jnp.float32))
    s = s * (HEAD_DIM ** -0.5) - slopes[None, None, :, :, None, None] * dist[None, :, None, None]
    s = jnp.where(valid[None, :, None, None], s, NEG_INF)
    sink = sinks.astype(jnp.float32).reshape(N_KV_HEADS, GQA_GROUP)[None, None, :, :, None, None]
    m = jnp.maximum(jnp.max(s, axis=-1, keepdims=True), sink)
    e = jnp.exp(s - m)
    p = e / (jnp.sum(e, axis=-1, keepdims=True) + jnp.exp(sink - m))
    return jnp.einsum('bnhgqk,bnkhd->bnqhgd', p, v.astype(jnp.float32))


def _attn_prompt(q, k, v, sinks):
    B, S = q.shape[0], q.shape[1]
    nC = S // CHUNK
    pad = WINDOW_CHUNKS * CHUNK
    qb = q.reshape(B, nC, CHUNK, N_KV_HEADS, GQA_GROUP, HEAD_DIM)

    def band(t):
        tp = jnp.pad(t, ((0, 0), (pad, 0), (0, 0), (0, 0))).reshape(B, nC + WINDOW_CHUNKS, CHUNK, N_KV_HEADS, HEAD_DIM)
        return jnp.concatenate([tp[:, j:j + nC] for j in range(WINDOW_CHUNKS + 1)], axis=2)

    kb, vb = band(k), band(v)
    qpos = jnp.arange(S, dtype=jnp.int32).reshape(nC, CHUNK)
    kpos = (jnp.arange(nC, dtype=jnp.int32)[:, None] * CHUNK - pad
            + jnp.arange((WINDOW_CHUNKS + 1) * CHUNK, dtype=jnp.int32)[None, :])
    o = _band_attention(qb, kb, vb, qpos, kpos, sinks)
    return o.reshape(B, S, D_ATTN)


def _attn_sample(q, k_all, v_all, sinks):
    B, T = q.shape[0], q.shape[1]
    n_keys = k_all.shape[1]
    qpos = (PAST_LEN + jnp.arange(T, dtype=jnp.int32))[None]
    kpos = (PAST_LEN + T - n_keys + jnp.arange(n_keys, dtype=jnp.int32))[None]
    o = _band_attention(q[:, None], k_all[:, None], v_all[:, None], qpos, kpos, sinks)
    return o.reshape(B, T, D_ATTN)


def _block_diag(x, w, b):
    B, T = x.shape[0], x.shape[1]
    xb = x.reshape(B, T, N_LRU_BLOCKS, LRU_BLOCK)
    return (jnp.einsum('btnd,nde->btne', xb, w) + b).reshape(B, T, D_LRU)


def _rglru(x, h0, w_ga, b_ga, w_gx, b_gx, lam):
    xf = x.astype(jnp.float32)
    r = jax.nn.sigmoid(_block_diag(xf, w_ga.astype(jnp.float32), b_ga.astype(jnp.float32)))
    i = jax.nn.sigmoid(_block_diag(xf, w_gx.astype(jnp.float32), b_gx.astype(jnp.float32)))
    log_a = -LRU_C * r * jax.nn.softplus(-lam.astype(jnp.float32))
    a = jnp.exp(log_a)
    mult = jnp.sqrt(jnp.maximum(-jnp.expm1(2.0 * log_a), 0.0))
    bterm = mult * (i * xf)
    bterm = bterm.at[:, 0].add(a[:, 0] * h0.astype(jnp.float32))

    def combine(lhs, rhs):
        a1, b1 = lhs
        a2, b2 = rhs
        return a1 * a2, a2 * b1 + b2

    _, h = lax.associative_scan(combine, (a, bterm), axis=1)
    return h


def _layer_norm(y, g, b):
    yf = y.astype(jnp.float32)
    mu = jnp.mean(yf, axis=-1, keepdims=True)
    var = jnp.mean(jnp.square(yf - mu), axis=-1, keepdims=True)
    return ((yf - mu) * lax.rsqrt(var + LN_EPS) * g.astype(jnp.float32) + b.astype(jnp.float32)).astype(y.dtype)


def _layer(x, k_cache, v_cache, conv_ctx, h0, w_in, b_in, conv_w, conv_b, w_ga, b_ga, w_gx, b_gx,
           lam, sinks, w_out, ln_g, ln_b):
    B, T = x.shape[0], x.shape[1]
    z = jnp.einsum('btd,de->bte', x, w_in) + b_in
    q, k, v, g_attn, x_lru, g_lru = jnp.split(z, SPLIT_POINTS, axis=-1)
    q = q.reshape(B, T, N_KV_HEADS, GQA_GROUP, HEAD_DIM)
    k = k.reshape(B, T, N_KV_HEADS, HEAD_DIM)
    v = v.reshape(B, T, N_KV_HEADS, HEAD_DIM)

    if k_cache is None:
        o_attn = _attn_prompt(q, k, v, sinks)
        k_all, v_all = k, v
    else:
        k_all = jnp.concatenate([k_cache.astype(k.dtype), k], axis=1)
        v_all = jnp.concatenate([v_cache.astype(v.dtype), v], axis=1)
        o_attn = _attn_sample(q, k_all, v_all, sinks)
    k_win = k_all[:, -WINDOW:]
    v_win = v_all[:, -WINDOW:]

    conv_in = jnp.concatenate([conv_ctx.astype(x_lru.dtype), x_lru], axis=1)
    xc = conv_b + sum(conv_in[:, j:j + T] * conv_w[j] for j in range(CONV_WIDTH))
    h = _rglru(xc, h0, w_ga, b_ga, w_gx, b_gx, lam)
    conv_new = conv_in[:, -(CONV_WIDTH - 1):]
    h_new = h[:, -1].astype(x.dtype)

    u = jnp.concatenate([o_attn.astype(x.dtype) * jax.nn.silu(g_attn),
                         h.astype(x.dtype) * jax.nn.silu(g_lru)], axis=-1)
    y = _layer_norm(DEEPNORM_ALPHA * x + jnp.einsum('bte,ed->btd', u, w_out), ln_g, ln_b)
    return y, k_win, v_win, conv_new, h_new


def setup_inputs(seed: int = 0) -> dict:
    key = jax.random.key(seed)
    ks = jax.random.split(key, 20)
    kv_buf = min(WINDOW, PAST_LEN)
    f32 = jnp.float32
    a0 = jax.random.uniform(ks[13], (DEPTH, D_LRU), f32, 0.9, 0.999)
    return {
        "x_prompt": jax.random.normal(ks[0], (BATCH, SEQ, D_MODEL), f32),
        "x_sample": jax.random.normal(ks[1], (DEC_BATCH, DEC_SEQ, D_MODEL), f32),
        "cache_k": jax.random.normal(ks[2], (DEPTH, DEC_BATCH, kv_buf, N_KV_HEADS, HEAD_DIM), f32),
        "cache_v": jax.random.normal(ks[3], (DEPTH, DEC_BATCH, kv_buf, N_KV_HEADS, HEAD_DIM), f32),
        "state_conv": jax.random.normal(ks[4], (DEPTH, DEC_BATCH, CONV_WIDTH - 1, D_LRU), f32),
        "state_h": jax.random.normal(ks[5], (DEPTH, DEC_BATCH, D_LRU), f32),
        "w_in": jax.random.normal(ks[6], (DEPTH, D_MODEL, D_IN), f32) * D_MODEL ** -0.5,
        "b_in": jax.random.normal(ks[7], (DEPTH, D_IN), f32) * 0.02,
        "conv_w": jax.random.normal(ks[8], (DEPTH, CONV_WIDTH, D_LRU), f32) * CONV_WIDTH ** -0.5,
        "conv_b": jax.random.normal(ks[9], (DEPTH, D_LRU), f32) * 0.02,
        "w_gate_a": jax.random.normal(ks[10], (DEPTH, N_LRU_BLOCKS, LRU_BLOCK, LRU_BLOCK), f32) * LRU_BLOCK ** -0.5,
        "b_gate_a": jax.random.normal(ks[11], (DEPTH, N_LRU_BLOCKS, LRU_BLOCK), f32) * 0.02,
        "w_gate_x": jax.random.normal(ks[12], (DEPTH, N_LRU_BLOCKS, LRU_BLOCK, LRU_BLOCK), f32) * LRU_BLOCK ** -0.5,
        "b_gate_x": jax.random.normal(ks[14], (DEPTH, N_LRU_BLOCKS, LRU_BLOCK), f32) * 0.02,
        "lru_lambda": jnp.log(a0) - jnp.log1p(-a0),
        "attn_sinks": jax.random.normal(ks[15], (DEPTH, N_Q_HEADS), f32),
        "w_out": jax.random.normal(ks[16], (DEPTH, D_MIX, D_MODEL), f32) * (D_MIX ** -0.5) * DEEPNORM_BETA,
        "ln_g": 1.0 + 0.02 * jax.random.normal(ks[17], (DEPTH, D_MODEL), f32),
        "ln_b": 0.02 * jax.random.normal(ks[18], (DEPTH, D_MODEL), f32),
    }


def reference(x_prompt, x_sample, cache_k, cache_v, state_conv, state_h, w_in, b_in, conv_w, conv_b,
              w_gate_a, b_gate_a, w_gate_x, b_gate_x, lru_lambda, attn_sinks, w_out, ln_g, ln_b):
    yp, ys = x_prompt, x_sample
    pk, pv, pc, ph = [], [], [], []
    sk, sv, sc, sh = [], [], [], []
    for l in range(DEPTH):
        w = (w_in[l], b_in[l], conv_w[l], conv_b[l], w_gate_a[l], b_gate_a[l], w_gate_x[l], b_gate_x[l],
             lru_lambda[l], attn_sinks[l], w_out[l], ln_g[l], ln_b[l])
        conv0 = jnp.zeros((yp.shape[0], CONV_WIDTH - 1, D_LRU), yp.dtype)
        h0 = jnp.zeros((yp.shape[0], D_LRU), jnp.float32)
        yp, k1, v1, c1, h1 = _layer(yp, None, None, conv0, h0, *w)
        ys, k2, v2, c2, h2 = _layer(ys, cache_k[l], cache_v[l], state_conv[l], state_h[l], *w)
        pk.append(k1); pv.append(v1); pc.append(c1); ph.append(h1)
        sk.append(k2); sv.append(v2); sc.append(c2); sh.append(h2)
    return (yp, ys, jnp.stack(pk), jnp.stack(pv), jnp.stack(pc), jnp.stack(ph),
            jnp.stack(sk), jnp.stack(sv), jnp.stack(sc), jnp.stack(sh))
```

```python
import functools

import jax
import jax.numpy as jnp
from jax import lax
from jax.experimental import pallas as pl
from jax.experimental.pallas import tpu as pltpu

F32 = jnp.float32
BF16 = jnp.bfloat16

D_MODEL = 2048
DEPTH = 1
CHUNK = 64
WINDOW = 128
WINDOW_CHUNKS = WINDOW // CHUNK
HEAD_DIM = 64
D_ATTN = D_MODEL // 2
D_LRU = D_MODEL - D_ATTN
N_Q_HEADS = D_ATTN // HEAD_DIM
N_KV_HEADS = 4
GQA_GROUP = N_Q_HEADS // N_KV_HEADS
D_KV = N_KV_HEADS * HEAD_DIM
N_LRU_BLOCKS = 16
LRU_BLOCK = D_LRU // N_LRU_BLOCKS
CONV_WIDTH = 4
LRU_C = 8.0
D_IN = 2 * D_ATTN + 2 * D_KV + 2 * D_LRU
PAST_LEN = 2048
DEEPNORM_ALPHA = (2.0 * DEPTH) ** 0.25
LN_EPS = 1e-5
NEG_INF = -1e30

OFF_Q = 0
OFF_K = D_ATTN
OFF_V = D_ATTN + D_KV
OFF_GA = D_ATTN + 2 * D_KV
OFF_XL = OFF_GA + D_ATTN
OFF_GL = OFF_XL + D_LRU

SUBLANES = 8
LANES = 128
MXU_DIM = 256
VMEM_LIMIT_BYTES = 56 * 1024 * 1024

GROUP_W = GQA_GROUP * HEAD_DIM
KEY_PAD = 2 * LANES
GATE_TILE = MXU_DIM
N_GATE_TILES = D_LRU // GATE_TILE
ROW_TILE = 256
N_TILE = 512

_SLOPES = tuple(2.0 ** (-8.0 * h / N_Q_HEADS) for h in range(1, N_Q_HEADS + 1))


def _resident(shape):
    return pl.BlockSpec(shape, lambda *_: (0,) * len(shape), pipeline_mode=pl.Buffered(1))


def _in_proj_kernel(x_ref, w_ref, b_ref, z_ref):
    xb = x_ref[...].astype(BF16)
    for n in range(0, D_IN, N_TILE):
        z_ref[:, n:n + N_TILE] = (
            jnp.dot(xb, w_ref[:, n:n + N_TILE], preferred_element_type=F32) + b_ref[:, n:n + N_TILE])


def _in_proj(x2d, w_bf, b_row):
    m = x2d.shape[0]
    tm = min(ROW_TILE, m)
    return pl.pallas_call(
        _in_proj_kernel,
        grid=(m // tm,),
        in_specs=[pl.BlockSpec((tm, D_MODEL), lambda i: (i, 0)),
                  _resident((D_MODEL, D_IN)),
                  _resident((1, D_IN))],
        out_specs=pl.BlockSpec((tm, D_IN), lambda i: (i, 0)),
        out_shape=jax.ShapeDtypeStruct((m, D_IN), F32),
        compiler_params=pltpu.CompilerParams(
            dimension_semantics=("arbitrary",), vmem_limit_bytes=VMEM_LIMIT_BYTES),
        name="in_proj",
    )(x2d, w_bf, b_row)


def _out_proj_kernel(u_ref, x_ref, w_ref, g_ref, b_ref, y_ref):
    y = DEEPNORM_ALPHA * x_ref[...] + jnp.dot(u_ref[...], w_ref[...], preferred_element_type=F32)
    mu = jnp.mean(y, axis=-1, keepdims=True)
    yc = y - mu
    var = jnp.mean(yc * yc, axis=-1, keepdims=True)
    y_ref[...] = yc * lax.rsqrt(var + LN_EPS) * g_ref[...] + b_ref[...]


def _out_proj(u2d, x2d, w_bf, g_row, b_row):
    m = x2d.shape[0]
    tm = min(ROW_TILE, m)
    return pl.pallas_call(
        _out_proj_kernel,
        grid=(m // tm,),
        in_specs=[pl.BlockSpec((tm, D_MODEL), lambda i: (i, 0)),
                  pl.BlockSpec((tm, D_MODEL), lambda i: (i, 0)),
                  _resident((D_MODEL, D_MODEL)),
                  _resident((1, D_MODEL)),
                  _resident((1, D_MODEL))],
        out_specs=pl.BlockSpec((tm, D_MODEL), lambda i: (i, 0)),
        out_shape=jax.ShapeDtypeStruct((m, D_MODEL), F32),
        compiler_params=pltpu.CompilerParams(
            dimension_semantics=("arbitrary",), vmem_limit_bytes=VMEM_LIMIT_BYTES),
        name="out_proj",
    )(u2d, x2d, w_bf, g_row, b_row)


def _sigmoid(x):
    return 1.0 / (1.0 + jnp.exp(-x))


def _bias_table(nq, kv, key0_pos, q0_pos, n_keys):
    rows = GQA_GROUP * nq
    r = lax.broadcasted_iota(jnp.int32, (rows, KEY_PAD), 0)
    c = lax.broadcasted_iota(jnp.int32, (rows, KEY_PAD), 1)
    shift = nq.bit_length() - 1
    g = jnp.right_shift(r, shift)
    qpos = q0_pos + (r & (nq - 1))
    kpos = key0_pos + c
    slope = jnp.full((rows, KEY_PAD), _SLOPES[kv * GQA_GROUP], F32)
    for gg in range(1, GQA_GROUP):
        slope = jnp.where(g == gg, _SLOPES[kv * GQA_GROUP + gg], slope)
    dist = jnp.abs(qpos - kpos).astype(F32)
    cshift = CHUNK.bit_length() - 1
    qc = jnp.right_shift(qpos, cshift)
    kc = jnp.right_shift(kpos, cshift)
    valid = (kpos >= 0) & (kc <= qc) & (kc >= qc - WINDOW_CHUNKS) & (c < n_keys)
    return jnp.where(valid, -(slope * dist), NEG_INF)


def _tile_heads(x, kv):
    pair = x[:, (kv // 2) * LANES:(kv // 2 + 1) * LANES]
    rolled = pltpu.roll(pair, HEAD_DIM, axis=1)
    lane = lax.broadcasted_iota(jnp.int32, pair.shape, 1)
    low = lane < HEAD_DIM
    both = jnp.where(low, pair, rolled) if kv % 2 == 0 else jnp.where(low, rolled, pair)
    both = both.astype(BF16)
    return jnp.concatenate([both, both], axis=1)


def _attn_unit(q, k4, v4, bias, sinks_ref, kv):
    nq = q.shape[0]
    qb = (q * (HEAD_DIM ** -0.5)).astype(BF16)
    lane = lax.broadcasted_iota(jnp.int32, (nq, GROUP_W), 1)
    head_of_lane = jnp.right_shift(lane, HEAD_DIM.bit_length() - 1)
    zero = jnp.zeros_like(qb)
    qs = jnp.concatenate([jnp.where(head_of_lane == g, qb, zero) for g in range(GQA_GROUP)], axis=0)
    s = lax.dot_general(qs, k4, (((1,), (1,)), ((), ())), preferred_element_type=F32) + bias
    sink = jnp.concatenate(
        [jnp.full((nq, 1), sinks_ref[kv * GQA_GROUP + g], F32) for g in range(GQA_GROUP)], axis=0)
    m = jnp.maximum(jnp.max(s, axis=-1, keepdims=True), sink)
    e = jnp.exp(s - m)
    den = jnp.sum(e, axis=-1, keepdims=True) + jnp.exp(sink - m)
    o4 = jnp.dot(e.astype(BF16), v4, preferred_element_type=F32) * (1.0 / den)
    out = jnp.zeros((nq, GROUP_W), F32)
    for g in range(GQA_GROUP):
        out = out + jnp.where(head_of_lane == g, o4[g * nq:(g + 1) * nq], 0.0)
    return out


def _lru(xl, gl, t_rows, xbuf, a_s, b_s, h_s, hc, cw_ref, cb_ref, wg_ref, bga_ref, bgx_ref, lam_ref, u_ref):
    xbuf[SUBLANES:SUBLANES + t_rows, :] = xl
    xc = cb_ref[...] + cw_ref[CONV_WIDTH - 1:CONV_WIDTH, :] * xl
    for j in range(CONV_WIDTH - 1):
        off = SUBLANES - (CONV_WIDTH - 1) + j
        xc = xc + cw_ref[j:j + 1, :] * xbuf[off:off + t_rows, :]
    lam = lam_ref[...]
    neg = -lam
    softplus = jnp.maximum(neg, 0.0) + jnp.log1p(jnp.exp(-jnp.abs(neg)))
    coef = -LRU_C * softplus
    for j in range(N_GATE_TILES):
        cols = slice(j * GATE_TILE, (j + 1) * GATE_TILE)
        xcj = xc[:, cols]
        pre = jnp.dot(xcj.astype(BF16), wg_ref[j], preferred_element_type=F32)
        r = _sigmoid(pre[:, :GATE_TILE] + bga_ref[:, cols])
        i = _sigmoid(pre[:, GATE_TILE:] + bgx_ref[:, cols])
        a = jnp.exp(coef[:, cols] * r)
        mult = jnp.sqrt(jnp.maximum(1.0 - a * a, 0.0))
        a_s[:, cols] = a
        b_s[:, cols] = mult * (i * xcj)

    row = lax.broadcasted_iota(jnp.int32, (SUBLANES, D_LRU), 0)

    def group(gi, h_prev):
        r0 = pl.multiple_of(gi * SUBLANES, SUBLANES)
        a8 = a_s[pl.ds(r0, SUBLANES), :]
        b8 = b_s[pl.ds(r0, SUBLANES), :]
        for d in (1, 2, 4):
            keep = row >= d
            a_sh = pltpu.roll(a8, d, axis=0)
            b_sh = pltpu.roll(b8, d, axis=0)
            b8 = jnp.where(keep, a8 * b_sh + b8, b8)
            a8 = jnp.where(keep, a8 * a_sh, a8)
        h8 = a8 * h_prev + b8
        h_s[pl.ds(r0, SUBLANES), :] = h8
        return jnp.broadcast_to(h8[SUBLANES - 1:SUBLANES, :], (SUBLANES, D_LRU))

    h_last = lax.fori_loop(0, t_rows // SUBLANES, group, hc[...], unroll=2)
    hc[...] = h_last
    u_ref[0, :, D_ATTN:] = (h_s[...] * (gl * _sigmoid(gl))).astype(u_ref.dtype)
    xbuf[0:SUBLANES, :] = xbuf[t_rows:t_rows + SUBLANES, :]


N_CHUNKS_PER_TILE = ROW_TILE // CHUNK
KBUF_ROWS = WINDOW + ROW_TILE + (KEY_PAD - (WINDOW + CHUNK))


def _prompt_mixer_kernel(sinks_ref, z_ref, cw_ref, cb_ref, wg_ref, bga_ref, bgx_ref, lam_ref,
                         u_ref, kwin_ref, vwin_ref, conv_ref, hout_ref,
                         bias_s, k4_s, v4_s, xbuf, a_s, b_s, h_s, hc):
    b = pl.program_id(0)
    t = pl.program_id(1)
    n_t = pl.num_programs(1)

    @pl.when((b == 0) & (t == 0))
    def _init_tables():
        for kv in range(N_KV_HEADS):
            for var in range(WINDOW_CHUNKS + 1):
                q0 = var * CHUNK
                bias_s[var, kv] = _bias_table(CHUNK, kv, q0 - WINDOW, q0, WINDOW + CHUNK)
        k4_s[...] = jnp.zeros_like(k4_s)
        v4_s[...] = jnp.zeros_like(v4_s)

    @pl.when(t == 0)
    def _reset_carry():
        k4_s[:, 0:WINDOW, :] = jnp.zeros((N_KV_HEADS, WINDOW, GROUP_W), BF16)
        v4_s[:, 0:WINDOW, :] = jnp.zeros((N_KV_HEADS, WINDOW, GROUP_W), BF16)
        xbuf[0:SUBLANES, :] = jnp.zeros((SUBLANES, D_LRU), F32)
        hc[...] = jnp.zeros_like(hc)

    k_new = z_ref[0, :, OFF_K:OFF_K + D_KV]
    v_new = z_ref[0, :, OFF_V:OFF_V + D_KV]
    for kv in range(N_KV_HEADS):
        k4_s[kv, WINDOW:WINDOW + ROW_TILE, :] = _tile_heads(k_new, kv)
        v4_s[kv, WINDOW:WINDOW + ROW_TILE, :] = _tile_heads(v_new, kv)

    for ci in range(N_CHUNKS_PER_TILE):
        rows = slice(ci * CHUNK, (ci + 1) * CHUNK)
        if ci < WINDOW_CHUNKS:
            var = jnp.where(t == 0, ci, WINDOW_CHUNKS)
        else:
            var = WINDOW_CHUNKS
        for kv in range(N_KV_HEADS):
            cols = slice(OFF_Q + kv * GROUP_W, OFF_Q + (kv + 1) * GROUP_W)
            o = _attn_unit(z_ref[0, rows, cols],
                           k4_s[kv, ci * CHUNK:ci * CHUNK + KEY_PAD, :],
                           v4_s[kv, ci * CHUNK:ci * CHUNK + KEY_PAD, :],
                           bias_s[var, kv], sinks_ref, kv)
            ga = z_ref[0, rows, OFF_GA + kv * GROUP_W:OFF_GA + (kv + 1) * GROUP_W]
            u_ref[0, rows, kv * GROUP_W:(kv + 1) * GROUP_W] = (o * (ga * _sigmoid(ga))).astype(u_ref.dtype)

    k4_s[:, 0:WINDOW, :] = k4_s[:, ROW_TILE:ROW_TILE + WINDOW, :]
    v4_s[:, 0:WINDOW, :] = v4_s[:, ROW_TILE:ROW_TILE + WINDOW, :]

    _lru(z_ref[0, :, OFF_XL:OFF_XL + D_LRU], z_ref[0, :, OFF_GL:OFF_GL + D_LRU], ROW_TILE,
         xbuf, a_s, b_s, h_s, hc, cw_ref, cb_ref, wg_ref, bga_ref, bgx_ref, lam_ref, u_ref)

    @pl.when(t == n_t - 1)
    def _emit_state():
        kwin_ref[0] = k_new[ROW_TILE - WINDOW:, :]
        vwin_ref[0] = v_new[ROW_TILE - WINDOW:, :]
        conv_ref[0] = xbuf[SUBLANES - (CONV_WIDTH - 1):SUBLANES, :]
        hout_ref[0] = hc[0:1, :]


def _prompt_mixer(z3, sinks, cw, cb, wg, bga, bgx, lam):
    bsz, seq, _ = z3.shape
    n_t = seq // ROW_TILE
    smem = pl.BlockSpec(memory_space=pltpu.SMEM)
    return pl.pallas_call(
        _prompt_mixer_kernel,
        grid=(bsz, n_t),
        in_specs=[smem,
                  pl.BlockSpec((1, ROW_TILE, D_IN), lambda b, t: (b, t, 0)),
                  _resident((CONV_WIDTH, D_LRU)), _resident((1, D_LRU)),
                  _resident((N_GATE_TILES, GATE_TILE, 2 * GATE_TILE)),
                  _resident((1, D_LRU)), _resident((1, D_LRU)), _resident((1, D_LRU))],
        out_specs=[pl.BlockSpec((1, ROW_TILE, D_MODEL), lambda b, t: (b, t, 0)),
                   pl.BlockSpec((1, WINDOW, D_KV), lambda b, t: (b, 0, 0)),
                   pl.BlockSpec((1, WINDOW, D_KV), lambda b, t: (b, 0, 0)),
                   pl.BlockSpec((1, CONV_WIDTH - 1, D_LRU), lambda b, t: (b, 0, 0)),
                   pl.BlockSpec((1, 1, D_LRU), lambda b, t: (b, 0, 0))],
        out_shape=[jax.ShapeDtypeStruct((bsz, seq, D_MODEL), BF16),
                   jax.ShapeDtypeStruct((bsz, WINDOW, D_KV), F32),
                   jax.ShapeDtypeStruct((bsz, WINDOW, D_KV), F32),
                   jax.ShapeDtypeStruct((bsz, CONV_WIDTH - 1, D_LRU), F32),
                   jax.ShapeDtypeStruct((bsz, 1, D_LRU), F32)],
        scratch_shapes=[pltpu.VMEM((WINDOW_CHUNKS + 1, N_KV_HEADS, GQA_GROUP * CHUNK, KEY_PAD), F32),
                        pltpu.VMEM((N_KV_HEADS, KBUF_ROWS, GROUP_W), BF16),
                        pltpu.VMEM((N_KV_HEADS, KBUF_ROWS, GROUP_W), BF16),
                        pltpu.VMEM((SUBLANES + ROW_TILE, D_LRU), F32),
                        pltpu.VMEM((ROW_TILE, D_LRU), F32),
                        pltpu.VMEM((ROW_TILE, D_LRU), F32),
                        pltpu.VMEM((ROW_TILE, D_LRU), F32),
                        pltpu.VMEM((SUBLANES, D_LRU), F32)],
        compiler_params=pltpu.CompilerParams(
            dimension_semantics=("arbitrary", "arbitrary"), vmem_limit_bytes=VMEM_LIMIT_BYTES),
        name="prompt_mixer",
    )(sinks, z3, cw, cb, wg, bga, bgx, lam)


def _sample_mixer_kernel(sinks_ref, z_ref, ck_ref, cv_ref, sc_ref, sh_ref,
                         cw_ref, cb_ref, wg_ref, bga_ref, bgx_ref, lam_ref,
                         u_ref, kwin_ref, vwin_ref, conv_ref, hout_ref,
                         bias_s, k4_s, v4_s, xbuf, a_s, b_s, h_s, hc, *, t_rows, n_cache):
    b = pl.program_id(0)
    n_keys = n_cache + t_rows

    @pl.when(b == 0)
    def _init_tables():
        for kv in range(N_KV_HEADS):
            bias_s[kv] = _bias_table(t_rows, kv, PAST_LEN - n_cache, PAST_LEN, n_keys)
        k4_s[...] = jnp.zeros_like(k4_s)
        v4_s[...] = jnp.zeros_like(v4_s)

    k_new = z_ref[0, :, OFF_K:OFF_K + D_KV]
    v_new = z_ref[0, :, OFF_V:OFF_V + D_KV]
    k_all = jnp.concatenate([ck_ref[0], k_new], axis=0)
    v_all = jnp.concatenate([cv_ref[0], v_new], axis=0)
    for kv in range(N_KV_HEADS):
        k4_s[kv, 0:n_keys, :] = _tile_heads(k_all, kv)
        v4_s[kv, 0:n_keys, :] = _tile_heads(v_all, kv)
    for kv in range(N_KV_HEADS):
        cols = slice(OFF_Q + kv * GROUP_W, OFF_Q + (kv + 1) * GROUP_W)
        o = _attn_unit(z_ref[0, :, cols], k4_s[kv], v4_s[kv], bias_s[kv], sinks_ref, kv)
        ga = z_ref[0, :, OFF_GA + kv * GROUP_W:OFF_GA + (kv + 1) * GROUP_W]
        u_ref[0, :, kv * GROUP_W:(kv + 1) * GROUP_W] = (o * (ga * _sigmoid(ga))).astype(u_ref.dtype)
    kwin_ref[0] = k_all[n_keys - WINDOW:, :]
    vwin_ref[0] = v_all[n_keys - WINDOW:, :]

    xbuf[0:SUBLANES, :] = jnp.zeros((SUBLANES, D_LRU), F32)
    xbuf[SUBLANES - (CONV_WIDTH - 1):SUBLANES, :] = sc_ref[0]
    hc[...] = jnp.broadcast_to(sh_ref[0], (SUBLANES, D_LRU))
    _lru(z_ref[0, :, OFF_XL:OFF_XL + D_LRU], z_ref[0, :, OFF_GL:OFF_GL + D_LRU], t_rows,
         xbuf, a_s, b_s, h_s, hc, cw_ref, cb_ref, wg_ref, bga_ref, bgx_ref, lam_ref, u_ref)
    conv_ref[0] = xbuf[SUBLANES - (CONV_WIDTH - 1):SUBLANES, :]
    hout_ref[0] = hc[0:1, :]


def _sample_mixer(z3, cache_k, cache_v, state_conv, state_h, sinks, cw, cb, wg, bga, bgx, lam):
    bsz, t_rows, _ = z3.shape
    n_cache = cache_k.shape[1]
    assert n_cache + t_rows <= KEY_PAD and n_cache + t_rows >= WINDOW
    assert t_rows % SUBLANES == 0 and t_rows & (t_rows - 1) == 0 and t_rows >= CONV_WIDTH - 1
    smem = pl.BlockSpec(memory_space=pltpu.SMEM)
    per_b = lambda shape: pl.BlockSpec((1,) + shape, lambda b: (b, 0, 0))
    return pl.pallas_call(
        functools.partial(_sample_mixer_kernel, t_rows=t_rows, n_cache=n_cache),
        grid=(bsz,),
        in_specs=[smem, per_b((t_rows, D_IN)), per_b((n_cache, D_KV)), per_b((n_cache, D_KV)),
                  per_b((CONV_WIDTH - 1, D_LRU)), per_b((1, D_LRU)),
                  _resident((CONV_WIDTH, D_LRU)), _resident((1, D_LRU)),
                  _resident((N_GATE_TILES, GATE_TILE, 2 * GATE_TILE)),
                  _resident((1, D_LRU)), _resident((1, D_LRU)), _resident((1, D_LRU))],
        out_specs=[per_b((t_rows, D_MODEL)), per_b((WINDOW, D_KV)), per_b((WINDOW, D_KV)),
                   per_b((CONV_WIDTH - 1, D_LRU)), per_b((1, D_LRU))],
        out_shape=[jax.ShapeDtypeStruct((bsz, t_rows, D_MODEL), BF16),
                   jax.ShapeDtypeStruct((bsz, WINDOW, D_KV), F32),
                   jax.ShapeDtypeStruct((bsz, WINDOW, D_KV), F32),
                   jax.ShapeDtypeStruct((bsz, CONV_WIDTH - 1, D_LRU), F32),
                   jax.ShapeDtypeStruct((bsz, 1, D_LRU), F32)],
        scratch_shapes=[pltpu.VMEM((N_KV_HEADS, GQA_GROUP * t_rows, KEY_PAD), F32),
                        pltpu.VMEM((N_KV_HEADS, KEY_PAD, GROUP_W), BF16),
                        pltpu.VMEM((N_KV_HEADS, KEY_PAD, GROUP_W), BF16),
                        pltpu.VMEM((SUBLANES + t_rows, D_LRU), F32),
                        pltpu.VMEM((t_rows, D_LRU), F32),
                        pltpu.VMEM((t_rows, D_LRU), F32),
                        pltpu.VMEM((t_rows, D_LRU), F32),
                        pltpu.VMEM((SUBLANES, D_LRU), F32)],
        compiler_params=pltpu.CompilerParams(
            dimension_semantics=("arbitrary",), vmem_limit_bytes=VMEM_LIMIT_BYTES),
        name="sample_mixer",
    )(sinks, z3, cache_k, cache_v, state_conv, state_h, cw, cb, wg, bga, bgx, lam)


def _block_diag_tiles(w):
    per = GATE_TILE // LRU_BLOCK
    w4 = w.reshape(N_GATE_TILES, per, LRU_BLOCK, LRU_BLOCK)
    eye = jnp.eye(per, dtype=w.dtype)
    return jnp.einsum('tade,ab->tadbe', w4, eye).reshape(N_GATE_TILES, GATE_TILE, GATE_TILE)


def kernel(x_prompt, x_sample, cache_k, cache_v, state_conv, state_h, w_in, b_in, conv_w, conv_b,
           w_gate_a, b_gate_a, w_gate_x, b_gate_x, lru_lambda, attn_sinks, w_out, ln_g, ln_b):
    assert w_in.shape[0] == DEPTH == 1
    bsz, seq, _ = x_prompt.shape
    dbsz, dseq, _ = x_sample.shape
    n_cache = cache_k.shape[2]
    assert seq % ROW_TILE == 0 and (dbsz * dseq) % min(ROW_TILE, dbsz * dseq) == 0

    w_in_bf = w_in[0].astype(BF16)
    w_out_bf = w_out[0].astype(BF16)
    b_in_row = b_in[0].reshape(1, D_IN)
    wg = jnp.concatenate([_block_diag_tiles(w_gate_a[0]), _block_diag_tiles(w_gate_x[0])], axis=-1).astype(BF16)
    bga = b_gate_a[0].reshape(1, D_LRU)
    bgx = b_gate_x[0].reshape(1, D_LRU)
    lam = lru_lambda[0].reshape(1, D_LRU)
    cw = conv_w[0]
    cb = conv_b[0].reshape(1, D_LRU)
    sinks = attn_sinks[0]
    ln_g_row = ln_g[0].reshape(1, D_MODEL)
    ln_b_row = ln_b[0].reshape(1, D_MODEL)

    xp2 = x_prompt.reshape(bsz * seq, D_MODEL)
    xs2 = x_sample.reshape(dbsz * dseq, D_MODEL)

    zp = _in_proj(xp2, w_in_bf, b_in_row).reshape(bsz, seq, D_IN)
    zs = _in_proj(xs2, w_in_bf, b_in_row).reshape(dbsz, dseq, D_IN)

    up, kp, vp, cp, hp = _prompt_mixer(zp, sinks, cw, cb, wg, bga, bgx, lam)
    us, ks, vs, cs, hs = _sample_mixer(
        zs, cache_k[0].reshape(dbsz, n_cache, D_KV), cache_v[0].reshape(dbsz, n_cache, D_KV),
        state_conv[0], state_h[0].reshape(dbsz, 1, D_LRU), sinks, cw, cb, wg, bga, bgx, lam)

    yp = _out_proj(up.reshape(bsz * seq, D_MODEL), xp2, w_out_bf, ln_g_row, ln_b_row).reshape(bsz, seq, D_MODEL)
    ys = _out_proj(us.reshape(dbsz * dseq, D_MODEL), xs2, w_out_bf, ln_g_row, ln_b_row).reshape(dbsz, dseq, D_MODEL)

    kv_shape = lambda n: (1, n, WINDOW, N_KV_HEADS, HEAD_DIM)
    return (yp, ys,
            kp.reshape(kv_shape(bsz)), vp.reshape(kv_shape(bsz)),
            cp.reshape(1, bsz, CONV_WIDTH - 1, D_LRU), hp.reshape(1, bsz, D_LRU),
            ks.reshape(kv_shape(dbsz)), vs.reshape(kv_shape(dbsz)),
            cs.reshape(1, dbsz, CONV_WIDTH - 1, D_LRU), hs.reshape(1, dbsz, D_LRU))
```

```python
import functools

import jax
import jax.numpy as jnp
from jax import lax
from jax.experimental import pallas as pl
from jax.experimental.pallas import tpu as pltpu

F32 = jnp.float32
BF16 = jnp.bfloat16

D_MODEL = 2048
DEPTH = 1
CHUNK = 64
WINDOW = 128
WINDOW_CHUNKS = WINDOW // CHUNK
HEAD_DIM = 64
D_ATTN = D_MODEL // 2
D_LRU = D_MODEL - D_ATTN
N_Q_HEADS = D_ATTN // HEAD_DIM
N_KV_HEADS = 4
GQA_GROUP = N_Q_HEADS // N_KV_HEADS
D_KV = N_KV_HEADS * HEAD_DIM
N_LRU_BLOCKS = 16
LRU_BLOCK = D_LRU // N_LRU_BLOCKS
CONV_WIDTH = 4
LRU_C = 8.0
D_IN = 2 * D_ATTN + 2 * D_KV + 2 * D_LRU
PAST_LEN = 2048
DEEPNORM_ALPHA = (2.0 * DEPTH) ** 0.25
LN_EPS = 1e-5
NEG_INF = -1e30

OFF_Q = 0
OFF_K = D_ATTN
OFF_V = D_ATTN + D_KV
OFF_GA = D_ATTN + 2 * D_KV
OFF_XL = OFF_GA + D_ATTN
OFF_GL = OFF_XL + D_LRU

SUBLANES = 8
LANES = 128
MXU_DIM = 256
VMEM_LIMIT_BYTES = 56 * 1024 * 1024

GROUP_W = GQA_GROUP * HEAD_DIM
KEY_PAD = 2 * LANES
V_EXT_W = GROUP_W + LANES
GATE_TILE = MXU_DIM
N_GATE_TILES = D_LRU // GATE_TILE
ROW_TILE = 256
N_TILE = 512
N_CHUNKS_PER_TILE = ROW_TILE // CHUNK
KBUF_ROWS = WINDOW + ROW_TILE + (KEY_PAD - (WINDOW + CHUNK))

_SLOPES = tuple(2.0 ** (-8.0 * h / N_Q_HEADS) for h in range(1, N_Q_HEADS + 1))


def _resident(shape):
    return pl.BlockSpec(shape, lambda *_: (0,) * len(shape), pipeline_mode=pl.Buffered(1))


class _ColGroups:
    def __init__(self, groups):
        self._groups = groups

    def _locate(self, idx):
        rows, cols = idx
        for first, end, ref in self._groups:
            if first <= cols.start and cols.stop <= end:
                return ref, rows, slice(cols.start - first, cols.stop - first)
        raise ValueError(f"columns {cols} straddle projection groups")

    def __getitem__(self, idx):
        ref, rows, cols = self._locate(idx)
        return ref[rows, cols]

    def __setitem__(self, idx, value):
        ref, rows, cols = self._locate(idx)
        ref[rows, cols] = value


_ALL_ROWS = slice(None)
_PROJ_GROUPS = ((OFF_Q, OFF_K), (OFF_K, OFF_GA), (OFF_GA, OFF_XL), (OFF_XL, OFF_GL), (OFF_GL, D_IN))


def _sigmoid(x):
    return 1.0 / (1.0 + jnp.exp(-x))


def _in_proj_into(xb, w_ref, b_ref, z, first=0, end=D_IN, width=N_TILE):
    for n in range(first, end, width):
        z[_ALL_ROWS, n:n + width] = (
            jnp.dot(xb, w_ref[:, n:n + width], preferred_element_type=F32) + b_ref[:, n:n + width])


def _out_proj_ln(u, x, w_ref, g_ref, b_ref):
    y = DEEPNORM_ALPHA * x + jnp.dot(u, w_ref[...], preferred_element_type=F32)
    mu = jnp.mean(y, axis=-1, keepdims=True)
    yc = y - mu
    var = jnp.mean(yc * yc, axis=-1, keepdims=True)
    return yc * lax.rsqrt(var + LN_EPS) * g_ref[...] + b_ref[...]


def _bias_table(nq, kv, key0_pos, q0_pos, n_keys):
    rows = GQA_GROUP * nq
    r = lax.broadcasted_iota(jnp.int32, (rows, KEY_PAD), 0)
    c = lax.broadcasted_iota(jnp.int32, (rows, KEY_PAD), 1)
    shift = nq.bit_length() - 1
    g = jnp.right_shift(r, shift)
    qpos = q0_pos + (r & (nq - 1))
    kpos = key0_pos + c
    slope = jnp.full((rows, KEY_PAD), _SLOPES[kv * GQA_GROUP], F32)
    for gg in range(1, GQA_GROUP):
        slope = jnp.where(g == gg, _SLOPES[kv * GQA_GROUP + gg], slope)
    dist = jnp.abs(qpos - kpos).astype(F32)
    cshift = CHUNK.bit_length() - 1
    qc = jnp.right_shift(qpos, cshift)
    kc = jnp.right_shift(kpos, cshift)
    valid = (kpos >= 0) & (kc <= qc) & (kc >= qc - WINDOW_CHUNKS) & (c < n_keys)
    return jnp.where(valid, -(slope * dist), NEG_INF)


def _tile_heads(x, kv):
    pair = x[:, (kv // 2) * LANES:(kv // 2 + 1) * LANES]
    rolled = pltpu.roll(pair, HEAD_DIM, axis=1)
    lane = lax.broadcasted_iota(jnp.int32, pair.shape, 1)
    low = lane < HEAD_DIM
    both = jnp.where(low, pair, rolled) if kv % 2 == 0 else jnp.where(low, rolled, pair)
    both = both.astype(BF16)
    return jnp.concatenate([both, both], axis=1)


def _sink_table(nq, kv, sinks_ref):
    return jnp.concatenate(
        [jnp.full((nq, LANES), sinks_ref[kv * GQA_GROUP + g], F32) for g in range(GQA_GROUP)], axis=0)


def _attn_unit(q, k4, v4e, bias, sink):
    nq = q.shape[0]
    qb = (q * (HEAD_DIM ** -0.5)).astype(BF16)
    lane = lax.broadcasted_iota(jnp.int32, (nq, GROUP_W), 1)
    head_of_lane = jnp.right_shift(lane, HEAD_DIM.bit_length() - 1)
    zero = jnp.zeros_like(qb)
    qs = jnp.concatenate([jnp.where(head_of_lane == g, qb, zero) for g in range(GQA_GROUP)], axis=0)
    s = lax.dot_general(qs, k4, (((1,), (1,)), ((), ())), preferred_element_type=F32) + bias
    m = jnp.maximum(jnp.broadcast_to(jnp.max(s, axis=-1, keepdims=True), sink.shape), sink)
    e = jnp.concatenate(
        [jnp.exp(s[:, j * LANES:(j + 1) * LANES] - m) for j in range(KEY_PAD // LANES)], axis=1)
    o4e = jnp.dot(e.astype(BF16), v4e, preferred_element_type=F32)
    inv = 1.0 / (o4e[:, GROUP_W:] + jnp.exp(sink - m))
    o4 = o4e[:, :GROUP_W] * jnp.concatenate([inv] * (GROUP_W // LANES), axis=1)
    out = jnp.zeros((nq, GROUP_W), F32)
    for g in range(GQA_GROUP):
        out = out + jnp.where(head_of_lane == g, o4[g * nq:(g + 1) * nq], 0.0)
    return out


def _lru(z, u, t_rows, xbuf, a_s, b_s, hc, cw_ref, cb_ref, wg_ref, bga_ref, bgx_ref, lam_ref):
    xl = z[_ALL_ROWS, OFF_XL:OFF_XL + D_LRU]
    xbuf[SUBLANES:SUBLANES + t_rows, :] = xl
    xc = cb_ref[...] + cw_ref[CONV_WIDTH - 1:CONV_WIDTH, :] * xl
    for j in range(CONV_WIDTH - 1):
        off = SUBLANES - (CONV_WIDTH - 1) + j
        xc = xc + cw_ref[j:j + 1, :] * xbuf[off:off + t_rows, :]
    neg = -lam_ref[...]
    softplus = jnp.maximum(neg, 0.0) + jnp.log1p(jnp.exp(-jnp.abs(neg)))
    coef = -LRU_C * softplus
    for j in range(N_GATE_TILES):
        cols = slice(j * GATE_TILE, (j + 1) * GATE_TILE)
        xcj = xc[:, cols]
        pre = jnp.dot(xcj.astype(BF16), wg_ref[j], preferred_element_type=F32)
        r = _sigmoid(pre[:, :GATE_TILE] + bga_ref[:, cols])
        i = _sigmoid(pre[:, GATE_TILE:] + bgx_ref[:, cols])
        a = jnp.exp(coef[:, cols] * r)
        mult = jnp.sqrt(jnp.maximum(1.0 - a * a, 0.0))
        a_s[:, cols] = a
        b_s[:, cols] = mult * (i * xcj)

    row = lax.broadcasted_iota(jnp.int32, (SUBLANES, D_LRU), 0)
    h_prev = hc[...]
    for gi in range(t_rows // SUBLANES):
        rows = slice(gi * SUBLANES, (gi + 1) * SUBLANES)
        a8 = a_s[rows, :]
        b8 = b_s[rows, :]
        for d in (1, 2, 4):
            keep = row >= d
            a_sh = pltpu.roll(a8, d, axis=0)
            b_sh = pltpu.roll(b8, d, axis=0)
            b8 = jnp.where(keep, a8 * b_sh + b8, b8)
            a8 = jnp.where(keep, a8 * a_sh, a8)
        h8 = a8 * h_prev + b8
        b_s[rows, :] = h8
        h_prev = jnp.broadcast_to(h8[SUBLANES - 1:SUBLANES, :], (SUBLANES, D_LRU))
    hc[...] = h_prev
    gl = z[_ALL_ROWS, OFF_GL:OFF_GL + D_LRU]
    u[:, D_ATTN:] = (b_s[...] * (gl * _sigmoid(gl))).astype(u.dtype)
    xbuf[0:SUBLANES, :] = xbuf[t_rows:t_rows + SUBLANES, :]


def _prompt_kernel(sinks_ref, x_ref, win_ref, bin_ref, cw_ref, cb_ref, wg_ref, bga_ref, bgx_ref, lam_ref,
                   u_ref, kwin_ref, vwin_ref, conv_ref, hout_ref,
                   zq, zkv, zga, zxl, zgl, bias_s, mask_s, sink_s, k4_s, v4_s, xbuf, a_s, b_s, hc):
    z = _ColGroups([(first, end, ref) for (first, end), ref in zip(_PROJ_GROUPS, (zq, zkv, zga, zxl, zgl))])
    u = u_ref.at[0]
    b = pl.program_id(0)
    t = pl.program_id(1)
    n_t = pl.num_programs(1)

    @pl.when((b == 0) & (t == 0))
    def _init_tables():
        for kv in range(N_KV_HEADS):
            bias_s[kv] = _bias_table(CHUNK, kv, 0, WINDOW, WINDOW + CHUNK)
            sink_s[kv] = _sink_table(CHUNK, kv, sinks_ref)
        col = lax.broadcasted_iota(jnp.int32, (SUBLANES, KEY_PAD), 1)
        for ci in range(WINDOW_CHUNKS):
            mask_s[ci] = jnp.where(col < WINDOW - ci * CHUNK, NEG_INF, 0.0)
        k4_s[...] = jnp.zeros_like(k4_s)
        v4_s[:, :, 0:GROUP_W] = jnp.zeros((N_KV_HEADS, KBUF_ROWS, GROUP_W), BF16)
        v4_s[:, :, GROUP_W:] = jnp.ones((N_KV_HEADS, KBUF_ROWS, LANES), BF16)

    @pl.when(t == 0)
    def _reset_carry():
        k4_s[:, 0:WINDOW, :] = jnp.zeros((N_KV_HEADS, WINDOW, GROUP_W), BF16)
        v4_s[:, 0:WINDOW, 0:GROUP_W] = jnp.zeros((N_KV_HEADS, WINDOW, GROUP_W), BF16)
        xbuf[0:SUBLANES, :] = jnp.zeros((SUBLANES, D_LRU), F32)
        hc[...] = jnp.zeros_like(hc)

    xb = x_ref[0].astype(BF16)
    _in_proj_into(xb, win_ref, bin_ref, z, 0, OFF_XL)
    late_cols = list(range(OFF_XL, D_IN, MXU_DIM))

    k_new = z[_ALL_ROWS, OFF_K:OFF_K + D_KV]
    v_new = z[_ALL_ROWS, OFF_V:OFF_V + D_KV]
    for kv in range(N_KV_HEADS):
        k4_s[kv, WINDOW:WINDOW + ROW_TILE, :] = _tile_heads(k_new, kv)
        v4_s[kv, WINDOW:WINDOW + ROW_TILE, 0:GROUP_W] = _tile_heads(v_new, kv)

    for ci in range(N_CHUNKS_PER_TILE):
        rows = slice(ci * CHUNK, (ci + 1) * CHUNK)
        if ci < WINDOW_CHUNKS:
            mask_row = jnp.where(t == 0, mask_s[ci, 0:1, :], 0.0)
        for kv in range(N_KV_HEADS):
            bias = bias_s[kv] + mask_row if ci < WINDOW_CHUNKS else bias_s[kv]
            o = _attn_unit(z[rows, OFF_Q + kv * GROUP_W:OFF_Q + (kv + 1) * GROUP_W],
                           k4_s[kv, ci * CHUNK:ci * CHUNK + KEY_PAD, :],
                           v4_s[kv, ci * CHUNK:ci * CHUNK + KEY_PAD, :],
                           bias, sink_s[kv])
            ga = z[rows, OFF_GA + kv * GROUP_W:OFF_GA + (kv + 1) * GROUP_W]
            u[rows, kv * GROUP_W:(kv + 1) * GROUP_W] = (o * (ga * _sigmoid(ga))).astype(u.dtype)
            if kv % 2 == 1 and late_cols:
                c0 = late_cols.pop(0)
                _in_proj_into(xb, win_ref, bin_ref, z, c0, c0 + MXU_DIM, MXU_DIM)
    assert not late_cols

    k4_s[:, 0:WINDOW, :] = k4_s[:, ROW_TILE:ROW_TILE + WINDOW, :]
    v4_s[:, 0:WINDOW, 0:GROUP_W] = v4_s[:, ROW_TILE:ROW_TILE + WINDOW, 0:GROUP_W]

    _lru(z, u, ROW_TILE, xbuf, a_s, b_s, hc, cw_ref, cb_ref, wg_ref, bga_ref, bgx_ref, lam_ref)

    @pl.when(t == n_t - 1)
    def _emit_state():
        kwin_ref[0] = z[slice(ROW_TILE - WINDOW, ROW_TILE), OFF_K:OFF_K + D_KV]
        vwin_ref[0] = z[slice(ROW_TILE - WINDOW, ROW_TILE), OFF_V:OFF_V + D_KV]
        conv_ref[0] = xbuf[SUBLANES - (CONV_WIDTH - 1):SUBLANES, :]
        hout_ref[0] = hc[0:1, :]


def _prompt_mixer(x3, sinks, w_in_bf, b_in_row, cw, cb, wg, bga, bgx, lam):
    bsz, seq, _ = x3.shape
    n_t = seq // ROW_TILE
    smem = pl.BlockSpec(memory_space=pltpu.SMEM)
    return pl.pallas_call(
        _prompt_kernel,
        grid=(bsz, n_t),
        in_specs=[smem,
                  pl.BlockSpec((1, ROW_TILE, D_MODEL), lambda b, t: (b, t, 0)),
                  _resident((D_MODEL, D_IN)), _resident((1, D_IN)),
                  _resident((CONV_WIDTH, D_LRU)), _resident((1, D_LRU)),
                  _resident((N_GATE_TILES, GATE_TILE, 2 * GATE_TILE)),
                  _resident((1, D_LRU)), _resident((1, D_LRU)), _resident((1, D_LRU))],
        out_specs=[pl.BlockSpec((1, ROW_TILE, D_MODEL), lambda b, t: (b, t, 0)),
                   pl.BlockSpec((1, WINDOW, D_KV), lambda b, t: (b, 0, 0)),
                   pl.BlockSpec((1, WINDOW, D_KV), lambda b, t: (b, 0, 0)),
                   pl.BlockSpec((1, CONV_WIDTH - 1, D_LRU), lambda b, t: (b, 0, 0)),
                   pl.BlockSpec((1, 1, D_LRU), lambda b, t: (b, 0, 0))],
        out_shape=[jax.ShapeDtypeStruct((bsz, seq, D_MODEL), BF16),
                   jax.ShapeDtypeStruct((bsz, WINDOW, D_KV), F32),
                   jax.ShapeDtypeStruct((bsz, WINDOW, D_KV), F32),
                   jax.ShapeDtypeStruct((bsz, CONV_WIDTH - 1, D_LRU), F32),
                   jax.ShapeDtypeStruct((bsz, 1, D_LRU), F32)],
        scratch_shapes=[pltpu.VMEM((ROW_TILE, end - first), F32) for first, end in _PROJ_GROUPS] + [
                        pltpu.VMEM((N_KV_HEADS, GQA_GROUP * CHUNK, KEY_PAD), F32),
                        pltpu.VMEM((WINDOW_CHUNKS, SUBLANES, KEY_PAD), F32),
                        pltpu.VMEM((N_KV_HEADS, GQA_GROUP * CHUNK, LANES), F32),
                        pltpu.VMEM((N_KV_HEADS, KBUF_ROWS, GROUP_W), BF16),
                        pltpu.VMEM((N_KV_HEADS, KBUF_ROWS, V_EXT_W), BF16),
                        pltpu.VMEM((SUBLANES + ROW_TILE, D_LRU), F32),
                        pltpu.VMEM((ROW_TILE, D_LRU), F32),
                        pltpu.VMEM((ROW_TILE, D_LRU), F32),
                        pltpu.VMEM((SUBLANES, D_LRU), F32)],
        compiler_params=pltpu.CompilerParams(
            dimension_semantics=("arbitrary", "arbitrary"), vmem_limit_bytes=VMEM_LIMIT_BYTES),
        name="prompt_mixer",
    )(sinks, x3, w_in_bf, b_in_row, cw, cb, wg, bga, bgx, lam)


def _in_proj_kernel(x_ref, w_ref, b_ref, z_ref):
    _in_proj_into(x_ref[...].astype(BF16), w_ref, b_ref, z_ref)


def _in_proj(x2d, w_bf, b_row):
    m = x2d.shape[0]
    tm = min(ROW_TILE, m)
    return pl.pallas_call(
        _in_proj_kernel,
        grid=(m // tm,),
        in_specs=[pl.BlockSpec((tm, D_MODEL), lambda i: (i, 0)),
                  _resident((D_MODEL, D_IN)),
                  _resident((1, D_IN))],
        out_specs=pl.BlockSpec((tm, D_IN), lambda i: (i, 0)),
        out_shape=jax.ShapeDtypeStruct((m, D_IN), F32),
        compiler_params=pltpu.CompilerParams(
            dimension_semantics=("arbitrary",), vmem_limit_bytes=VMEM_LIMIT_BYTES),
        name="in_proj",
    )(x2d, w_bf, b_row)


def _out_proj_kernel(u_ref, x_ref, w_ref, g_ref, b_ref, y_ref):
    y_ref[...] = _out_proj_ln(u_ref[...], x_ref[...], w_ref, g_ref, b_ref)


def _out_proj(u2d, x2d, w_bf, g_row, b_row):
    m = x2d.shape[0]
    tm = min(ROW_TILE, m)
    return pl.pallas_call(
        _out_proj_kernel,
        grid=(m // tm,),
        in_specs=[pl.BlockSpec((tm, D_MODEL), lambda i: (i, 0)),
                  pl.BlockSpec((tm, D_MODEL), lambda i: (i, 0)),
                  _resident((D_MODEL, D_MODEL)),
                  _resident((1, D_MODEL)),
                  _resident((1, D_MODEL))],
        out_specs=pl.BlockSpec((tm, D_MODEL), lambda i: (i, 0)),
        out_shape=jax.ShapeDtypeStruct((m, D_MODEL), F32),
        compiler_params=pltpu.CompilerParams(
            dimension_semantics=("arbitrary",), vmem_limit_bytes=VMEM_LIMIT_BYTES),
        name="out_proj",
    )(u2d, x2d, w_bf, g_row, b_row)


def _sample_mixer_kernel(sinks_ref, z_ref, ck_ref, cv_ref, sc_ref, sh_ref,
                         cw_ref, cb_ref, wg_ref, bga_ref, bgx_ref, lam_ref,
                         u_ref, kwin_ref, vwin_ref, conv_ref, hout_ref,
                         bias_s, sink_s, k4_s, v4_s, xbuf, a_s, b_s, hc, *, t_rows, n_cache):
    b = pl.program_id(0)
    n_keys = n_cache + t_rows
    z = _ColGroups([(0, D_IN, z_ref.at[0])])
    u = u_ref.at[0]

    @pl.when(b == 0)
    def _init_tables():
        for kv in range(N_KV_HEADS):
            bias_s[kv] = _bias_table(t_rows, kv, PAST_LEN - n_cache, PAST_LEN, n_keys)
            sink_s[kv] = _sink_table(t_rows, kv, sinks_ref)
        k4_s[...] = jnp.zeros_like(k4_s)
        v4_s[:, :, 0:GROUP_W] = jnp.zeros((N_KV_HEADS, KEY_PAD, GROUP_W), BF16)
        v4_s[:, :, GROUP_W:] = jnp.ones((N_KV_HEADS, KEY_PAD, LANES), BF16)

    k_all = jnp.concatenate([ck_ref[0], z[_ALL_ROWS, OFF_K:OFF_K + D_KV]], axis=0)
    v_all = jnp.concatenate([cv_ref[0], z[_ALL_ROWS, OFF_V:OFF_V + D_KV]], axis=0)
    for kv in range(N_KV_HEADS):
        k4_s[kv, 0:n_keys, :] = _tile_heads(k_all, kv)
        v4_s[kv, 0:n_keys, 0:GROUP_W] = _tile_heads(v_all, kv)
    for kv in range(N_KV_HEADS):
        o = _attn_unit(z[_ALL_ROWS, OFF_Q + kv * GROUP_W:OFF_Q + (kv + 1) * GROUP_W],
                       k4_s[kv], v4_s[kv], bias_s[kv], sink_s[kv])
        ga = z[_ALL_ROWS, OFF_GA + kv * GROUP_W:OFF_GA + (kv + 1) * GROUP_W]
        u[:, kv * GROUP_W:(kv + 1) * GROUP_W] = (o * (ga * _sigmoid(ga))).astype(u.dtype)
    kwin_ref[0] = k_all[n_keys - WINDOW:, :]
    vwin_ref[0] = v_all[n_keys - WINDOW:, :]

    xbuf[0:SUBLANES, :] = jnp.zeros((SUBLANES, D_LRU), F32)
    xbuf[SUBLANES - (CONV_WIDTH - 1):SUBLANES, :] = sc_ref[0]
    hc[...] = jnp.broadcast_to(sh_ref[0], (SUBLANES, D_LRU))
    _lru(z, u, t_rows, xbuf, a_s, b_s, hc, cw_ref, cb_ref, wg_ref, bga_ref, bgx_ref, lam_ref)
    conv_ref[0] = xbuf[SUBLANES - (CONV_WIDTH - 1):SUBLANES, :]
    hout_ref[0] = hc[0:1, :]


def _sample_mixer(z3, cache_k, cache_v, state_conv, state_h, sinks, cw, cb, wg, bga, bgx, lam):
    bsz, t_rows, _ = z3.shape
    n_cache = cache_k.shape[1]
    assert n_cache + t_rows <= KEY_PAD and n_cache + t_rows >= WINDOW
    assert t_rows % SUBLANES == 0 and t_rows & (t_rows - 1) == 0 and t_rows >= CONV_WIDTH - 1
    smem = pl.BlockSpec(memory_space=pltpu.SMEM)
    per_b = lambda shape: pl.BlockSpec((1,) + shape, lambda b: (b, 0, 0))
    return pl.pallas_call(
        functools.partial(_sample_mixer_kernel, t_rows=t_rows, n_cache=n_cache),
        grid=(bsz,),
        in_specs=[smem, per_b((t_rows, D_IN)), per_b((n_cache, D_KV)), per_b((n_cache, D_KV)),
                  per_b((CONV_WIDTH - 1, D_LRU)), per_b((1, D_LRU)),
                  _resident((CONV_WIDTH, D_LRU)), _resident((1, D_LRU)),
                  _resident((N_GATE_TILES, GATE_TILE, 2 * GATE_TILE)),
                  _resident((1, D_LRU)), _resident((1, D_LRU)), _resident((1, D_LRU))],
        out_specs=[per_b((t_rows, D_MODEL)), per_b((WINDOW, D_KV)), per_b((WINDOW, D_KV)),
                   per_b((CONV_WIDTH - 1, D_LRU)), per_b((1, D_LRU))],
        out_shape=[jax.ShapeDtypeStruct((bsz, t_rows, D_MODEL), BF16),
                   jax.ShapeDtypeStruct((bsz, WINDOW, D_KV), F32),
                   jax.ShapeDtypeStruct((bsz, WINDOW, D_KV), F32),
                   jax.ShapeDtypeStruct((bsz, CONV_WIDTH - 1, D_LRU), F32),
                   jax.ShapeDtypeStruct((bsz, 1, D_LRU), F32)],
        scratch_shapes=[pltpu.VMEM((N_KV_HEADS, GQA_GROUP * t_rows, KEY_PAD), F32),
                        pltpu.VMEM((N_KV_HEADS, GQA_GROUP * t_rows, LANES), F32),
                        pltpu.VMEM((N_KV_HEADS, KEY_PAD, GROUP_W), BF16),
                        pltpu.VMEM((N_KV_HEADS, KEY_PAD, V_EXT_W), BF16),
                        pltpu.VMEM((SUBLANES + t_rows, D_LRU), F32),
                        pltpu.VMEM((t_rows, D_LRU), F32),
                        pltpu.VMEM((t_rows, D_LRU), F32),
                        pltpu.VMEM((SUBLANES, D_LRU), F32)],
        compiler_params=pltpu.CompilerParams(
            dimension_semantics=("arbitrary",), vmem_limit_bytes=VMEM_LIMIT_BYTES),
        name="sample_mixer",
    )(sinks, z3, cache_k, cache_v, state_conv, state_h, cw, cb, wg, bga, bgx, lam)


def _block_diag_tiles(w):
    per = GATE_TILE // LRU_BLOCK
    w4 = w.reshape(N_GATE_TILES, per, LRU_BLOCK, LRU_BLOCK)
    eye = jnp.eye(per, dtype=w.dtype)
    return jnp.einsum('tade,ab->tadbe', w4, eye).reshape(N_GATE_TILES, GATE_TILE, GATE_TILE)


def kernel(x_prompt, x_sample, cache_k, cache_v, state_conv, state_h, w_in, b_in, conv_w, conv_b,
           w_gate_a, b_gate_a, w_gate_x, b_gate_x, lru_lambda, attn_sinks, w_out, ln_g, ln_b):
    assert w_in.shape[0] == DEPTH == 1
    bsz, seq, _ = x_prompt.shape
    dbsz, dseq, _ = x_sample.shape
    n_cache = cache_k.shape[2]
    assert seq % ROW_TILE == 0 and (dbsz * dseq) % min(ROW_TILE, dbsz * dseq) == 0

    w_in_bf = w_in[0].astype(BF16)
    w_out_bf = w_out[0].astype(BF16)
    b_in_row = b_in[0].reshape(1, D_IN)
    wg = jnp.concatenate([_block_diag_tiles(w_gate_a[0]), _block_diag_tiles(w_gate_x[0])], axis=-1).astype(BF16)
    bga = b_gate_a[0].reshape(1, D_LRU)
    bgx = b_gate_x[0].reshape(1, D_LRU)
    lam = lru_lambda[0].reshape(1, D_LRU)
    cw = conv_w[0]
    cb = conv_b[0].reshape(1, D_LRU)
    sinks = attn_sinks[0]
    ln_g_row = ln_g[0].reshape(1, D_MODEL)
    ln_b_row = ln_b[0].reshape(1, D_MODEL)

    up, kp, vp, cp, hp = _prompt_mixer(x_prompt, sinks, w_in_bf, b_in_row, cw, cb, wg, bga, bgx, lam)
    yp = _out_proj(up.reshape(bsz * seq, D_MODEL), x_prompt.reshape(bsz * seq, D_MODEL),
                   w_out_bf, ln_g_row, ln_b_row).reshape(bsz, seq, D_MODEL)

    xs2 = x_sample.reshape(dbsz * dseq, D_MODEL)
    zs = _in_proj(xs2, w_in_bf, b_in_row).reshape(dbsz, dseq, D_IN)
    us, ks, vs, cs, hs = _sample_mixer(
        zs, cache_k[0].reshape(dbsz, n_cache, D_KV), cache_v[0].reshape(dbsz, n_cache, D_KV),
        state_conv[0], state_h[0].reshape(dbsz, 1, D_LRU), sinks, cw, cb, wg, bga, bgx, lam)
    ys = _out_proj(us.reshape(dbsz * dseq, D_MODEL), xs2, w_out_bf, ln_g_row, ln_b_row).reshape(dbsz, dseq, D_MODEL)

    kv_shape = lambda n: (1, n, WINDOW, N_KV_HEADS, HEAD_DIM)
    return (yp, ys,
            kp.reshape(kv_shape(bsz)), vp.reshape(kv_shape(bsz)),
            cp.reshape(1, bsz, CONV_WIDTH - 1, D_LRU), hp.reshape(1, bsz, D_LRU),
            ks.reshape(kv_shape(dbsz)), vs.reshape(kv_shape(dbsz)),
            cs.reshape(1, dbsz, CONV_WIDTH - 1, D_LRU), hs.reshape(1, dbsz, D_LRU))
```

```python
import functools

import jax
import jax.numpy as jnp
from jax import lax
from jax.experimental import pallas as pl
from jax.experimental.pallas import tpu as pltpu

F32 = jnp.float32
BF16 = jnp.bfloat16

D_MODEL = 2048
DEPTH = 1
CHUNK = 64
WINDOW = 128
WINDOW_CHUNKS = WINDOW // CHUNK
HEAD_DIM = 64
D_ATTN = D_MODEL // 2
D_LRU = D_MODEL - D_ATTN
N_Q_HEADS = D_ATTN // HEAD_DIM
N_KV_HEADS = 4
GQA_GROUP = N_Q_HEADS // N_KV_HEADS
D_KV = N_KV_HEADS * HEAD_DIM
N_LRU_BLOCKS = 16
LRU_BLOCK = D_LRU // N_LRU_BLOCKS
CONV_WIDTH = 4
LRU_C = 8.0
D_IN = 2 * D_ATTN + 2 * D_KV + 2 * D_LRU
PAST_LEN = 2048
DEEPNORM_ALPHA = (2.0 * DEPTH) ** 0.25
LN_EPS = 1e-5
NEG_INF = -1e30

OFF_Q = 0
OFF_K = D_ATTN
OFF_V = D_ATTN + D_KV
OFF_GA = D_ATTN + 2 * D_KV
OFF_XL = OFF_GA + D_ATTN
OFF_GL = OFF_XL + D_LRU

SUBLANES = 8
LANES = 128
MXU_DIM = 256
VMEM_LIMIT_BYTES = 56 * 1024 * 1024

GROUP_W = GQA_GROUP * HEAD_DIM
KEY_PAD = 2 * LANES
V_EXT_W = GROUP_W + LANES
GATE_TILE = MXU_DIM
N_GATE_TILES = D_LRU // GATE_TILE
ROW_TILE = 256
OUT_ROW_TILE = 512
UNITS_PER_LATE_PIECE = 4
SCORE_LOOKAHEAD = 2
SCORE_RING = SCORE_LOOKAHEAD + 2
N_TILE = 512
N_CHUNKS_PER_TILE = ROW_TILE // CHUNK
KBUF_ROWS = WINDOW + ROW_TILE + (KEY_PAD - (WINDOW + CHUNK))

_SLOPES = tuple(2.0 ** (-8.0 * h / N_Q_HEADS) for h in range(1, N_Q_HEADS + 1))


def _resident(shape):
    return pl.BlockSpec(shape, lambda *_: (0,) * len(shape), pipeline_mode=pl.Buffered(1))


class _ColGroups:
    def __init__(self, groups):
        self._groups = groups

    def _locate(self, idx):
        rows, cols = idx
        for first, end, ref in self._groups:
            if first <= cols.start and cols.stop <= end:
                return ref, rows, slice(cols.start - first, cols.stop - first)
        raise ValueError(f"columns {cols} straddle projection groups")

    def __getitem__(self, idx):
        ref, rows, cols = self._locate(idx)
        return ref[rows, cols]

    def __setitem__(self, idx, value):
        ref, rows, cols = self._locate(idx)
        ref[rows, cols] = value


_ALL_ROWS = slice(None)
_PROJ_GROUPS = ((OFF_Q, OFF_K), (OFF_K, OFF_GA), (OFF_GA, OFF_XL), (OFF_XL, OFF_GL), (OFF_GL, D_IN))


def _sigmoid(x):
    return 1.0 / (1.0 + jnp.exp(-x))


def _in_proj_into(xb, w_ref, b_ref, z, first=0, end=D_IN, width=N_TILE):
    for n in range(first, end, width):
        z[_ALL_ROWS, n:n + width] = (
            jnp.dot(xb, w_ref[:, n:n + width], preferred_element_type=F32) + b_ref[:, n:n + width])


def _bias_table(nq, kv, key0_pos, q0_pos, n_keys):
    rows = GQA_GROUP * nq
    r = lax.broadcasted_iota(jnp.int32, (rows, KEY_PAD), 0)
    c = lax.broadcasted_iota(jnp.int32, (rows, KEY_PAD), 1)
    shift = nq.bit_length() - 1
    g = jnp.right_shift(r, shift)
    qpos = q0_pos + (r & (nq - 1))
    kpos = key0_pos + c
    slope = jnp.full((rows, KEY_PAD), _SLOPES[kv * GQA_GROUP], F32)
    for gg in range(1, GQA_GROUP):
        slope = jnp.where(g == gg, _SLOPES[kv * GQA_GROUP + gg], slope)
    dist = jnp.abs(qpos - kpos).astype(F32)
    cshift = CHUNK.bit_length() - 1
    qc = jnp.right_shift(qpos, cshift)
    kc = jnp.right_shift(kpos, cshift)
    valid = (kpos >= 0) & (kc <= qc) & (kc >= qc - WINDOW_CHUNKS) & (c < n_keys)
    return jnp.where(valid, -(slope * dist), NEG_INF)


def _tile_heads(x, kv):
    pair = x[:, (kv // 2) * LANES:(kv // 2 + 1) * LANES]
    rolled = pltpu.roll(pair, HEAD_DIM, axis=1)
    lane = lax.broadcasted_iota(jnp.int32, pair.shape, 1)
    low = lane < HEAD_DIM
    both = jnp.where(low, pair, rolled) if kv % 2 == 0 else jnp.where(low, rolled, pair)
    both = both.astype(BF16)
    return jnp.concatenate([both, both], axis=1)


def _sink_table(nq, kv, sinks_ref):
    return jnp.concatenate(
        [jnp.full((nq, LANES), sinks_ref[kv * GQA_GROUP + g], F32) for g in range(GQA_GROUP)], axis=0)


def _head_of_lane(nq):
    lane = lax.broadcasted_iota(jnp.int32, (nq, GROUP_W), 1)
    return jnp.right_shift(lane, HEAD_DIM.bit_length() - 1)


def _attn_scores(q, k4):
    nq = q.shape[0]
    qb = (q * (HEAD_DIM ** -0.5)).astype(BF16)
    head_of_lane = _head_of_lane(nq)
    zero = jnp.zeros_like(qb)
    qs = jnp.concatenate([jnp.where(head_of_lane == g, qb, zero) for g in range(GQA_GROUP)], axis=0)
    return lax.dot_general(qs, k4, (((1,), (1,)), ((), ())), preferred_element_type=F32)


def _attn_finish(scores, v4e, bias, sink):
    nq = scores.shape[0] // GQA_GROUP
    head_of_lane = _head_of_lane(nq)
    s = scores + bias
    m = jnp.maximum(jnp.broadcast_to(jnp.max(s, axis=-1, keepdims=True), sink.shape), sink)
    e = jnp.concatenate(
        [jnp.exp(s[:, j * LANES:(j + 1) * LANES] - m) for j in range(KEY_PAD // LANES)], axis=1)
    o4e = jnp.dot(e.astype(BF16), v4e, preferred_element_type=F32)
    inv = 1.0 / (o4e[:, GROUP_W:] + jnp.exp(sink - m))
    o4 = o4e[:, :GROUP_W] * jnp.concatenate([inv] * (GROUP_W // LANES), axis=1)
    out = jnp.zeros((nq, GROUP_W), F32)
    for g in range(GQA_GROUP):
        out = out + jnp.where(head_of_lane == g, o4[g * nq:(g + 1) * nq], 0.0)
    return out


def _lru(z, u_out, t_rows, xbuf, a_s, b_s, hc, cw_ref, cb_ref, wg_ref, bga_ref, bgx_ref, lam_ref, fill=None):
    fill = fill or (lambda: None)
    neg = -lam_ref[...]
    softplus = jnp.maximum(neg, 0.0) + jnp.log1p(jnp.exp(-jnp.abs(neg)))
    coef = -LRU_C * softplus
    tiles = [slice(j * GATE_TILE, (j + 1) * GATE_TILE) for j in range(N_GATE_TILES)]

    def conv_and_gate_matmul(cols, j):
        xl = z[_ALL_ROWS, OFF_XL + cols.start:OFF_XL + cols.stop]
        xbuf[SUBLANES:SUBLANES + t_rows, cols] = xl
        xc = cb_ref[:, cols] + cw_ref[CONV_WIDTH - 1:CONV_WIDTH, cols] * xl
        for k in range(CONV_WIDTH - 1):
            off = SUBLANES - (CONV_WIDTH - 1) + k
            xc = xc + cw_ref[k:k + 1, cols] * xbuf[off:off + t_rows, cols]
        pre = jnp.dot(xc.astype(BF16), wg_ref[j], preferred_element_type=F32)
        a_s[:, cols] = pre[:, :GATE_TILE]
        b_s[:, cols] = pre[:, GATE_TILE:]
        return xc

    def gate_math(cols, xc):
        r = _sigmoid(a_s[:, cols] + bga_ref[:, cols])
        i = _sigmoid(b_s[:, cols] + bgx_ref[:, cols])
        a = jnp.exp(coef[:, cols] * r)
        mult = jnp.sqrt(jnp.maximum(1.0 - a * a, 0.0))
        a_s[:, cols] = a
        b_s[:, cols] = mult * (i * xc)

    fill()
    xc_prev = None
    for j, cols in enumerate(tiles):
        xc = conv_and_gate_matmul(cols, j)
        if xc_prev is not None:
            gate_math(tiles[j - 1], xc_prev)
        xc_prev = xc
        fill()
    gate_math(tiles[-1], xc_prev)

    row = lax.broadcasted_iota(jnp.int32, (SUBLANES, D_LRU), 0)
    h_prev = hc[...]
    for gi in range(t_rows // SUBLANES):
        rows = slice(gi * SUBLANES, (gi + 1) * SUBLANES)
        a8 = a_s[rows, :]
        b8 = b_s[rows, :]
        for d in (1, 2, 4):
            keep = row >= d
            a_sh = pltpu.roll(a8, d, axis=0)
            b_sh = pltpu.roll(b8, d, axis=0)
            b8 = jnp.where(keep, a8 * b_sh + b8, b8)
            a8 = jnp.where(keep, a8 * a_sh, a8)
        h8 = a8 * h_prev + b8
        b_s[rows, :] = h8
        h_prev = jnp.broadcast_to(h8[SUBLANES - 1:SUBLANES, :], (SUBLANES, D_LRU))
    hc[...] = h_prev
    gl = z[_ALL_ROWS, OFF_GL:OFF_GL + D_LRU]
    u_out[...] = (b_s[...] * (gl * _sigmoid(gl))).astype(u_out.dtype)
    xbuf[0:SUBLANES, :] = xbuf[t_rows:t_rows + SUBLANES, :]


def _prompt_kernel(sinks_ref, x_ref, win_ref, bin_ref, cw_ref, cb_ref, wg_ref, bga_ref, bgx_ref, lam_ref,
                   ua_ref, ul_ref, kwin_ref, vwin_ref, conv_ref, hout_ref,
                   zq, zkv, zga, zxl, zgl, bias_s, mask_s, sink_s, s_ring, k4_s, v4_s, xbuf, a_s, b_s, hc,
                   *, n_t, n_tiles):
    z = _ColGroups([(first, end, ref) for (first, end), ref in zip(_PROJ_GROUPS, (zq, zkv, zga, zxl, zgl))])
    ua = ua_ref.at[0]
    s = pl.program_id(0)
    t_a = lax.rem(jnp.minimum(s, n_tiles - 1), n_t)
    t_l = lax.rem(jnp.maximum(s - 1, 0), n_t)

    @pl.when(s == 0)
    def _init():
        for kv in range(N_KV_HEADS):
            bias_s[kv] = _bias_table(CHUNK, kv, 0, WINDOW, WINDOW + CHUNK)
            sink_s[kv] = _sink_table(CHUNK, kv, sinks_ref)
        col = lax.broadcasted_iota(jnp.int32, (SUBLANES, KEY_PAD), 1)
        for ci in range(WINDOW_CHUNKS):
            mask_s[ci] = jnp.where(col < WINDOW - ci * CHUNK, NEG_INF, 0.0)
        k4_s[...] = jnp.zeros_like(k4_s)
        v4_s[:, :, 0:GROUP_W] = jnp.zeros((N_KV_HEADS, KBUF_ROWS, GROUP_W), BF16)
        v4_s[:, :, GROUP_W:] = jnp.ones((N_KV_HEADS, KBUF_ROWS, LANES), BF16)
        zxl[...] = jnp.zeros_like(zxl)
        zgl[...] = jnp.zeros_like(zgl)

    @pl.when(t_a == 0)
    def _reset_attention_carry():
        k4_s[:, 0:WINDOW, :] = jnp.zeros((N_KV_HEADS, WINDOW, GROUP_W), BF16)
        v4_s[:, 0:WINDOW, 0:GROUP_W] = jnp.zeros((N_KV_HEADS, WINDOW, GROUP_W), BF16)

    @pl.when(t_l == 0)
    def _reset_lru_carry():
        xbuf[0:SUBLANES, :] = jnp.zeros((SUBLANES, D_LRU), F32)
        hc[...] = jnp.zeros_like(hc)

    xb = x_ref[0].astype(BF16)
    early_cols = list(range(OFF_K, OFF_GA, N_TILE)) + list(range(OFF_Q, OFF_K, N_TILE)) + list(
        range(OFF_GA, OFF_XL, N_TILE))
    late_cols = list(range(OFF_XL, D_IN, N_TILE))

    def project(cols):
        if cols:
            c0 = cols.pop(0)
            _in_proj_into(xb, win_ref, bin_ref, z, c0, c0 + N_TILE)

    _lru(z, ul_ref.at[0], ROW_TILE, xbuf, a_s, b_s, hc, cw_ref, cb_ref, wg_ref, bga_ref, bgx_ref, lam_ref,
         fill=lambda: project(early_cols))
    while early_cols:
        project(early_cols)

    k_new = z[_ALL_ROWS, OFF_K:OFF_K + D_KV]
    v_new = z[_ALL_ROWS, OFF_V:OFF_V + D_KV]
    for kv in range(N_KV_HEADS):
        k4_s[kv, WINDOW:WINDOW + ROW_TILE, :] = _tile_heads(k_new, kv)
        v4_s[kv, WINDOW:WINDOW + ROW_TILE, 0:GROUP_W] = _tile_heads(v_new, kv)

    units = [(ci, kv) for ci in range(N_CHUNKS_PER_TILE) for kv in range(N_KV_HEADS)]

    def scores_into_ring(idx):
        ci, kv = units[idx]
        s_ring[idx % SCORE_RING] = _attn_scores(
            z[ci * CHUNK:(ci + 1) * CHUNK, OFF_Q + kv * GROUP_W:OFF_Q + (kv + 1) * GROUP_W],
            k4_s[kv, ci * CHUNK:ci * CHUNK + KEY_PAD, :])

    mask_rows = [jnp.where(t_a == 0, mask_s[ci, 0:1, :], 0.0) for ci in range(WINDOW_CHUNKS)]
    for idx in range(SCORE_LOOKAHEAD):
        scores_into_ring(idx)
    for idx, (ci, kv) in enumerate(units):
        if idx + SCORE_LOOKAHEAD < len(units):
            scores_into_ring(idx + SCORE_LOOKAHEAD)
        if idx % UNITS_PER_LATE_PIECE == 1:
            project(late_cols)
        s_cur = s_ring[idx % SCORE_RING]
        rows = slice(ci * CHUNK, (ci + 1) * CHUNK)
        bias = bias_s[kv] + mask_rows[ci] if ci < WINDOW_CHUNKS else bias_s[kv]
        o = _attn_finish(s_cur, v4_s[kv, ci * CHUNK:ci * CHUNK + KEY_PAD, :], bias, sink_s[kv])
        ga = z[rows, OFF_GA + kv * GROUP_W:OFF_GA + (kv + 1) * GROUP_W]
        ua[rows, kv * GROUP_W:(kv + 1) * GROUP_W] = (o * (ga * _sigmoid(ga))).astype(ua.dtype)
    assert not early_cols and not late_cols

    @pl.when(t_l == n_t - 1)
    def _emit_lru_state():
        conv_ref[0] = xbuf[SUBLANES - (CONV_WIDTH - 1):SUBLANES, :]
        hout_ref[0] = hc[0:1, :]

    @pl.when(t_a == n_t - 1)
    def _emit_kv_window():
        kwin_ref[0] = z[slice(ROW_TILE - WINDOW, ROW_TILE), OFF_K:OFF_K + D_KV]
        vwin_ref[0] = z[slice(ROW_TILE - WINDOW, ROW_TILE), OFF_V:OFF_V + D_KV]

    @pl.when(s < n_tiles - 1)
    def _carry_kv():
        k4_s[:, 0:WINDOW, :] = k4_s[:, ROW_TILE:ROW_TILE + WINDOW, :]
        v4_s[:, 0:WINDOW, 0:GROUP_W] = v4_s[:, ROW_TILE:ROW_TILE + WINDOW, 0:GROUP_W]


def _prompt_mixer(x3, sinks, w_in_bf, b_in_row, cw, cb, wg, bga, bgx, lam):
    bsz, seq, _ = x3.shape
    n_t = seq // ROW_TILE
    n_tiles = bsz * n_t
    smem = pl.BlockSpec(memory_space=pltpu.SMEM)

    def attn_tile(s):
        ta = jnp.minimum(s, n_tiles - 1)
        return ta // n_t, ta % n_t

    def lru_tile(s):
        tl = jnp.maximum(s - 1, 0)
        return tl // n_t, tl % n_t

    return pl.pallas_call(
        functools.partial(_prompt_kernel, n_t=n_t, n_tiles=n_tiles),
        grid=(n_tiles + 1,),
        in_specs=[smem,
                  pl.BlockSpec((1, ROW_TILE, D_MODEL), lambda s: (*attn_tile(s), 0)),
                  _resident((D_MODEL, D_IN)), _resident((1, D_IN)),
                  _resident((CONV_WIDTH, D_LRU)), _resident((1, D_LRU)),
                  _resident((N_GATE_TILES, GATE_TILE, 2 * GATE_TILE)),
                  _resident((1, D_LRU)), _resident((1, D_LRU)), _resident((1, D_LRU))],
        out_specs=[pl.BlockSpec((1, ROW_TILE, D_ATTN), lambda s: (*attn_tile(s), 0)),
                   pl.BlockSpec((1, ROW_TILE, D_LRU), lambda s: (*lru_tile(s), 0)),
                   pl.BlockSpec((1, WINDOW, D_KV), lambda s: (attn_tile(s)[0], 0, 0)),
                   pl.BlockSpec((1, WINDOW, D_KV), lambda s: (attn_tile(s)[0], 0, 0)),
                   pl.BlockSpec((1, CONV_WIDTH - 1, D_LRU), lambda s: (lru_tile(s)[0], 0, 0)),
                   pl.BlockSpec((1, 1, D_LRU), lambda s: (lru_tile(s)[0], 0, 0))],
        out_shape=[jax.ShapeDtypeStruct((bsz, seq, D_ATTN), BF16),
                   jax.ShapeDtypeStruct((bsz, seq, D_LRU), BF16),
                   jax.ShapeDtypeStruct((bsz, WINDOW, D_KV), F32),
                   jax.ShapeDtypeStruct((bsz, WINDOW, D_KV), F32),
                   jax.ShapeDtypeStruct((bsz, CONV_WIDTH - 1, D_LRU), F32),
                   jax.ShapeDtypeStruct((bsz, 1, D_LRU), F32)],
        scratch_shapes=[pltpu.VMEM((ROW_TILE, end - first), F32) for first, end in _PROJ_GROUPS] + [
                        pltpu.VMEM((N_KV_HEADS, GQA_GROUP * CHUNK, KEY_PAD), F32),
                        pltpu.VMEM((WINDOW_CHUNKS, SUBLANES, KEY_PAD), F32),
                        pltpu.VMEM((N_KV_HEADS, GQA_GROUP * CHUNK, LANES), F32),
                        pltpu.VMEM((SCORE_RING, GQA_GROUP * CHUNK, KEY_PAD), F32),
                        pltpu.VMEM((N_KV_HEADS, KBUF_ROWS, GROUP_W), BF16),
                        pltpu.VMEM((N_KV_HEADS, KBUF_ROWS, V_EXT_W), BF16),
                        pltpu.VMEM((SUBLANES + ROW_TILE, D_LRU), F32),
                        pltpu.VMEM((ROW_TILE, D_LRU), F32),
                        pltpu.VMEM((ROW_TILE, D_LRU), F32),
                        pltpu.VMEM((SUBLANES, D_LRU), F32)],
        compiler_params=pltpu.CompilerParams(
            dimension_semantics=("arbitrary",), vmem_limit_bytes=VMEM_LIMIT_BYTES),
        name="prompt_mixer",
    )(sinks, x3, w_in_bf, b_in_row, cw, cb, wg, bga, bgx, lam)


def _out_proj_kernel(ua_ref, ul_ref, x_ref, w_ref, g_ref, b_ref, y_ref):
    y = (DEEPNORM_ALPHA * x_ref[...]
         + jnp.dot(ua_ref[...], w_ref[0:D_ATTN, :], preferred_element_type=F32)
         + jnp.dot(ul_ref[...], w_ref[D_ATTN:, :], preferred_element_type=F32))
    mu = jnp.mean(y, axis=-1, keepdims=True)
    yc = y - mu
    var = jnp.mean(yc * yc, axis=-1, keepdims=True)
    y_ref[...] = yc * lax.rsqrt(var + LN_EPS) * g_ref[...] + b_ref[...]


def _out_proj(ua2d, ul2d, x2d, w_bf, g_row, b_row):
    m = x2d.shape[0]
    tm = min(OUT_ROW_TILE, m)
    return pl.pallas_call(
        _out_proj_kernel,
        grid=(m // tm,),
        in_specs=[pl.BlockSpec((tm, D_ATTN), lambda i: (i, 0)),
                  pl.BlockSpec((tm, D_LRU), lambda i: (i, 0)),
                  pl.BlockSpec((tm, D_MODEL), lambda i: (i, 0)),
                  _resident((D_MODEL, D_MODEL)),
                  _resident((1, D_MODEL)),
                  _resident((1, D_MODEL))],
        out_specs=pl.BlockSpec((tm, D_MODEL), lambda i: (i, 0)),
        out_shape=jax.ShapeDtypeStruct((m, D_MODEL), F32),
        compiler_params=pltpu.CompilerParams(
            dimension_semantics=("arbitrary",), vmem_limit_bytes=VMEM_LIMIT_BYTES),
        name="out_proj",
    )(ua2d, ul2d, x2d, w_bf, g_row, b_row)


def _in_proj_kernel(x_ref, w_ref, b_ref, z_ref):
    _in_proj_into(x_ref[...].astype(BF16), w_ref, b_ref, z_ref)


def _in_proj(x2d, w_bf, b_row):
    m = x2d.shape[0]
    tm = min(ROW_TILE, m)
    return pl.pallas_call(
        _in_proj_kernel,
        grid=(m // tm,),
        in_specs=[pl.BlockSpec((tm, D_MODEL), lambda i: (i, 0)),
                  _resident((D_MODEL, D_IN)),
                  _resident((1, D_IN))],
        out_specs=pl.BlockSpec((tm, D_IN), lambda i: (i, 0)),
        out_shape=jax.ShapeDtypeStruct((m, D_IN), F32),
        compiler_params=pltpu.CompilerParams(
            dimension_semantics=("arbitrary",), vmem_limit_bytes=VMEM_LIMIT_BYTES),
        name="in_proj",
    )(x2d, w_bf, b_row)


def _sample_mixer_kernel(sinks_ref, z_ref, ck_ref, cv_ref, sc_ref, sh_ref,
                         cw_ref, cb_ref, wg_ref, bga_ref, bgx_ref, lam_ref,
                         ua_ref, ul_ref, kwin_ref, vwin_ref, conv_ref, hout_ref,
                         bias_s, sink_s, k4_s, v4_s, xbuf, a_s, b_s, hc, *, t_rows, n_cache):
    b = pl.program_id(0)
    n_keys = n_cache + t_rows
    z = _ColGroups([(0, D_IN, z_ref.at[0])])
    ua = ua_ref.at[0]

    @pl.when(b == 0)
    def _init_tables():
        for kv in range(N_KV_HEADS):
            bias_s[kv] = _bias_table(t_rows, kv, PAST_LEN - n_cache, PAST_LEN, n_keys)
            sink_s[kv] = _sink_table(t_rows, kv, sinks_ref)
        k4_s[...] = jnp.zeros_like(k4_s)
        v4_s[:, :, 0:GROUP_W] = jnp.zeros((N_KV_HEADS, KEY_PAD, GROUP_W), BF16)
        v4_s[:, :, GROUP_W:] = jnp.ones((N_KV_HEADS, KEY_PAD, LANES), BF16)

    k_all = jnp.concatenate([ck_ref[0], z[_ALL_ROWS, OFF_K:OFF_K + D_KV]], axis=0)
    v_all = jnp.concatenate([cv_ref[0], z[_ALL_ROWS, OFF_V:OFF_V + D_KV]], axis=0)
    for kv in range(N_KV_HEADS):
        k4_s[kv, 0:n_keys, :] = _tile_heads(k_all, kv)
        v4_s[kv, 0:n_keys, 0:GROUP_W] = _tile_heads(v_all, kv)
    for kv in range(N_KV_HEADS):
        scores = _attn_scores(z[_ALL_ROWS, OFF_Q + kv * GROUP_W:OFF_Q + (kv + 1) * GROUP_W], k4_s[kv])
        o = _attn_finish(scores, v4_s[kv], bias_s[kv], sink_s[kv])
        ga = z[_ALL_ROWS, OFF_GA + kv * GROUP_W:OFF_GA + (kv + 1) * GROUP_W]
        ua[:, kv * GROUP_W:(kv + 1) * GROUP_W] = (o * (ga * _sigmoid(ga))).astype(ua.dtype)
    kwin_ref[0] = k_all[n_keys - WINDOW:, :]
    vwin_ref[0] = v_all[n_keys - WINDOW:, :]

    xbuf[0:SUBLANES, :] = jnp.zeros((SUBLANES, D_LRU), F32)
    xbuf[SUBLANES - (CONV_WIDTH - 1):SUBLANES, :] = sc_ref[0]
    hc[...] = jnp.broadcast_to(sh_ref[0], (SUBLANES, D_LRU))
    _lru(z, ul_ref.at[0], t_rows, xbuf, a_s, b_s, hc, cw_ref, cb_ref, wg_ref, bga_ref, bgx_ref, lam_ref)
    conv_ref[0] = xbuf[SUBLANES - (CONV_WIDTH - 1):SUBLANES, :]
    hout_ref[0] = hc[0:1, :]


def _sample_mixer(z3, cache_k, cache_v, state_conv, state_h, sinks, cw, cb, wg, bga, bgx, lam):
    bsz, t_rows, _ = z3.shape
    n_cache = cache_k.shape[1]
    assert n_cache + t_rows <= KEY_PAD and n_cache + t_rows >= WINDOW
    assert t_rows % SUBLANES == 0 and t_rows & (t_rows - 1) == 0 and t_rows >= CONV_WIDTH - 1
    smem = pl.BlockSpec(memory_space=pltpu.SMEM)
    per_b = lambda shape: pl.BlockSpec((1,) + shape, lambda b: (b, 0, 0))
    return pl.pallas_call(
        functools.partial(_sample_mixer_kernel, t_rows=t_rows, n_cache=n_cache),
        grid=(bsz,),
        in_specs=[smem, per_b((t_rows, D_IN)), per_b((n_cache, D_KV)), per_b((n_cache, D_KV)),
                  per_b((CONV_WIDTH - 1, D_LRU)), per_b((1, D_LRU)),
                  _resident((CONV_WIDTH, D_LRU)), _resident((1, D_LRU)),
                  _resident((N_GATE_TILES, GATE_TILE, 2 * GATE_TILE)),
                  _resident((1, D_LRU)), _resident((1, D_LRU)), _resident((1, D_LRU))],
        out_specs=[per_b((t_rows, D_ATTN)), per_b((t_rows, D_LRU)), per_b((WINDOW, D_KV)), per_b((WINDOW, D_KV)),
                   per_b((CONV_WIDTH - 1, D_LRU)), per_b((1, D_LRU))],
        out_shape=[jax.ShapeDtypeStruct((bsz, t_rows, D_ATTN), BF16),
                   jax.ShapeDtypeStruct((bsz, t_rows, D_LRU), BF16),
                   jax.ShapeDtypeStruct((bsz, WINDOW, D_KV), F32),
                   jax.ShapeDtypeStruct((bsz, WINDOW, D_KV), F32),
                   jax.ShapeDtypeStruct((bsz, CONV_WIDTH - 1, D_LRU), F32),
                   jax.ShapeDtypeStruct((bsz, 1, D_LRU), F32)],
        scratch_shapes=[pltpu.VMEM((N_KV_HEADS, GQA_GROUP * t_rows, KEY_PAD), F32),
                        pltpu.VMEM((N_KV_HEADS, GQA_GROUP * t_rows, LANES), F32),
                        pltpu.VMEM((N_KV_HEADS, KEY_PAD, GROUP_W), BF16),
                        pltpu.VMEM((N_KV_HEADS, KEY_PAD, V_EXT_W), BF16),
                        pltpu.VMEM((SUBLANES + t_rows, D_LRU), F32),
                        pltpu.VMEM((t_rows, D_LRU), F32),
                        pltpu.VMEM((t_rows, D_LRU), F32),
                        pltpu.VMEM((SUBLANES, D_LRU), F32)],
        compiler_params=pltpu.CompilerParams(
            dimension_semantics=("arbitrary",), vmem_limit_bytes=VMEM_LIMIT_BYTES),
        name="sample_mixer",
    )(sinks, z3, cache_k, cache_v, state_conv, state_h, cw, cb, wg, bga, bgx, lam)


def _block_diag_tiles(w):
    per = GATE_TILE // LRU_BLOCK
    w4 = w.reshape(N_GATE_TILES, per, LRU_BLOCK, LRU_BLOCK)
    eye = jnp.eye(per, dtype=w.dtype)
    return jnp.einsum('tade,ab->tadbe', w4, eye).reshape(N_GATE_TILES, GATE_TILE, GATE_TILE)


def kernel(x_prompt, x_sample, cache_k, cache_v, state_conv, state_h, w_in, b_in, conv_w, conv_b,
           w_gate_a, b_gate_a, w_gate_x, b_gate_x, lru_lambda, attn_sinks, w_out, ln_g, ln_b):
    assert w_in.shape[0] == DEPTH == 1
    bsz, seq, _ = x_prompt.shape
    dbsz, dseq, _ = x_sample.shape
    n_cache = cache_k.shape[2]
    n_p, n_s = bsz * seq, dbsz * dseq
    assert seq % ROW_TILE == 0 and n_p % OUT_ROW_TILE == 0 and n_s % min(ROW_TILE, n_s) == 0

    w_in_bf = w_in[0].astype(BF16)
    w_out_bf = w_out[0].astype(BF16)
    b_in_row = b_in[0].reshape(1, D_IN)
    wg = jnp.concatenate([_block_diag_tiles(w_gate_a[0]), _block_diag_tiles(w_gate_x[0])], axis=-1).astype(BF16)
    bga = b_gate_a[0].reshape(1, D_LRU)
    bgx = b_gate_x[0].reshape(1, D_LRU)
    lam = lru_lambda[0].reshape(1, D_LRU)
    cw = conv_w[0]
    cb = conv_b[0].reshape(1, D_LRU)
    sinks = attn_sinks[0]
    ln_g_row = ln_g[0].reshape(1, D_MODEL)
    ln_b_row = ln_b[0].reshape(1, D_MODEL)

    uap, ulp, kp, vp, cp, hp = _prompt_mixer(x_prompt, sinks, w_in_bf, b_in_row, cw, cb, wg, bga, bgx, lam)
    yp = _out_proj(uap.reshape(n_p, D_ATTN), ulp.reshape(n_p, D_LRU), x_prompt.reshape(n_p, D_MODEL),
                   w_out_bf, ln_g_row, ln_b_row).reshape(bsz, seq, D_MODEL)

    xs2 = x_sample.reshape(n_s, D_MODEL)
    zs = _in_proj(xs2, w_in_bf, b_in_row).reshape(dbsz, dseq, D_IN)
    uas, uls, ks, vs, cs, hs = _sample_mixer(
        zs, cache_k[0].reshape(dbsz, n_cache, D_KV), cache_v[0].reshape(dbsz, n_cache, D_KV),
        state_conv[0], state_h[0].reshape(dbsz, 1, D_LRU), sinks, cw, cb, wg, bga, bgx, lam)
    ys = _out_proj(uas.reshape(n_s, D_ATTN), uls.reshape(n_s, D_LRU), xs2,
                   w_out_bf, ln_g_row, ln_b_row).reshape(dbsz, dseq, D_MODEL)

    kv_shape = lambda n: (1, n, WINDOW, N_KV_HEADS, HEAD_DIM)
    return (yp, ys,
            kp.reshape(kv_shape(bsz)), vp.reshape(kv_shape(bsz)),
            cp.reshape(1, bsz, CONV_WIDTH - 1, D_LRU), hp.reshape(1, bsz, D_LRU),
            ks.reshape(kv_shape(dbsz)), vs.reshape(kv_shape(dbsz)),
            cs.reshape(1, dbsz, CONV_WIDTH - 1, D_LRU), hs.reshape(1, dbsz, D_LRU))
```

```python
import functools

import jax
import jax.numpy as jnp
from jax import lax
from jax.experimental import pallas as pl
from jax.experimental.pallas import tpu as pltpu

F32 = jnp.float32
BF16 = jnp.bfloat16

D_MODEL = 2048
DEPTH = 1
CHUNK = 64
WINDOW = 128
WINDOW_CHUNKS = WINDOW // CHUNK
HEAD_DIM = 64
D_ATTN = D_MODEL // 2
D_LRU = D_MODEL - D_ATTN
N_Q_HEADS = D_ATTN // HEAD_DIM
N_KV_HEADS = 4
GQA_GROUP = N_Q_HEADS // N_KV_HEADS
D_KV = N_KV_HEADS * HEAD_DIM
N_LRU_BLOCKS = 16
LRU_BLOCK = D_LRU // N_LRU_BLOCKS
CONV_WIDTH = 4
LRU_C = 8.0
D_IN = 2 * D_ATTN + 2 * D_KV + 2 * D_LRU
PAST_LEN = 2048
DEEPNORM_ALPHA = (2.0 * DEPTH) ** 0.25
LN_EPS = 1e-5
NEG_INF = -1e30

OFF_Q = 0
OFF_K = D_ATTN
OFF_V = D_ATTN + D_KV
OFF_GA = D_ATTN + 2 * D_KV
OFF_XL = OFF_GA + D_ATTN
OFF_GL = OFF_XL + D_LRU

SUBLANES = 8
LANES = 128
MXU_DIM = 256
VMEM_LIMIT_BYTES = 56 * 1024 * 1024

GROUP_W = GQA_GROUP * HEAD_DIM
KEY_PAD = 2 * LANES
V_EXT_W = 2 * LANES
GATE_TILE = MXU_DIM
N_GATE_TILES = D_LRU // GATE_TILE
ROW_TILE = 256
OUT_ROW_TILE = 512
UNITS_PER_LATE_PIECE = 4
SCORE_LOOKAHEAD = 2
SCORE_RING = SCORE_LOOKAHEAD + 2
SAMPLE_SEQS_PER_STEP = 4
N_TILE = 512
N_CHUNKS_PER_TILE = ROW_TILE // CHUNK
KBUF_ROWS = WINDOW + ROW_TILE + (KEY_PAD - (WINDOW + CHUNK))

_SLOPES = tuple(2.0 ** (-8.0 * h / N_Q_HEADS) for h in range(1, N_Q_HEADS + 1))


def _resident(shape):
    return pl.BlockSpec(shape, lambda *_: (0,) * len(shape), pipeline_mode=pl.Buffered(1))


class _ColGroups:
    def __init__(self, groups):
        self._groups = groups

    def _locate(self, idx):
        rows, cols = idx
        for first, end, ref in self._groups:
            if first <= cols.start and cols.stop <= end:
                return ref, rows, slice(cols.start - first, cols.stop - first)
        raise ValueError(f"columns {cols} straddle projection groups")

    def __getitem__(self, idx):
        ref, rows, cols = self._locate(idx)
        return ref[rows, cols]

    def __setitem__(self, idx, value):
        ref, rows, cols = self._locate(idx)
        ref[rows, cols] = value


_ALL_ROWS = slice(None)
_PROJ_GROUPS = ((OFF_Q, OFF_K), (OFF_K, OFF_GA), (OFF_GA, OFF_XL), (OFF_XL, OFF_GL), (OFF_GL, D_IN))


def _sigmoid(x):
    return 1.0 / (1.0 + jnp.exp(-x))


def _in_proj_into(xb, w_ref, b_ref, z, first=0, end=D_IN, width=N_TILE):
    for n in range(first, end, width):
        z[_ALL_ROWS, n:n + width] = (
            jnp.dot(xb, w_ref[:, n:n + width], preferred_element_type=F32) + b_ref[:, n:n + width])


def _bias_table(nq, kv, key0_pos, q0_pos, n_keys):
    rows = GQA_GROUP * nq
    r = lax.broadcasted_iota(jnp.int32, (rows, KEY_PAD), 0)
    c = lax.broadcasted_iota(jnp.int32, (rows, KEY_PAD), 1)
    shift = nq.bit_length() - 1
    g = jnp.right_shift(r, shift)
    qpos = q0_pos + (r & (nq - 1))
    kpos = key0_pos + c
    slope = jnp.full((rows, KEY_PAD), _SLOPES[kv * GQA_GROUP], F32)
    for gg in range(1, GQA_GROUP):
        slope = jnp.where(g == gg, _SLOPES[kv * GQA_GROUP + gg], slope)
    dist = jnp.abs(qpos - kpos).astype(F32)
    cshift = CHUNK.bit_length() - 1
    qc = jnp.right_shift(qpos, cshift)
    kc = jnp.right_shift(kpos, cshift)
    valid = (kpos >= 0) & (kc <= qc) & (kc >= qc - WINDOW_CHUNKS) & (c < n_keys)
    return jnp.where(valid, -(slope * dist), NEG_INF)


def _tile_heads(x, kv, copies):
    pair = x[:, (kv // 2) * LANES:(kv // 2 + 1) * LANES]
    rolled = pltpu.roll(pair, HEAD_DIM, axis=1)
    lane = lax.broadcasted_iota(jnp.int32, pair.shape, 1)
    low = lane < HEAD_DIM
    both = jnp.where(low, pair, rolled) if kv % 2 == 0 else jnp.where(low, rolled, pair)
    both = both.astype(BF16)
    return jnp.concatenate([both] * (copies * HEAD_DIM // LANES), axis=1)


def _sink_table(nq, kv, sinks_ref):
    return jnp.concatenate(
        [jnp.full((nq, LANES), sinks_ref[kv * GQA_GROUP + g], F32) for g in range(GQA_GROUP)], axis=0)


def _head_of_lane(nq):
    lane = lax.broadcasted_iota(jnp.int32, (nq, GROUP_W), 1)
    return jnp.right_shift(lane, HEAD_DIM.bit_length() - 1)


def _attn_scores(q, k4):
    nq = q.shape[0]
    qb = (q * (HEAD_DIM ** -0.5)).astype(BF16)
    head_of_lane = _head_of_lane(nq)
    zero = jnp.zeros_like(qb)
    qs = jnp.concatenate([jnp.where(head_of_lane == g, qb, zero) for g in range(GQA_GROUP)], axis=0)
    return lax.dot_general(qs, k4, (((1,), (1,)), ((), ())), preferred_element_type=F32)


def _attn_finish(scores, v4e, bias, sink):
    nq = scores.shape[0] // GQA_GROUP
    s = scores + bias
    m = jnp.maximum(jnp.broadcast_to(jnp.max(s, axis=-1, keepdims=True), sink.shape), sink)
    e = jnp.concatenate(
        [jnp.exp(s[:, j * LANES:(j + 1) * LANES] - m) for j in range(KEY_PAD // LANES)], axis=1)
    o4e = jnp.dot(e.astype(BF16), v4e, preferred_element_type=F32)
    o2 = o4e[:, :LANES] * (1.0 / (o4e[:, LANES:] + jnp.exp(sink - m)))
    low = lax.broadcasted_iota(jnp.int32, (nq, LANES), 1) < HEAD_DIM
    return jnp.concatenate(
        [jnp.where(low, o2[g * nq:(g + 1) * nq], o2[(g + 1) * nq:(g + 2) * nq]) for g in range(0, GQA_GROUP, 2)],
        axis=1)


def _lru(z, u_out, t_rows, xbuf, a_s, b_s, hc, cw_ref, cb_ref, wg_ref, bga_ref, bgx_ref, lam_ref, fill=None):
    fill = fill or (lambda: None)
    neg = -lam_ref[...]
    softplus = jnp.maximum(neg, 0.0) + jnp.log1p(jnp.exp(-jnp.abs(neg)))
    coef = -LRU_C * softplus
    tiles = [slice(j * GATE_TILE, (j + 1) * GATE_TILE) for j in range(N_GATE_TILES)]

    def conv_and_gate_matmul(cols, j):
        xl = z[_ALL_ROWS, OFF_XL + cols.start:OFF_XL + cols.stop]
        xbuf[SUBLANES:SUBLANES + t_rows, cols] = xl
        xc = cb_ref[:, cols] + cw_ref[CONV_WIDTH - 1:CONV_WIDTH, cols] * xl
        for k in range(CONV_WIDTH - 1):
            off = SUBLANES - (CONV_WIDTH - 1) + k
            xc = xc + cw_ref[k:k + 1, cols] * xbuf[off:off + t_rows, cols]
        pre = jnp.dot(xc.astype(BF16), wg_ref[j], preferred_element_type=F32)
        a_s[:, cols] = pre[:, :GATE_TILE]
        b_s[:, cols] = pre[:, GATE_TILE:]
        return xc

    def gate_math(cols, xc):
        r = _sigmoid(a_s[:, cols] + bga_ref[:, cols])
        i = _sigmoid(b_s[:, cols] + bgx_ref[:, cols])
        a = jnp.exp(coef[:, cols] * r)
        mult = jnp.sqrt(jnp.maximum(1.0 - a * a, 0.0))
        a_s[:, cols] = a
        b_s[:, cols] = mult * (i * xc)

    fill()
    xc_prev = None
    for j, cols in enumerate(tiles):
        xc = conv_and_gate_matmul(cols, j)
        if xc_prev is not None:
            gate_math(tiles[j - 1], xc_prev)
        xc_prev = xc
        fill()
    gate_math(tiles[-1], xc_prev)

    row = lax.broadcasted_iota(jnp.int32, (SUBLANES, D_LRU), 0)
    h_prev = hc[...]
    for gi in range(t_rows // SUBLANES):
        rows = slice(gi * SUBLANES, (gi + 1) * SUBLANES)
        a8 = a_s[rows, :]
        b8 = b_s[rows, :]
        for d in (1, 2, 4):
            keep = row >= d
            a_sh = pltpu.roll(a8, d, axis=0)
            b_sh = pltpu.roll(b8, d, axis=0)
            b8 = jnp.where(keep, a8 * b_sh + b8, b8)
            a8 = jnp.where(keep, a8 * a_sh, a8)
        h8 = a8 * h_prev + b8
        b_s[rows, :] = h8
        h_prev = jnp.broadcast_to(h8[SUBLANES - 1:SUBLANES, :], (SUBLANES, D_LRU))
    hc[...] = h_prev
    gl = z[_ALL_ROWS, OFF_GL:OFF_GL + D_LRU]
    u_out[...] = (b_s[...] * (gl * _sigmoid(gl))).astype(u_out.dtype)
    xbuf[0:SUBLANES, :] = xbuf[t_rows:t_rows + SUBLANES, :]


def _prompt_kernel(sinks_ref, x_ref, win_ref, bin_ref, cw_ref, cb_ref, wg_ref, bga_ref, bgx_ref, lam_ref,
                   ua_ref, ul_ref, kwin_ref, vwin_ref, conv_ref, hout_ref,
                   zq, zkv, zga, zxl, zgl, bias_s, mask_s, sink_s, s_ring, k4_s, v4_s, xbuf, a_s, b_s, hc,
                   *, n_t, n_tiles):
    z = _ColGroups([(first, end, ref) for (first, end), ref in zip(_PROJ_GROUPS, (zq, zkv, zga, zxl, zgl))])
    ua = ua_ref.at[0]
    s = pl.program_id(0)
    t_a = lax.rem(jnp.minimum(s, n_tiles - 1), n_t)
    t_l = lax.rem(jnp.maximum(s - 1, 0), n_t)

    @pl.when(s == 0)
    def _init():
        for kv in range(N_KV_HEADS):
            bias_s[kv] = _bias_table(CHUNK, kv, 0, WINDOW, WINDOW + CHUNK)
            sink_s[kv] = _sink_table(CHUNK, kv, sinks_ref)
        col = lax.broadcasted_iota(jnp.int32, (SUBLANES, KEY_PAD), 1)
        for ci in range(WINDOW_CHUNKS):
            mask_s[ci] = jnp.where(col < WINDOW - ci * CHUNK, NEG_INF, 0.0)
        k4_s[...] = jnp.zeros_like(k4_s)
        v4_s[:, :, 0:LANES] = jnp.zeros((N_KV_HEADS, KBUF_ROWS, LANES), BF16)
        v4_s[:, :, LANES:] = jnp.ones((N_KV_HEADS, KBUF_ROWS, LANES), BF16)
        zxl[...] = jnp.zeros_like(zxl)
        zgl[...] = jnp.zeros_like(zgl)

    @pl.when(t_a == 0)
    def _reset_attention_carry():
        k4_s[:, 0:WINDOW, :] = jnp.zeros((N_KV_HEADS, WINDOW, GROUP_W), BF16)
        v4_s[:, 0:WINDOW, 0:LANES] = jnp.zeros((N_KV_HEADS, WINDOW, LANES), BF16)

    @pl.when(t_l == 0)
    def _reset_lru_carry():
        xbuf[0:SUBLANES, :] = jnp.zeros((SUBLANES, D_LRU), F32)
        hc[...] = jnp.zeros_like(hc)

    xb = x_ref[0].astype(BF16)
    early_cols = list(range(OFF_K, OFF_GA, N_TILE)) + list(range(OFF_Q, OFF_K, N_TILE)) + list(
        range(OFF_GA, OFF_XL, N_TILE))
    late_cols = list(range(OFF_XL, D_IN, N_TILE))

    def project(cols):
        if cols:
            c0 = cols.pop(0)
            _in_proj_into(xb, win_ref, bin_ref, z, c0, c0 + N_TILE)

    _lru(z, ul_ref.at[0], ROW_TILE, xbuf, a_s, b_s, hc, cw_ref, cb_ref, wg_ref, bga_ref, bgx_ref, lam_ref,
         fill=lambda: project(early_cols))
    while early_cols:
        project(early_cols)

    k_new = z[_ALL_ROWS, OFF_K:OFF_K + D_KV]
    v_new = z[_ALL_ROWS, OFF_V:OFF_V + D_KV]
    for kv in range(N_KV_HEADS):
        k4_s[kv, WINDOW:WINDOW + ROW_TILE, :] = _tile_heads(k_new, kv, GQA_GROUP)
        v4_s[kv, WINDOW:WINDOW + ROW_TILE, 0:LANES] = _tile_heads(v_new, kv, 2)

    units = [(ci, kv) for ci in range(N_CHUNKS_PER_TILE) for kv in range(N_KV_HEADS)]

    def scores_into_ring(idx):
        ci, kv = units[idx]
        s_ring[idx % SCORE_RING] = _attn_scores(
            z[ci * CHUNK:(ci + 1) * CHUNK, OFF_Q + kv * GROUP_W:OFF_Q + (kv + 1) * GROUP_W],
            k4_s[kv, ci * CHUNK:ci * CHUNK + KEY_PAD, :])

    mask_rows = [jnp.where(t_a == 0, mask_s[ci, 0:1, :], 0.0) for ci in range(WINDOW_CHUNKS)]
    for idx in range(SCORE_LOOKAHEAD):
        scores_into_ring(idx)
    for idx, (ci, kv) in enumerate(units):
        if idx + SCORE_LOOKAHEAD < len(units):
            scores_into_ring(idx + SCORE_LOOKAHEAD)
        if idx % UNITS_PER_LATE_PIECE == 1:
            project(late_cols)
        s_cur = s_ring[idx % SCORE_RING]
        rows = slice(ci * CHUNK, (ci + 1) * CHUNK)
        bias = bias_s[kv] + mask_rows[ci] if ci < WINDOW_CHUNKS else bias_s[kv]
        o = _attn_finish(s_cur, v4_s[kv, ci * CHUNK:ci * CHUNK + KEY_PAD, :], bias, sink_s[kv])
        ga = z[rows, OFF_GA + kv * GROUP_W:OFF_GA + (kv + 1) * GROUP_W]
        ua[rows, kv * GROUP_W:(kv + 1) * GROUP_W] = (o * (ga * _sigmoid(ga))).astype(ua.dtype)
    assert not early_cols and not late_cols

    @pl.when(t_l == n_t - 1)
    def _emit_lru_state():
        conv_ref[0] = xbuf[SUBLANES - (CONV_WIDTH - 1):SUBLANES, :]
        hout_ref[0] = hc[0:1, :]

    @pl.when(t_a == n_t - 1)
    def _emit_kv_window():
        kwin_ref[0] = z[slice(ROW_TILE - WINDOW, ROW_TILE), OFF_K:OFF_K + D_KV]
        vwin_ref[0] = z[slice(ROW_TILE - WINDOW, ROW_TILE), OFF_V:OFF_V + D_KV]

    @pl.when(s < n_tiles - 1)
    def _carry_kv():
        k4_s[:, 0:WINDOW, :] = k4_s[:, ROW_TILE:ROW_TILE + WINDOW, :]
        v4_s[:, 0:WINDOW, 0:LANES] = v4_s[:, ROW_TILE:ROW_TILE + WINDOW, 0:LANES]


def _prompt_mixer(x3, sinks, w_in_bf, b_in_row, cw, cb, wg, bga, bgx, lam):
    bsz, seq, _ = x3.shape
    n_t = seq // ROW_TILE
    n_tiles = bsz * n_t
    smem = pl.BlockSpec(memory_space=pltpu.SMEM)

    def attn_tile(s):
        ta = jnp.minimum(s, n_tiles - 1)
        return ta // n_t, ta % n_t

    def lru_tile(s):
        tl = jnp.maximum(s - 1, 0)
        return tl // n_t, tl % n_t

    return pl.pallas_call(
        functools.partial(_prompt_kernel, n_t=n_t, n_tiles=n_tiles),
        grid=(n_tiles + 1,),
        in_specs=[smem,
                  pl.BlockSpec((1, ROW_TILE, D_MODEL), lambda s: (*attn_tile(s), 0)),
                  _resident((D_MODEL, D_IN)), _resident((1, D_IN)),
                  _resident((CONV_WIDTH, D_LRU)), _resident((1, D_LRU)),
                  _resident((N_GATE_TILES, GATE_TILE, 2 * GATE_TILE)),
                  _resident((1, D_LRU)), _resident((1, D_LRU)), _resident((1, D_LRU))],
        out_specs=[pl.BlockSpec((1, ROW_TILE, D_ATTN), lambda s: (*attn_tile(s), 0)),
                   pl.BlockSpec((1, ROW_TILE, D_LRU), lambda s: (*lru_tile(s), 0)),
                   pl.BlockSpec((1, WINDOW, D_KV), lambda s: (attn_tile(s)[0], 0, 0)),
                   pl.BlockSpec((1, WINDOW, D_KV), lambda s: (attn_tile(s)[0], 0, 0)),
                   pl.BlockSpec((1, CONV_WIDTH - 1, D_LRU), lambda s: (lru_tile(s)[0], 0, 0)),
                   pl.BlockSpec((1, 1, D_LRU), lambda s: (lru_tile(s)[0], 0, 0))],
        out_shape=[jax.ShapeDtypeStruct((bsz, seq, D_ATTN), BF16),
                   jax.ShapeDtypeStruct((bsz, seq, D_LRU), BF16),
                   jax.ShapeDtypeStruct((bsz, WINDOW, D_KV), F32),
                   jax.ShapeDtypeStruct((bsz, WINDOW, D_KV), F32),
                   jax.ShapeDtypeStruct((bsz, CONV_WIDTH - 1, D_LRU), F32),
                   jax.ShapeDtypeStruct((bsz, 1, D_LRU), F32)],
        scratch_shapes=[pltpu.VMEM((ROW_TILE, end - first), F32) for first, end in _PROJ_GROUPS] + [
                        pltpu.VMEM((N_KV_HEADS, GQA_GROUP * CHUNK, KEY_PAD), F32),
                        pltpu.VMEM((WINDOW_CHUNKS, SUBLANES, KEY_PAD), F32),
                        pltpu.VMEM((N_KV_HEADS, GQA_GROUP * CHUNK, LANES), F32),
                        pltpu.VMEM((SCORE_RING, GQA_GROUP * CHUNK, KEY_PAD), F32),
                        pltpu.VMEM((N_KV_HEADS, KBUF_ROWS, GROUP_W), BF16),
                        pltpu.VMEM((N_KV_HEADS, KBUF_ROWS, V_EXT_W), BF16),
                        pltpu.VMEM((SUBLANES + ROW_TILE, D_LRU), F32),
                        pltpu.VMEM((ROW_TILE, D_LRU), F32),
                        pltpu.VMEM((ROW_TILE, D_LRU), F32),
                        pltpu.VMEM((SUBLANES, D_LRU), F32)],
        compiler_params=pltpu.CompilerParams(
            dimension_semantics=("arbitrary",), vmem_limit_bytes=VMEM_LIMIT_BYTES),
        name="prompt_mixer",
    )(sinks, x3, w_in_bf, b_in_row, cw, cb, wg, bga, bgx, lam)


def _out_proj_kernel(ua_ref, ul_ref, x_ref, w_ref, g_ref, b_ref, y_ref):
    y = (DEEPNORM_ALPHA * x_ref[...]
         + jnp.dot(ua_ref[...], w_ref[0:D_ATTN, :], preferred_element_type=F32)
         + jnp.dot(ul_ref[...], w_ref[D_ATTN:, :], preferred_element_type=F32))
    mu = jnp.mean(y, axis=-1, keepdims=True)
    yc = y - mu
    var = jnp.mean(yc * yc, axis=-1, keepdims=True)
    y_ref[...] = yc * lax.rsqrt(var + LN_EPS) * g_ref[...] + b_ref[...]


def _out_proj(ua2d, ul2d, x2d, w_bf, g_row, b_row):
    m = x2d.shape[0]
    tm = min(OUT_ROW_TILE, m)
    return pl.pallas_call(
        _out_proj_kernel,
        grid=(m // tm,),
        in_specs=[pl.BlockSpec((tm, D_ATTN), lambda i: (i, 0)),
                  pl.BlockSpec((tm, D_LRU), lambda i: (i, 0)),
                  pl.BlockSpec((tm, D_MODEL), lambda i: (i, 0)),
                  _resident((D_MODEL, D_MODEL)),
                  _resident((1, D_MODEL)),
                  _resident((1, D_MODEL))],
        out_specs=pl.BlockSpec((tm, D_MODEL), lambda i: (i, 0)),
        out_shape=jax.ShapeDtypeStruct((m, D_MODEL), F32),
        compiler_params=pltpu.CompilerParams(
            dimension_semantics=("arbitrary",), vmem_limit_bytes=VMEM_LIMIT_BYTES),
        name="out_proj",
    )(ua2d, ul2d, x2d, w_bf, g_row, b_row)


def _in_proj_kernel(x_ref, w_ref, b_ref, z_ref):
    z_ref[...] = jnp.dot(x_ref[...].astype(BF16), w_ref[...], preferred_element_type=F32) + b_ref[...]


def _in_proj(x2d, w_bf, b_row):
    m = x2d.shape[0]
    return pl.pallas_call(
        _in_proj_kernel,
        grid=(D_IN // N_TILE,),
        in_specs=[pl.BlockSpec((m, D_MODEL), lambda n: (0, 0)),
                  pl.BlockSpec((D_MODEL, N_TILE), lambda n: (0, n)),
                  pl.BlockSpec((1, N_TILE), lambda n: (0, n))],
        out_specs=pl.BlockSpec((m, N_TILE), lambda n: (0, n)),
        out_shape=jax.ShapeDtypeStruct((m, D_IN), F32),
        compiler_params=pltpu.CompilerParams(
            dimension_semantics=("arbitrary",), vmem_limit_bytes=VMEM_LIMIT_BYTES),
        name="in_proj",
    )(x2d, w_bf, b_row)


def _sample_mixer_kernel(sinks_ref, z_ref, ck_ref, cv_ref, sc_ref, sh_ref,
                         cw_ref, cb_ref, wg_ref, bga_ref, bgx_ref, lam_ref,
                         ua_ref, ul_ref, kwin_ref, vwin_ref, conv_ref, hout_ref,
                         bias_s, sink_s, k4_s, v4_s, xbuf, a_s, b_s, hc, *, t_rows, n_cache, n_seq):
    n_keys = n_cache + t_rows

    @pl.when(pl.program_id(0) == 0)
    def _init_tables():
        for kv in range(N_KV_HEADS):
            bias_s[kv] = _bias_table(t_rows, kv, PAST_LEN - n_cache, PAST_LEN, n_keys)
            sink_s[kv] = _sink_table(t_rows, kv, sinks_ref)
        k4_s[...] = jnp.zeros_like(k4_s)
        v4_s[:, :, :, 0:LANES] = jnp.zeros((n_seq, N_KV_HEADS, KEY_PAD, LANES), BF16)
        v4_s[:, :, :, LANES:] = jnp.ones((n_seq, N_KV_HEADS, KEY_PAD, LANES), BF16)

    for i in range(n_seq):
        z = _ColGroups([(0, D_IN, z_ref.at[i])])
        ua = ua_ref.at[i]
        k_all = jnp.concatenate([ck_ref[i], z[_ALL_ROWS, OFF_K:OFF_K + D_KV]], axis=0)
        v_all = jnp.concatenate([cv_ref[i], z[_ALL_ROWS, OFF_V:OFF_V + D_KV]], axis=0)
        for kv in range(N_KV_HEADS):
            k4_s[i, kv, 0:n_keys, :] = _tile_heads(k_all, kv, GQA_GROUP)
            v4_s[i, kv, 0:n_keys, 0:LANES] = _tile_heads(v_all, kv, 2)
        for kv in range(N_KV_HEADS):
            scores = _attn_scores(z[_ALL_ROWS, OFF_Q + kv * GROUP_W:OFF_Q + (kv + 1) * GROUP_W], k4_s[i, kv])
            o = _attn_finish(scores, v4_s[i, kv], bias_s[kv], sink_s[kv])
            ga = z[_ALL_ROWS, OFF_GA + kv * GROUP_W:OFF_GA + (kv + 1) * GROUP_W]
            ua[:, kv * GROUP_W:(kv + 1) * GROUP_W] = (o * (ga * _sigmoid(ga))).astype(ua.dtype)
        kwin_ref[i] = k_all[n_keys - WINDOW:, :]
        vwin_ref[i] = v_all[n_keys - WINDOW:, :]

        xb_i, hc_i = xbuf.at[i], hc.at[i]
        xb_i[0:SUBLANES, :] = jnp.zeros((SUBLANES, D_LRU), F32)
        xb_i[SUBLANES - (CONV_WIDTH - 1):SUBLANES, :] = sc_ref[i]
        hc_i[...] = jnp.broadcast_to(sh_ref[i], (SUBLANES, D_LRU))
        _lru(z, ul_ref.at[i], t_rows, xb_i, a_s.at[i], b_s.at[i], hc_i,
             cw_ref, cb_ref, wg_ref, bga_ref, bgx_ref, lam_ref)
        conv_ref[i] = xb_i[SUBLANES - (CONV_WIDTH - 1):SUBLANES, :]
        hout_ref[i] = hc_i[0:1, :]


def _sample_mixer(z3, cache_k, cache_v, state_conv, state_h, sinks, cw, cb, wg, bga, bgx, lam):
    bsz, t_rows, _ = z3.shape
    n_cache = cache_k.shape[1]
    n_seq = min(SAMPLE_SEQS_PER_STEP, bsz)
    assert bsz % n_seq == 0
    assert n_cache + t_rows <= KEY_PAD and n_cache + t_rows >= WINDOW
    assert t_rows % SUBLANES == 0 and t_rows & (t_rows - 1) == 0 and t_rows >= CONV_WIDTH - 1
    smem = pl.BlockSpec(memory_space=pltpu.SMEM)
    per_b = lambda shape: pl.BlockSpec((n_seq,) + shape, lambda b: (b, 0, 0))
    return pl.pallas_call(
        functools.partial(_sample_mixer_kernel, t_rows=t_rows, n_cache=n_cache, n_seq=n_seq),
        grid=(bsz // n_seq,),
        in_specs=[smem, per_b((t_rows, D_IN)), per_b((n_cache, D_KV)), per_b((n_cache, D_KV)),
                  per_b((CONV_WIDTH - 1, D_LRU)), per_b((1, D_LRU)),
                  _resident((CONV_WIDTH, D_LRU)), _resident((1, D_LRU)),
                  _resident((N_GATE_TILES, GATE_TILE, 2 * GATE_TILE)),
                  _resident((1, D_LRU)), _resident((1, D_LRU)), _resident((1, D_LRU))],
        out_specs=[per_b((t_rows, D_ATTN)), per_b((t_rows, D_LRU)), per_b((WINDOW, D_KV)), per_b((WINDOW, D_KV)),
                   per_b((CONV_WIDTH - 1, D_LRU)), per_b((1, D_LRU))],
        out_shape=[jax.ShapeDtypeStruct((bsz, t_rows, D_ATTN), BF16),
                   jax.ShapeDtypeStruct((bsz, t_rows, D_LRU), BF16),
                   jax.ShapeDtypeStruct((bsz, WINDOW, D_KV), F32),
                   jax.ShapeDtypeStruct((bsz, WINDOW, D_KV), F32),
                   jax.ShapeDtypeStruct((bsz, CONV_WIDTH - 1, D_LRU), F32),
                   jax.ShapeDtypeStruct((bsz, 1, D_LRU), F32)],
        scratch_shapes=[pltpu.VMEM((N_KV_HEADS, GQA_GROUP * t_rows, KEY_PAD), F32),
                        pltpu.VMEM((N_KV_HEADS, GQA_GROUP * t_rows, LANES), F32),
                        pltpu.VMEM((n_seq, N_KV_HEADS, KEY_PAD, GROUP_W), BF16),
                        pltpu.VMEM((n_seq, N_KV_HEADS, KEY_PAD, V_EXT_W), BF16),
                        pltpu.VMEM((n_seq, SUBLANES + t_rows, D_LRU), F32),
                        pltpu.VMEM((n_seq, t_rows, D_LRU), F32),
                        pltpu.VMEM((n_seq, t_rows, D_LRU), F32),
                        pltpu.VMEM((n_seq, SUBLANES, D_LRU), F32)],
        compiler_params=pltpu.CompilerParams(
            dimension_semantics=("arbitrary",), vmem_limit_bytes=VMEM_LIMIT_BYTES),
        name="sample_mixer",
    )(sinks, z3, cache_k, cache_v, state_conv, state_h, cw, cb, wg, bga, bgx, lam)


def _block_diag_tiles(w):
    per = GATE_TILE // LRU_BLOCK
    w4 = w.reshape(N_GATE_TILES, per, LRU_BLOCK, LRU_BLOCK)
    eye = jnp.eye(per, dtype=w.dtype)
    return jnp.einsum('tade,ab->tadbe', w4, eye).reshape(N_GATE_TILES, GATE_TILE, GATE_TILE)


def kernel(x_prompt, x_sample, cache_k, cache_v, state_conv, state_h, w_in, b_in, conv_w, conv_b,
           w_gate_a, b_gate_a, w_gate_x, b_gate_x, lru_lambda, attn_sinks, w_out, ln_g, ln_b):
    assert w_in.shape[0] == DEPTH == 1
    bsz, seq, _ = x_prompt.shape
    dbsz, dseq, _ = x_sample.shape
    n_cache = cache_k.shape[2]
    n_p, n_s = bsz * seq, dbsz * dseq
    assert seq % ROW_TILE == 0 and n_p % OUT_ROW_TILE == 0

    w_in_bf = w_in[0].astype(BF16)
    w_out_bf = w_out[0].astype(BF16)
    b_in_row = b_in[0].reshape(1, D_IN)
    wg = jnp.concatenate([_block_diag_tiles(w_gate_a[0]), _block_diag_tiles(w_gate_x[0])], axis=-1).astype(BF16)
    bga = b_gate_a[0].reshape(1, D_LRU)
    bgx = b_gate_x[0].reshape(1, D_LRU)
    lam = lru_lambda[0].reshape(1, D_LRU)
    cw = conv_w[0]
    cb = conv_b[0].reshape(1, D_LRU)
    sinks = attn_sinks[0]
    ln_g_row = ln_g[0].reshape(1, D_MODEL)
    ln_b_row = ln_b[0].reshape(1, D_MODEL)

    uap, ulp, kp, vp, cp, hp = _prompt_mixer(x_prompt, sinks, w_in_bf, b_in_row, cw, cb, wg, bga, bgx, lam)
    yp = _out_proj(uap.reshape(n_p, D_ATTN), ulp.reshape(n_p, D_LRU), x_prompt.reshape(n_p, D_MODEL),
                   w_out_bf, ln_g_row, ln_b_row).reshape(bsz, seq, D_MODEL)

    xs2 = x_sample.reshape(n_s, D_MODEL)
    zs = _in_proj(xs2, w_in_bf, b_in_row).reshape(dbsz, dseq, D_IN)
    uas, uls, ks, vs, cs, hs = _sample_mixer(
        zs, cache_k[0].reshape(dbsz, n_cache, D_KV), cache_v[0].reshape(dbsz, n_cache, D_KV),
        state_conv[0], state_h[0].reshape(dbsz, 1, D_LRU), sinks, cw, cb, wg, bga, bgx, lam)
    ys = _out_proj(uas.reshape(n_s, D_ATTN), uls.reshape(n_s, D_LRU), xs2,
                   w_out_bf, ln_g_row, ln_b_row).reshape(dbsz, dseq, D_MODEL)

    kv_shape = lambda n: (1, n, WINDOW, N_KV_HEADS, HEAD_DIM)
    return (yp, ys,
            kp.reshape(kv_shape(bsz)), vp.reshape(kv_shape(bsz)),
            cp.reshape(1, bsz, CONV_WIDTH - 1, D_LRU), hp.reshape(1, bsz, D_LRU),
            ks.reshape(kv_shape(dbsz)), vs.reshape(kv_shape(dbsz)),
            cs.reshape(1, dbsz, CONV_WIDTH - 1, D_LRU), hs.reshape(1, dbsz, D_LRU))
```

```python
import functools

import jax
import jax.numpy as jnp
from jax import lax
from jax.experimental import pallas as pl
from jax.experimental.pallas import tpu as pltpu

F32 = jnp.float32
BF16 = jnp.bfloat16

D_MODEL = 2048
DEPTH = 1
CHUNK = 64
WINDOW = 128
WINDOW_CHUNKS = WINDOW // CHUNK
HEAD_DIM = 64
D_ATTN = D_MODEL // 2
D_LRU = D_MODEL - D_ATTN
N_Q_HEADS = D_ATTN // HEAD_DIM
N_KV_HEADS = 4
GQA_GROUP = N_Q_HEADS // N_KV_HEADS
D_KV = N_KV_HEADS * HEAD_DIM
N_LRU_BLOCKS = 16
LRU_BLOCK = D_LRU // N_LRU_BLOCKS
CONV_WIDTH = 4
LRU_C = 8.0
D_IN = 2 * D_ATTN + 2 * D_KV + 2 * D_LRU
PAST_LEN = 2048
DEEPNORM_ALPHA = (2.0 * DEPTH) ** 0.25
LN_EPS = 1e-5
NEG_INF = -1e30

OFF_Q = 0
OFF_K = D_ATTN
OFF_V = D_ATTN + D_KV
OFF_GA = D_ATTN + 2 * D_KV
OFF_XL = OFF_GA + D_ATTN
OFF_GL = OFF_XL + D_LRU

SUBLANES = 8
LANES = 128
MXU_DIM = 256
VMEM_LIMIT_BYTES = 56 * 1024 * 1024

GROUP_W = GQA_GROUP * HEAD_DIM
KEY_PAD = 2 * LANES
V_EXT_W = 2 * LANES
GATE_TILE = MXU_DIM
N_GATE_TILES = D_LRU // GATE_TILE
ROW_TILE = 256
OUT_ROW_TILE = 512
UNITS_PER_LATE_PIECE = 4
SCORE_LOOKAHEAD = 2
SCORE_RING = SCORE_LOOKAHEAD + 2
N_TILE = 512
N_CHUNKS_PER_TILE = ROW_TILE // CHUNK
KBUF_ROWS = WINDOW + ROW_TILE + (KEY_PAD - (WINDOW + CHUNK))

_SLOPES = tuple(2.0 ** (-8.0 * h / N_Q_HEADS) for h in range(1, N_Q_HEADS + 1))


def _resident(shape):
    return pl.BlockSpec(shape, lambda *_: (0,) * len(shape), pipeline_mode=pl.Buffered(1))


class _ColGroups:
    def __init__(self, groups):
        self._groups = groups

    def _locate(self, idx):
        rows, cols = idx
        for first, end, ref in self._groups:
            if first <= cols.start and cols.stop <= end:
                return ref, rows, slice(cols.start - first, cols.stop - first)
        raise ValueError(f"columns {cols} straddle projection groups")

    def __getitem__(self, idx):
        ref, rows, cols = self._locate(idx)
        return ref[rows, cols]

    def __setitem__(self, idx, value):
        ref, rows, cols = self._locate(idx)
        ref[rows, cols] = value


_ALL_ROWS = slice(None)
_PROJ_GROUPS = ((OFF_Q, OFF_K), (OFF_K, OFF_GA), (OFF_GA, OFF_XL), (OFF_XL, OFF_GL), (OFF_GL, D_IN))


def _sigmoid(x):
    return 1.0 / (1.0 + jnp.exp(-x))


def _in_proj_into(xb, w_ref, b_ref, z, first=0, end=D_IN, width=N_TILE):
    for n in range(first, end, width):
        z[_ALL_ROWS, n:n + width] = (
            jnp.dot(xb, w_ref[:, n:n + width], preferred_element_type=F32) + b_ref[:, n:n + width])


def _bias_table(nq, kv, key0_pos, q0_pos, n_keys):
    rows = GQA_GROUP * nq
    r = lax.broadcasted_iota(jnp.int32, (rows, KEY_PAD), 0)
    c = lax.broadcasted_iota(jnp.int32, (rows, KEY_PAD), 1)
    shift = nq.bit_length() - 1
    g = jnp.right_shift(r, shift)
    qpos = q0_pos + (r & (nq - 1))
    kpos = key0_pos + c
    slope = jnp.full((rows, KEY_PAD), _SLOPES[kv * GQA_GROUP], F32)
    for gg in range(1, GQA_GROUP):
        slope = jnp.where(g == gg, _SLOPES[kv * GQA_GROUP + gg], slope)
    dist = jnp.abs(qpos - kpos).astype(F32)
    cshift = CHUNK.bit_length() - 1
    qc = jnp.right_shift(qpos, cshift)
    kc = jnp.right_shift(kpos, cshift)
    valid = (kpos >= 0) & (kc <= qc) & (kc >= qc - WINDOW_CHUNKS) & (c < n_keys)
    return jnp.where(valid, -(slope * dist), NEG_INF)


def _tile_heads(x, kv, copies):
    pair = x[:, (kv // 2) * LANES:(kv // 2 + 1) * LANES]
    rolled = pltpu.roll(pair, HEAD_DIM, axis=1)
    lane = lax.broadcasted_iota(jnp.int32, pair.shape, 1)
    low = lane < HEAD_DIM
    both = jnp.where(low, pair, rolled) if kv % 2 == 0 else jnp.where(low, rolled, pair)
    both = both.astype(BF16)
    return jnp.concatenate([both] * (copies * HEAD_DIM // LANES), axis=1)


def _sink_table(nq, kv, sinks_ref):
    return jnp.concatenate(
        [jnp.full((nq, LANES), sinks_ref[kv * GQA_GROUP + g], F32) for g in range(GQA_GROUP)], axis=0)


def _head_of_lane(nq):
    lane = lax.broadcasted_iota(jnp.int32, (nq, GROUP_W), 1)
    return jnp.right_shift(lane, HEAD_DIM.bit_length() - 1)


def _attn_scores(q, k4):
    nq = q.shape[0]
    qb = (q * (HEAD_DIM ** -0.5)).astype(BF16)
    head_of_lane = _head_of_lane(nq)
    zero = jnp.zeros_like(qb)
    qs = jnp.concatenate([jnp.where(head_of_lane == g, qb, zero) for g in range(GQA_GROUP)], axis=0)
    return lax.dot_general(qs, k4, (((1,), (1,)), ((), ())), preferred_element_type=F32)


def _attn_finish(scores, v4e, bias, sink):
    nq = scores.shape[0] // GQA_GROUP
    s = scores + bias
    m = jnp.maximum(jnp.broadcast_to(jnp.max(s, axis=-1, keepdims=True), sink.shape), sink)
    e = jnp.concatenate(
        [jnp.exp(s[:, j * LANES:(j + 1) * LANES] - m) for j in range(KEY_PAD // LANES)], axis=1)
    o4e = jnp.dot(e.astype(BF16), v4e, preferred_element_type=F32)
    o2 = o4e[:, :LANES] * (1.0 / (o4e[:, LANES:] + jnp.exp(sink - m)))
    low = lax.broadcasted_iota(jnp.int32, (nq, LANES), 1) < HEAD_DIM
    return jnp.concatenate(
        [jnp.where(low, o2[g * nq:(g + 1) * nq], o2[(g + 1) * nq:(g + 2) * nq]) for g in range(0, GQA_GROUP, 2)],
        axis=1)


def _lru(z, u_out, t_rows, xbuf, a_s, b_s, hc, cw_ref, cb_ref, wg_ref, bg_ref, lam_ref, fill=None):
    fill = fill or (lambda: None)
    neg = -lam_ref[...]
    softplus = jnp.maximum(neg, 0.0) + jnp.log1p(jnp.exp(-jnp.abs(neg)))
    coef = -LRU_C * softplus
    tiles = [slice(j * GATE_TILE, (j + 1) * GATE_TILE) for j in range(N_GATE_TILES)]

    def conv_and_gate_matmul(cols, j):
        xl = z[_ALL_ROWS, OFF_XL + cols.start:OFF_XL + cols.stop]
        xbuf[SUBLANES:SUBLANES + t_rows, cols] = xl
        xc = cb_ref[:, cols] + cw_ref[CONV_WIDTH - 1:CONV_WIDTH, cols] * xl
        for k in range(CONV_WIDTH - 1):
            off = SUBLANES - (CONV_WIDTH - 1) + k
            xc = xc + cw_ref[k:k + 1, cols] * xbuf[off:off + t_rows, cols]
        pre = jnp.dot(xc.astype(BF16), wg_ref[j], preferred_element_type=F32)
        a_s[:, cols] = pre[:, :GATE_TILE]
        b_s[:, cols] = pre[:, GATE_TILE:]
        return xc

    def gate_math(cols, xc):
        r = _sigmoid(a_s[:, cols] + bg_ref[0:1, cols])
        i = _sigmoid(b_s[:, cols] + bg_ref[1:2, cols])
        a = jnp.exp(coef[:, cols] * r)
        mult = jnp.sqrt(jnp.maximum(1.0 - a * a, 0.0))
        a_s[:, cols] = a
        b_s[:, cols] = mult * (i * xc)

    fill()
    xc_prev = None
    for j, cols in enumerate(tiles):
        xc = conv_and_gate_matmul(cols, j)
        if xc_prev is not None:
            gate_math(tiles[j - 1], xc_prev)
        xc_prev = xc
        fill()
    gate_math(tiles[-1], xc_prev)

    row = lax.broadcasted_iota(jnp.int32, (SUBLANES, D_LRU), 0)
    h_prev = hc[...]
    for gi in range(t_rows // SUBLANES):
        rows = slice(gi * SUBLANES, (gi + 1) * SUBLANES)
        a8 = a_s[rows, :]
        b8 = b_s[rows, :]
        for d in (1, 2, 4):
            keep = row >= d
            a_sh = pltpu.roll(a8, d, axis=0)
            b_sh = pltpu.roll(b8, d, axis=0)
            b8 = jnp.where(keep, a8 * b_sh + b8, b8)
            a8 = jnp.where(keep, a8 * a_sh, a8)
        h8 = a8 * h_prev + b8
        b_s[rows, :] = h8
        h_prev = jnp.broadcast_to(h8[SUBLANES - 1:SUBLANES, :], (SUBLANES, D_LRU))
    hc[...] = h_prev
    gl = z[_ALL_ROWS, OFF_GL:OFF_GL + D_LRU]
    u_out[...] = (b_s[...] * (gl * _sigmoid(gl))).astype(u_out.dtype)
    xbuf[0:SUBLANES, :] = xbuf[t_rows:t_rows + SUBLANES, :]


def _prompt_kernel(sinks_ref, x_ref, win_ref, bin_ref, cw_ref, cb_ref, wg_ref, bg_ref, lam_ref,
                   ua_ref, ul_ref, kwin_ref, vwin_ref, conv_ref, hout_ref,
                   zq, zkv, zga, zxl, zgl, bias_s, mask_s, sink_s, s_ring, k4_s, v4_s, xbuf, a_s, b_s, hc,
                   *, n_t, n_tiles):
    z = _ColGroups([(first, end, ref) for (first, end), ref in zip(_PROJ_GROUPS, (zq, zkv, zga, zxl, zgl))])
    ua = ua_ref.at[0]
    s = pl.program_id(0)
    t_a = lax.rem(jnp.minimum(s, n_tiles - 1), n_t)
    t_l = lax.rem(jnp.maximum(s - 1, 0), n_t)

    @pl.when(s == 0)
    def _init():
        for kv in range(N_KV_HEADS):
            bias_s[kv] = _bias_table(CHUNK, kv, 0, WINDOW, WINDOW + CHUNK)
            sink_s[kv] = _sink_table(CHUNK, kv, sinks_ref)
        col = lax.broadcasted_iota(jnp.int32, (SUBLANES, KEY_PAD), 1)
        for ci in range(WINDOW_CHUNKS):
            mask_s[ci] = jnp.where(col < WINDOW - ci * CHUNK, NEG_INF, 0.0)
        k4_s[...] = jnp.zeros_like(k4_s)
        v4_s[:, :, 0:LANES] = jnp.zeros((N_KV_HEADS, KBUF_ROWS, LANES), BF16)
        v4_s[:, :, LANES:] = jnp.ones((N_KV_HEADS, KBUF_ROWS, LANES), BF16)
        zxl[...] = jnp.zeros_like(zxl)
        zgl[...] = jnp.zeros_like(zgl)

    @pl.when(t_a == 0)
    def _reset_attention_carry():
        k4_s[:, 0:WINDOW, :] = jnp.zeros((N_KV_HEADS, WINDOW, GROUP_W), BF16)
        v4_s[:, 0:WINDOW, 0:LANES] = jnp.zeros((N_KV_HEADS, WINDOW, LANES), BF16)

    @pl.when(t_l == 0)
    def _reset_lru_carry():
        xbuf[0:SUBLANES, :] = jnp.zeros((SUBLANES, D_LRU), F32)
        hc[...] = jnp.zeros_like(hc)

    xb = x_ref[0].astype(BF16)
    early_cols = list(range(OFF_K, OFF_GA, N_TILE)) + list(range(OFF_Q, OFF_K, N_TILE)) + list(
        range(OFF_GA, OFF_XL, N_TILE))
    late_cols = list(range(OFF_XL, D_IN, N_TILE))

    def project(cols):
        if cols:
            c0 = cols.pop(0)
            _in_proj_into(xb, win_ref, bin_ref, z, c0, c0 + N_TILE)

    _lru(z, ul_ref.at[0], ROW_TILE, xbuf, a_s, b_s, hc, cw_ref, cb_ref, wg_ref, bg_ref, lam_ref,
         fill=lambda: project(early_cols))
    while early_cols:
        project(early_cols)

    k_new = z[_ALL_ROWS, OFF_K:OFF_K + D_KV]
    v_new = z[_ALL_ROWS, OFF_V:OFF_V + D_KV]
    for kv in range(N_KV_HEADS):
        k4_s[kv, WINDOW:WINDOW + ROW_TILE, :] = _tile_heads(k_new, kv, GQA_GROUP)
        v4_s[kv, WINDOW:WINDOW + ROW_TILE, 0:LANES] = _tile_heads(v_new, kv, 2)

    units = [(ci, kv) for ci in range(N_CHUNKS_PER_TILE) for kv in range(N_KV_HEADS)]

    def scores_into_ring(idx):
        ci, kv = units[idx]
        s_ring[idx % SCORE_RING] = _attn_scores(
            z[ci * CHUNK:(ci + 1) * CHUNK, OFF_Q + kv * GROUP_W:OFF_Q + (kv + 1) * GROUP_W],
            k4_s[kv, ci * CHUNK:ci * CHUNK + KEY_PAD, :])

    mask_rows = [jnp.where(t_a == 0, mask_s[ci, 0:1, :], 0.0) for ci in range(WINDOW_CHUNKS)]
    for idx in range(SCORE_LOOKAHEAD):
        scores_into_ring(idx)
    for idx, (ci, kv) in enumerate(units):
        if idx + SCORE_LOOKAHEAD < len(units):
            scores_into_ring(idx + SCORE_LOOKAHEAD)
        if idx % UNITS_PER_LATE_PIECE == 1:
            project(late_cols)
        s_cur = s_ring[idx % SCORE_RING]
        rows = slice(ci * CHUNK, (ci + 1) * CHUNK)
        bias = bias_s[kv] + mask_rows[ci] if ci < WINDOW_CHUNKS else bias_s[kv]
        o = _attn_finish(s_cur, v4_s[kv, ci * CHUNK:ci * CHUNK + KEY_PAD, :], bias, sink_s[kv])
        ga = z[rows, OFF_GA + kv * GROUP_W:OFF_GA + (kv + 1) * GROUP_W]
        ua[rows, kv * GROUP_W:(kv + 1) * GROUP_W] = (o * (ga * _sigmoid(ga))).astype(ua.dtype)
    assert not early_cols and not late_cols

    @pl.when(t_l == n_t - 1)
    def _emit_lru_state():
        seq_l = lax.div(jnp.maximum(s - 1, 0), n_t)
        for j in range(CONV_WIDTH - 1):
            row = SUBLANES - (CONV_WIDTH - 1) + j
            conv_ref[j, pl.ds(seq_l, 1), :] = xbuf[row:row + 1, :]
        hout_ref[pl.ds(seq_l, 1), :] = hc[0:1, :]

    @pl.when(t_a == n_t - 1)
    def _emit_kv_window():
        kwin_ref[0] = z[slice(ROW_TILE - WINDOW, ROW_TILE), OFF_K:OFF_K + D_KV].T
        vwin_ref[0] = z[slice(ROW_TILE - WINDOW, ROW_TILE), OFF_V:OFF_V + D_KV].T

    @pl.when(s < n_tiles - 1)
    def _carry_kv():
        k4_s[:, 0:WINDOW, :] = k4_s[:, ROW_TILE:ROW_TILE + WINDOW, :]
        v4_s[:, 0:WINDOW, 0:LANES] = v4_s[:, ROW_TILE:ROW_TILE + WINDOW, 0:LANES]


def _prompt_mixer(x3, sinks, w_in_bf, b_in_row, cw, cb, wg, bg, lam):
    bsz, seq, _ = x3.shape
    n_t = seq // ROW_TILE
    n_tiles = bsz * n_t
    smem = pl.BlockSpec(memory_space=pltpu.SMEM)

    def attn_tile(s):
        ta = jnp.minimum(s, n_tiles - 1)
        return ta // n_t, ta % n_t

    def lru_tile(s):
        tl = jnp.maximum(s - 1, 0)
        return tl // n_t, tl % n_t

    return pl.pallas_call(
        functools.partial(_prompt_kernel, n_t=n_t, n_tiles=n_tiles),
        grid=(n_tiles + 1,),
        in_specs=[smem,
                  pl.BlockSpec((1, ROW_TILE, D_MODEL), lambda s: (*attn_tile(s), 0)),
                  _resident((D_MODEL, D_IN)), _resident((1, D_IN)),
                  _resident((CONV_WIDTH, D_LRU)), _resident((1, D_LRU)),
                  _resident((N_GATE_TILES, GATE_TILE, 2 * GATE_TILE)),
                  _resident((2, D_LRU)), _resident((1, D_LRU))],
        out_specs=[pl.BlockSpec((1, ROW_TILE, D_ATTN), lambda s: (*attn_tile(s), 0)),
                   pl.BlockSpec((1, ROW_TILE, D_LRU), lambda s: (*lru_tile(s), 0)),
                   pl.BlockSpec((1, D_KV, WINDOW), lambda s: (attn_tile(s)[0], 0, 0)),
                   pl.BlockSpec((1, D_KV, WINDOW), lambda s: (attn_tile(s)[0], 0, 0)),
                   pl.BlockSpec((CONV_WIDTH - 1, bsz, D_LRU), lambda s: (0, 0, 0)),
                   pl.BlockSpec((bsz, D_LRU), lambda s: (0, 0))],
        out_shape=[jax.ShapeDtypeStruct((bsz, seq, D_ATTN), BF16),
                   jax.ShapeDtypeStruct((bsz, seq, D_LRU), BF16),
                   jax.ShapeDtypeStruct((bsz, D_KV, WINDOW), F32),
                   jax.ShapeDtypeStruct((bsz, D_KV, WINDOW), F32),
                   jax.ShapeDtypeStruct((CONV_WIDTH - 1, bsz, D_LRU), F32),
                   jax.ShapeDtypeStruct((bsz, D_LRU), F32)],
        scratch_shapes=[pltpu.VMEM((ROW_TILE, end - first), F32) for first, end in _PROJ_GROUPS] + [
                        pltpu.VMEM((N_KV_HEADS, GQA_GROUP * CHUNK, KEY_PAD), F32),
                        pltpu.VMEM((WINDOW_CHUNKS, SUBLANES, KEY_PAD), F32),
                        pltpu.VMEM((N_KV_HEADS, GQA_GROUP * CHUNK, LANES), F32),
                        pltpu.VMEM((SCORE_RING, GQA_GROUP * CHUNK, KEY_PAD), F32),
                        pltpu.VMEM((N_KV_HEADS, KBUF_ROWS, GROUP_W), BF16),
                        pltpu.VMEM((N_KV_HEADS, KBUF_ROWS, V_EXT_W), BF16),
                        pltpu.VMEM((SUBLANES + ROW_TILE, D_LRU), F32),
                        pltpu.VMEM((ROW_TILE, D_LRU), F32),
                        pltpu.VMEM((ROW_TILE, D_LRU), F32),
                        pltpu.VMEM((SUBLANES, D_LRU), F32)],
        compiler_params=pltpu.CompilerParams(
            dimension_semantics=("arbitrary",), vmem_limit_bytes=VMEM_LIMIT_BYTES),
        name="prompt_mixer",
    )(sinks, x3, w_in_bf, b_in_row, cw, cb, wg, bg, lam)


def _out_proj_kernel(ua_ref, ul_ref, x_ref, w_ref, g_ref, b_ref, y_ref):
    y = (DEEPNORM_ALPHA * x_ref[...]
         + jnp.dot(ua_ref[...], w_ref[0:D_ATTN, :], preferred_element_type=F32)
         + jnp.dot(ul_ref[...], w_ref[D_ATTN:, :], preferred_element_type=F32))
    mu = jnp.mean(y, axis=-1, keepdims=True)
    yc = y - mu
    var = jnp.mean(yc * yc, axis=-1, keepdims=True)
    y_ref[...] = yc * lax.rsqrt(var + LN_EPS) * g_ref[...] + b_ref[...]


def _out_proj(ua2d, ul2d, x2d, w_bf, g_row, b_row):
    m = x2d.shape[0]
    tm = min(OUT_ROW_TILE, m)
    return pl.pallas_call(
        _out_proj_kernel,
        grid=(m // tm,),
        in_specs=[pl.BlockSpec((tm, D_ATTN), lambda i: (i, 0)),
                  pl.BlockSpec((tm, D_LRU), lambda i: (i, 0)),
                  pl.BlockSpec((tm, D_MODEL), lambda i: (i, 0)),
                  _resident((D_MODEL, D_MODEL)),
                  _resident((1, D_MODEL)),
                  _resident((1, D_MODEL))],
        out_specs=pl.BlockSpec((tm, D_MODEL), lambda i: (i, 0)),
        out_shape=jax.ShapeDtypeStruct((m, D_MODEL), F32),
        compiler_params=pltpu.CompilerParams(
            dimension_semantics=("arbitrary",), vmem_limit_bytes=VMEM_LIMIT_BYTES),
        name="out_proj",
    )(ua2d, ul2d, x2d, w_bf, g_row, b_row)


def _in_proj_kernel(x_ref, w_ref, b_ref, z_ref):
    z_ref[...] = jnp.dot(x_ref[...].astype(BF16), w_ref[...], preferred_element_type=F32) + b_ref[...]


def _in_proj(x2d, w_bf, b_row):
    m = x2d.shape[0]
    return pl.pallas_call(
        _in_proj_kernel,
        grid=(D_IN // N_TILE,),
        in_specs=[pl.BlockSpec((m, D_MODEL), lambda n: (0, 0)),
                  pl.BlockSpec((D_MODEL, N_TILE), lambda n: (0, n)),
                  pl.BlockSpec((1, N_TILE), lambda n: (0, n))],
        out_specs=pl.BlockSpec((m, N_TILE), lambda n: (0, n)),
        out_shape=jax.ShapeDtypeStruct((m, D_IN), F32),
        compiler_params=pltpu.CompilerParams(
            dimension_semantics=("arbitrary",), vmem_limit_bytes=VMEM_LIMIT_BYTES),
        name="in_proj",
    )(x2d, w_bf, b_row)


def _sample_mixer_kernel(sinks_ref, z_ref, ck_ref, cv_ref, sc_ref, sh_ref,
                         cw_ref, cb_ref, wg_ref, bg_ref, lam_ref,
                         ua_ref, ul_ref, kwin_ref, vwin_ref, conv_ref, hout_ref,
                         bias_s, sink_s, k4_s, v4_s, xbuf, a_s, b_s, hc, *, t_rows, n_cache):
    b = pl.program_id(0)
    n_keys = n_cache + t_rows
    z = _ColGroups([(0, D_IN, z_ref.at[0])])
    ua = ua_ref.at[0]

    @pl.when(b == 0)
    def _init_tables():
        for kv in range(N_KV_HEADS):
            bias_s[kv] = _bias_table(t_rows, kv, PAST_LEN - n_cache, PAST_LEN, n_keys)
            sink_s[kv] = _sink_table(t_rows, kv, sinks_ref)
        k4_s[...] = jnp.zeros_like(k4_s)
        v4_s[:, :, 0:LANES] = jnp.zeros((N_KV_HEADS, KEY_PAD, LANES), BF16)
        v4_s[:, :, LANES:] = jnp.ones((N_KV_HEADS, KEY_PAD, LANES), BF16)

    k_all = jnp.concatenate([ck_ref[0], z[_ALL_ROWS, OFF_K:OFF_K + D_KV]], axis=0)
    v_all = jnp.concatenate([cv_ref[0], z[_ALL_ROWS, OFF_V:OFF_V + D_KV]], axis=0)
    for kv in range(N_KV_HEADS):
        k4_s[kv, 0:n_keys, :] = _tile_heads(k_all, kv, GQA_GROUP)
        v4_s[kv, 0:n_keys, 0:LANES] = _tile_heads(v_all, kv, 2)
    for kv in range(N_KV_HEADS):
        scores = _attn_scores(z[_ALL_ROWS, OFF_Q + kv * GROUP_W:OFF_Q + (kv + 1) * GROUP_W], k4_s[kv])
        o = _attn_finish(scores, v4_s[kv], bias_s[kv], sink_s[kv])
        ga = z[_ALL_ROWS, OFF_GA + kv * GROUP_W:OFF_GA + (kv + 1) * GROUP_W]
        ua[:, kv * GROUP_W:(kv + 1) * GROUP_W] = (o * (ga * _sigmoid(ga))).astype(ua.dtype)
    kwin_ref[0] = k_all[n_keys - WINDOW:, :]
    vwin_ref[0] = v_all[n_keys - WINDOW:, :]

    ctx0 = SUBLANES - (CONV_WIDTH - 1)
    xbuf[0:SUBLANES, :] = jnp.zeros((SUBLANES, D_LRU), F32)
    for j in range(CONV_WIDTH - 1):
        xbuf[ctx0 + j:ctx0 + j + 1, :] = sc_ref[j, pl.ds(b, 1), :]
    hc[...] = jnp.broadcast_to(sh_ref[pl.ds(b, 1), :], (SUBLANES, D_LRU))
    _lru(z, ul_ref.at[0], t_rows, xbuf, a_s, b_s, hc, cw_ref, cb_ref, wg_ref, bg_ref, lam_ref)
    for j in range(CONV_WIDTH - 1):
        conv_ref[j, pl.ds(b, 1), :] = xbuf[ctx0 + j:ctx0 + j + 1, :]
    hout_ref[pl.ds(b, 1), :] = hc[0:1, :]


def _sample_mixer(z3, cache_k, cache_v, state_conv_t, state_h, sinks, cw, cb, wg, bg, lam):
    bsz, t_rows, _ = z3.shape
    n_cache = cache_k.shape[1]
    assert n_cache + t_rows <= KEY_PAD and n_cache + t_rows >= WINDOW
    assert t_rows % SUBLANES == 0 and t_rows & (t_rows - 1) == 0 and t_rows >= CONV_WIDTH - 1
    smem = pl.BlockSpec(memory_space=pltpu.SMEM)
    per_b = lambda shape: pl.BlockSpec((1,) + shape, lambda b: (b, 0, 0))
    whole = lambda shape: pl.BlockSpec(shape, lambda b: (0,) * len(shape))
    return pl.pallas_call(
        functools.partial(_sample_mixer_kernel, t_rows=t_rows, n_cache=n_cache),
        grid=(bsz,),
        in_specs=[smem, per_b((t_rows, D_IN)), per_b((n_cache, D_KV)), per_b((n_cache, D_KV)),
                  whole((CONV_WIDTH - 1, bsz, D_LRU)), whole((bsz, D_LRU)),
                  _resident((CONV_WIDTH, D_LRU)), _resident((1, D_LRU)),
                  _resident((N_GATE_TILES, GATE_TILE, 2 * GATE_TILE)),
                  _resident((2, D_LRU)), _resident((1, D_LRU))],
        out_specs=[per_b((t_rows, D_ATTN)), per_b((t_rows, D_LRU)), per_b((WINDOW, D_KV)), per_b((WINDOW, D_KV)),
                   whole((CONV_WIDTH - 1, bsz, D_LRU)), whole((bsz, D_LRU))],
        out_shape=[jax.ShapeDtypeStruct((bsz, t_rows, D_ATTN), BF16),
                   jax.ShapeDtypeStruct((bsz, t_rows, D_LRU), BF16),
                   jax.ShapeDtypeStruct((bsz, WINDOW, D_KV), F32),
                   jax.ShapeDtypeStruct((bsz, WINDOW, D_KV), F32),
                   jax.ShapeDtypeStruct((CONV_WIDTH - 1, bsz, D_LRU), F32),
                   jax.ShapeDtypeStruct((bsz, D_LRU), F32)],
        scratch_shapes=[pltpu.VMEM((N_KV_HEADS, GQA_GROUP * t_rows, KEY_PAD), F32),
                        pltpu.VMEM((N_KV_HEADS, GQA_GROUP * t_rows, LANES), F32),
                        pltpu.VMEM((N_KV_HEADS, KEY_PAD, GROUP_W), BF16),
                        pltpu.VMEM((N_KV_HEADS, KEY_PAD, V_EXT_W), BF16),
                        pltpu.VMEM((SUBLANES + t_rows, D_LRU), F32),
                        pltpu.VMEM((t_rows, D_LRU), F32),
                        pltpu.VMEM((t_rows, D_LRU), F32),
                        pltpu.VMEM((SUBLANES, D_LRU), F32)],
        compiler_params=pltpu.CompilerParams(
            dimension_semantics=("arbitrary",), vmem_limit_bytes=VMEM_LIMIT_BYTES),
        name="sample_mixer",
    )(sinks, z3, cache_k, cache_v, state_conv_t, state_h, cw, cb, wg, bg, lam)


def _gate_weight_tiles(w_a, w_x):
    per = GATE_TILE // LRU_BLOCK
    w5 = jnp.stack([w_a, w_x]).reshape(2, N_GATE_TILES, per, LRU_BLOCK, LRU_BLOCK)
    eye = jnp.eye(per, dtype=w_a.dtype)
    return jnp.einsum('gtade,ab->tadgbe', w5, eye).reshape(N_GATE_TILES, GATE_TILE, 2 * GATE_TILE).astype(BF16)


def kernel(x_prompt, x_sample, cache_k, cache_v, state_conv, state_h, w_in, b_in, conv_w, conv_b,
           w_gate_a, b_gate_a, w_gate_x, b_gate_x, lru_lambda, attn_sinks, w_out, ln_g, ln_b):
    assert w_in.shape[0] == DEPTH == 1
    bsz, seq, _ = x_prompt.shape
    dbsz, dseq, _ = x_sample.shape
    n_cache = cache_k.shape[2]
    n_p, n_s = bsz * seq, dbsz * dseq
    assert seq % ROW_TILE == 0 and n_p % OUT_ROW_TILE == 0

    w_in_bf = w_in[0].astype(BF16)
    w_out_bf = w_out[0].astype(BF16)
    wg = _gate_weight_tiles(w_gate_a[0], w_gate_x[0])
    bg = jnp.stack([b_gate_a[0].reshape(D_LRU), b_gate_x[0].reshape(D_LRU)])
    cw = conv_w[0]
    sinks = attn_sinks[0]

    uap, ulp, kp, vp, cp, hp = _prompt_mixer(x_prompt, sinks, w_in_bf, b_in, cw, conv_b, wg, bg, lru_lambda)
    yp = _out_proj(uap.reshape(n_p, D_ATTN), ulp.reshape(n_p, D_LRU), x_prompt.reshape(n_p, D_MODEL),
                   w_out_bf, ln_g, ln_b).reshape(bsz, seq, D_MODEL)

    xs2 = x_sample.reshape(n_s, D_MODEL)
    zs = _in_proj(xs2, w_in_bf, b_in).reshape(dbsz, dseq, D_IN)
    uas, uls, ks, vs, cs, hs = _sample_mixer(
        zs, cache_k[0].reshape(dbsz, n_cache, D_KV), cache_v[0].reshape(dbsz, n_cache, D_KV),
        state_conv[0].transpose(1, 0, 2), state_h[0],
        sinks, cw, conv_b, wg, bg, lru_lambda)
    ys = _out_proj(uas.reshape(n_s, D_ATTN), uls.reshape(n_s, D_LRU), xs2,
                   w_out_bf, ln_g, ln_b).reshape(dbsz, dseq, D_MODEL)

    def from_channel_major(w):
        return w.reshape(bsz, N_KV_HEADS, HEAD_DIM, WINDOW).transpose(0, 3, 1, 2)[None]

    def from_row_major(w):
        return w.reshape(1, dbsz, WINDOW, N_KV_HEADS, HEAD_DIM)

    return (yp, ys,
            from_channel_major(kp), from_channel_major(vp), cp.transpose(1, 0, 2)[None], hp[None],
            from_row_major(ks), from_row_major(vs), cs.transpose(1, 0, 2)[None], hs[None])
```

```python
import functools

import jax
import jax.numpy as jnp
from jax import lax
from jax.experimental import pallas as pl
from jax.experimental.pallas import tpu as pltpu

F32 = jnp.float32
BF16 = jnp.bfloat16

D_MODEL = 2048
DEPTH = 1
CHUNK = 64
WINDOW = 128
WINDOW_CHUNKS = WINDOW // CHUNK
HEAD_DIM = 64
D_ATTN = D_MODEL // 2
D_LRU = D_MODEL - D_ATTN
N_Q_HEADS = D_ATTN // HEAD_DIM
N_KV_HEADS = 4
GQA_GROUP = N_Q_HEADS // N_KV_HEADS
D_KV = N_KV_HEADS * HEAD_DIM
N_LRU_BLOCKS = 16
LRU_BLOCK = D_LRU // N_LRU_BLOCKS
CONV_WIDTH = 4
LRU_C = 8.0
D_IN = 2 * D_ATTN + 2 * D_KV + 2 * D_LRU
PAST_LEN = 2048
DEEPNORM_ALPHA = (2.0 * DEPTH) ** 0.25
LN_EPS = 1e-5
NEG_INF = -1e30

OFF_Q = 0
OFF_K = D_ATTN
OFF_V = D_ATTN + D_KV
OFF_GA = D_ATTN + 2 * D_KV
OFF_XL = OFF_GA + D_ATTN
OFF_GL = OFF_XL + D_LRU

SUBLANES = 8
LANES = 128
MXU_DIM = 256
VMEM_LIMIT_BYTES = 56 * 1024 * 1024

GROUP_W = GQA_GROUP * HEAD_DIM
KEY_PAD = 2 * LANES
V_EXT_W = 2 * LANES
GATE_TILE = MXU_DIM
N_GATE_TILES = D_LRU // GATE_TILE
ROW_TILE = 256
OUT_ROW_TILE = 512
UNITS_PER_LATE_PIECE = 4
SCORE_LOOKAHEAD = 2
SCORE_RING = SCORE_LOOKAHEAD + 2
N_TILE = 512
N_CHUNKS_PER_TILE = ROW_TILE // CHUNK
KBUF_ROWS = WINDOW + ROW_TILE + (KEY_PAD - (WINDOW + CHUNK))

_SLOPES = tuple(2.0 ** (-8.0 * h / N_Q_HEADS) for h in range(1, N_Q_HEADS + 1))


def _resident(shape):
    return pl.BlockSpec(shape, lambda *_: (0,) * len(shape), pipeline_mode=pl.Buffered(1))


class _ColGroups:
    def __init__(self, groups):
        self._groups = groups

    def _locate(self, idx):
        rows, cols = idx
        for first, end, ref in self._groups:
            if first <= cols.start and cols.stop <= end:
                return ref, rows, slice(cols.start - first, cols.stop - first)
        raise ValueError(f"columns {cols} straddle projection groups")

    def __getitem__(self, idx):
        ref, rows, cols = self._locate(idx)
        return ref[rows, cols]

    def __setitem__(self, idx, value):
        ref, rows, cols = self._locate(idx)
        ref[rows, cols] = value


_ALL_ROWS = slice(None)
_PROJ_GROUPS = ((OFF_Q, OFF_K), (OFF_K, OFF_GA), (OFF_GA, OFF_XL), (OFF_XL, OFF_GL), (OFF_GL, D_IN))


def _sigmoid(x):
    return 1.0 / (1.0 + jnp.exp(-x))


def _in_proj_into(xb, w_ref, b_ref, z, first=0, end=D_IN, width=N_TILE):
    for n in range(first, end, width):
        z[_ALL_ROWS, n:n + width] = (
            jnp.dot(xb, w_ref[:, n:n + width], preferred_element_type=F32) + b_ref[:, n:n + width])


def _bias_table(nq, kv, key0_pos, q0_pos, n_keys):
    rows = GQA_GROUP * nq
    r = lax.broadcasted_iota(jnp.int32, (rows, KEY_PAD), 0)
    c = lax.broadcasted_iota(jnp.int32, (rows, KEY_PAD), 1)
    shift = nq.bit_length() - 1
    g = jnp.right_shift(r, shift)
    qpos = q0_pos + (r & (nq - 1))
    kpos = key0_pos + c
    slope = jnp.full((rows, KEY_PAD), _SLOPES[kv * GQA_GROUP], F32)
    for gg in range(1, GQA_GROUP):
        slope = jnp.where(g == gg, _SLOPES[kv * GQA_GROUP + gg], slope)
    dist = jnp.abs(qpos - kpos).astype(F32)
    cshift = CHUNK.bit_length() - 1
    qc = jnp.right_shift(qpos, cshift)
    kc = jnp.right_shift(kpos, cshift)
    valid = (kpos >= 0) & (kc <= qc) & (kc >= qc - WINDOW_CHUNKS) & (c < n_keys)
    return jnp.where(valid, -(slope * dist), NEG_INF)


def _tile_heads(x, kv, copies):
    pair = x[:, (kv // 2) * LANES:(kv // 2 + 1) * LANES]
    rolled = pltpu.roll(pair, HEAD_DIM, axis=1)
    lane = lax.broadcasted_iota(jnp.int32, pair.shape, 1)
    low = lane < HEAD_DIM
    both = jnp.where(low, pair, rolled) if kv % 2 == 0 else jnp.where(low, rolled, pair)
    both = both.astype(BF16)
    return jnp.concatenate([both] * (copies * HEAD_DIM // LANES), axis=1)


def _sink_table(nq, kv, sinks_ref):
    return jnp.concatenate(
        [jnp.full((nq, LANES), sinks_ref[kv * GQA_GROUP + g], F32) for g in range(GQA_GROUP)], axis=0)


def _head_of_lane(nq):
    lane = lax.broadcasted_iota(jnp.int32, (nq, GROUP_W), 1)
    return jnp.right_shift(lane, HEAD_DIM.bit_length() - 1)


def _attn_scores(q, k4):
    nq = q.shape[0]
    qb = (q * (HEAD_DIM ** -0.5)).astype(BF16)
    head_of_lane = _head_of_lane(nq)
    zero = jnp.zeros_like(qb)
    qs = jnp.concatenate([jnp.where(head_of_lane == g, qb, zero) for g in range(GQA_GROUP)], axis=0)
    return lax.dot_general(qs, k4, (((1,), (1,)), ((), ())), preferred_element_type=F32)


def _attn_finish(scores, v4e, bias, sink):
    nq = scores.shape[0] // GQA_GROUP
    s = scores + bias
    m = jnp.maximum(jnp.broadcast_to(jnp.max(s, axis=-1, keepdims=True), sink.shape), sink)
    e = jnp.concatenate(
        [jnp.exp(s[:, j * LANES:(j + 1) * LANES] - m) for j in range(KEY_PAD // LANES)], axis=1)
    o4e = jnp.dot(e.astype(BF16), v4e, preferred_element_type=F32)
    o2 = o4e[:, :LANES] * (1.0 / (o4e[:, LANES:] + jnp.exp(sink - m)))
    low = lax.broadcasted_iota(jnp.int32, (nq, LANES), 1) < HEAD_DIM
    return jnp.concatenate(
        [jnp.where(low, o2[g * nq:(g + 1) * nq], o2[(g + 1) * nq:(g + 2) * nq]) for g in range(0, GQA_GROUP, 2)],
        axis=1)


def _lru(z, u_out, t_rows, xbuf, a_s, b_s, hc, cw_ref, cb_ref, wg_ref, bg_ref, lam_ref, fill=None):
    fill = fill or (lambda: None)
    neg = -lam_ref[...]
    softplus = jnp.maximum(neg, 0.0) + jnp.log1p(jnp.exp(-jnp.abs(neg)))
    coef = -LRU_C * softplus
    tiles = [slice(j * GATE_TILE, (j + 1) * GATE_TILE) for j in range(N_GATE_TILES)]

    def conv_and_gate_matmul(cols, j):
        xl = z[_ALL_ROWS, OFF_XL + cols.start:OFF_XL + cols.stop]
        xbuf[SUBLANES:SUBLANES + t_rows, cols] = xl
        xc = cb_ref[:, cols] + cw_ref[CONV_WIDTH - 1:CONV_WIDTH, cols] * xl
        for k in range(CONV_WIDTH - 1):
            off = SUBLANES - (CONV_WIDTH - 1) + k
            xc = xc + cw_ref[k:k + 1, cols] * xbuf[off:off + t_rows, cols]
        pre = jnp.dot(xc.astype(BF16), wg_ref[j], preferred_element_type=F32)
        a_s[:, cols] = pre[:, :GATE_TILE]
        b_s[:, cols] = pre[:, GATE_TILE:]
        return xc

    def gate_math(cols, xc):
        r = _sigmoid(a_s[:, cols] + bg_ref[0:1, cols])
        i = _sigmoid(b_s[:, cols] + bg_ref[1:2, cols])
        a = jnp.exp(coef[:, cols] * r)
        mult = jnp.sqrt(jnp.maximum(1.0 - a * a, 0.0))
        a_s[:, cols] = a
        b_s[:, cols] = mult * (i * xc)

    fill()
    xc_prev = None
    for j, cols in enumerate(tiles):
        xc = conv_and_gate_matmul(cols, j)
        if xc_prev is not None:
            gate_math(tiles[j - 1], xc_prev)
        xc_prev = xc
        fill()
    gate_math(tiles[-1], xc_prev)

    row = lax.broadcasted_iota(jnp.int32, (SUBLANES, D_LRU), 0)
    h_prev = hc[...]
    for gi in range(t_rows // SUBLANES):
        rows = slice(gi * SUBLANES, (gi + 1) * SUBLANES)
        a8 = a_s[rows, :]
        b8 = b_s[rows, :]
        for d in (1, 2, 4):
            keep = row >= d
            a_sh = pltpu.roll(a8, d, axis=0)
            b_sh = pltpu.roll(b8, d, axis=0)
            b8 = jnp.where(keep, a8 * b_sh + b8, b8)
            a8 = jnp.where(keep, a8 * a_sh, a8)
        h8 = a8 * h_prev + b8
        b_s[rows, :] = h8
        h_prev = jnp.broadcast_to(h8[SUBLANES - 1:SUBLANES, :], (SUBLANES, D_LRU))
    hc[...] = h_prev
    gl = z[_ALL_ROWS, OFF_GL:OFF_GL + D_LRU]
    u_out[...] = (b_s[...] * (gl * _sigmoid(gl))).astype(u_out.dtype)
    xbuf[0:SUBLANES, :] = xbuf[t_rows:t_rows + SUBLANES, :]


def _prompt_kernel(sinks_ref, x_ref, win_ref, bin_ref, cw_ref, cb_ref, wg_ref, bg_ref, lam_ref,
                   ua_ref, ul_ref, kwin_ref, vwin_ref, conv_ref, hout_ref,
                   zq, zkv, zga, zxl, zgl, bias_s, mask_s, sink_s, s_ring, k4_s, v4_s, xbuf, a_s, b_s, hc,
                   *, n_t, n_tiles):
    z = _ColGroups([(first, end, ref) for (first, end), ref in zip(_PROJ_GROUPS, (zq, zkv, zga, zxl, zgl))])
    ua = ua_ref.at[0]
    s = pl.program_id(0)
    t_a = lax.rem(jnp.minimum(s, n_tiles - 1), n_t)
    t_l = lax.rem(jnp.maximum(s - 1, 0), n_t)

    @pl.when(s == 0)
    def _init():
        for kv in range(N_KV_HEADS):
            bias_s[kv] = _bias_table(CHUNK, kv, 0, WINDOW, WINDOW + CHUNK)
            sink_s[kv] = _sink_table(CHUNK, kv, sinks_ref)
        col = lax.broadcasted_iota(jnp.int32, (SUBLANES, KEY_PAD), 1)
        for ci in range(WINDOW_CHUNKS):
            mask_s[ci] = jnp.where(col < WINDOW - ci * CHUNK, NEG_INF, 0.0)
        k4_s[...] = jnp.zeros_like(k4_s)
        v4_s[:, :, 0:LANES] = jnp.zeros((N_KV_HEADS, KBUF_ROWS, LANES), BF16)
        v4_s[:, :, LANES:] = jnp.ones((N_KV_HEADS, KBUF_ROWS, LANES), BF16)
        zxl[...] = jnp.zeros_like(zxl)
        zgl[...] = jnp.zeros_like(zgl)

    @pl.when(t_a == 0)
    def _reset_attention_carry():
        k4_s[:, 0:WINDOW, :] = jnp.zeros((N_KV_HEADS, WINDOW, GROUP_W), BF16)
        v4_s[:, 0:WINDOW, 0:LANES] = jnp.zeros((N_KV_HEADS, WINDOW, LANES), BF16)

    @pl.when(t_l == 0)
    def _reset_lru_carry():
        xbuf[0:SUBLANES, :] = jnp.zeros((SUBLANES, D_LRU), F32)
        hc[...] = jnp.zeros_like(hc)

    xb = x_ref[0].astype(BF16)
    early_cols = list(range(OFF_K, OFF_GA, N_TILE)) + list(range(OFF_Q, OFF_K, N_TILE)) + list(
        range(OFF_GA, OFF_XL, N_TILE))
    late_cols = list(range(OFF_XL, D_IN, N_TILE))

    def project(cols):
        if cols:
            c0 = cols.pop(0)
            _in_proj_into(xb, win_ref, bin_ref, z, c0, c0 + N_TILE)

    _lru(z, ul_ref.at[0], ROW_TILE, xbuf, a_s, b_s, hc, cw_ref, cb_ref, wg_ref, bg_ref, lam_ref,
         fill=lambda: project(early_cols))
    while early_cols:
        project(early_cols)

    k_new = z[_ALL_ROWS, OFF_K:OFF_K + D_KV]
    v_new = z[_ALL_ROWS, OFF_V:OFF_V + D_KV]
    for kv in range(N_KV_HEADS):
        k4_s[kv, WINDOW:WINDOW + ROW_TILE, :] = _tile_heads(k_new, kv, GQA_GROUP)
        v4_s[kv, WINDOW:WINDOW + ROW_TILE, 0:LANES] = _tile_heads(v_new, kv, 2)

    units = [(ci, kv) for ci in range(N_CHUNKS_PER_TILE) for kv in range(N_KV_HEADS)]

    def scores_into_ring(idx):
        ci, kv = units[idx]
        s_ring[idx % SCORE_RING] = _attn_scores(
            z[ci * CHUNK:(ci + 1) * CHUNK, OFF_Q + kv * GROUP_W:OFF_Q + (kv + 1) * GROUP_W],
            k4_s[kv, ci * CHUNK:ci * CHUNK + KEY_PAD, :])

    mask_rows = [jnp.where(t_a == 0, mask_s[ci, 0:1, :], 0.0) for ci in range(WINDOW_CHUNKS)]
    for idx in range(SCORE_LOOKAHEAD):
        scores_into_ring(idx)
    for idx, (ci, kv) in enumerate(units):
        if idx + SCORE_LOOKAHEAD < len(units):
            scores_into_ring(idx + SCORE_LOOKAHEAD)
        if idx % UNITS_PER_LATE_PIECE == 1:
            project(late_cols)
        s_cur = s_ring[idx % SCORE_RING]
        rows = slice(ci * CHUNK, (ci + 1) * CHUNK)
        bias = bias_s[kv] + mask_rows[ci] if ci < WINDOW_CHUNKS else bias_s[kv]
        o = _attn_finish(s_cur, v4_s[kv, ci * CHUNK:ci * CHUNK + KEY_PAD, :], bias, sink_s[kv])
        ga = z[rows, OFF_GA + kv * GROUP_W:OFF_GA + (kv + 1) * GROUP_W]
        ua[rows, kv * GROUP_W:(kv + 1) * GROUP_W] = (o * (ga * _sigmoid(ga))).astype(ua.dtype)
    assert not early_cols and not late_cols

    @pl.when(t_l == n_t - 1)
    def _emit_lru_state():
        seq_l = lax.div(jnp.maximum(s - 1, 0), n_t)
        for j in range(CONV_WIDTH - 1):
            row = SUBLANES - (CONV_WIDTH - 1) + j
            conv_ref[j, pl.ds(seq_l, 1), :] = xbuf[row:row + 1, :]
        hout_ref[pl.ds(seq_l, 1), :] = hc[0:1, :]

    @pl.when(t_a == n_t - 1)
    def _emit_kv_window():
        kwin_ref[0] = z[slice(ROW_TILE - WINDOW, ROW_TILE), OFF_K:OFF_K + D_KV].T
        vwin_ref[0] = z[slice(ROW_TILE - WINDOW, ROW_TILE), OFF_V:OFF_V + D_KV].T

    @pl.when(s < n_tiles - 1)
    def _carry_kv():
        k4_s[:, 0:WINDOW, :] = k4_s[:, ROW_TILE:ROW_TILE + WINDOW, :]
        v4_s[:, 0:WINDOW, 0:LANES] = v4_s[:, ROW_TILE:ROW_TILE + WINDOW, 0:LANES]


def _prompt_mixer(x3, sinks, w_in_bf, b_in_row, cw, cb, wg, bg, lam):
    bsz, seq, _ = x3.shape
    n_t = seq // ROW_TILE
    n_tiles = bsz * n_t
    smem = pl.BlockSpec(memory_space=pltpu.SMEM)

    def attn_tile(s):
        ta = jnp.minimum(s, n_tiles - 1)
        return ta // n_t, ta % n_t

    def lru_tile(s):
        tl = jnp.maximum(s - 1, 0)
        return tl // n_t, tl % n_t

    return pl.pallas_call(
        functools.partial(_prompt_kernel, n_t=n_t, n_tiles=n_tiles),
        grid=(n_tiles + 1,),
        in_specs=[smem,
                  pl.BlockSpec((1, ROW_TILE, D_MODEL), lambda s: (*attn_tile(s), 0)),
                  _resident((D_MODEL, D_IN)), _resident((1, D_IN)),
                  _resident((CONV_WIDTH, D_LRU)), _resident((1, D_LRU)),
                  _resident((N_GATE_TILES, GATE_TILE, 2 * GATE_TILE)),
                  _resident((2, D_LRU)), _resident((1, D_LRU))],
        out_specs=[pl.BlockSpec((1, ROW_TILE, D_ATTN), lambda s: (*attn_tile(s), 0)),
                   pl.BlockSpec((1, ROW_TILE, D_LRU), lambda s: (*lru_tile(s), 0)),
                   pl.BlockSpec((1, D_KV, WINDOW), lambda s: (attn_tile(s)[0], 0, 0)),
                   pl.BlockSpec((1, D_KV, WINDOW), lambda s: (attn_tile(s)[0], 0, 0)),
                   pl.BlockSpec((CONV_WIDTH - 1, bsz, D_LRU), lambda s: (0, 0, 0)),
                   pl.BlockSpec((bsz, D_LRU), lambda s: (0, 0))],
        out_shape=[jax.ShapeDtypeStruct((bsz, seq, D_ATTN), BF16),
                   jax.ShapeDtypeStruct((bsz, seq, D_LRU), BF16),
                   jax.ShapeDtypeStruct((bsz, D_KV, WINDOW), F32),
                   jax.ShapeDtypeStruct((bsz, D_KV, WINDOW), F32),
                   jax.ShapeDtypeStruct((CONV_WIDTH - 1, bsz, D_LRU), F32),
                   jax.ShapeDtypeStruct((bsz, D_LRU), F32)],
        scratch_shapes=[pltpu.VMEM((ROW_TILE, end - first), F32) for first, end in _PROJ_GROUPS] + [
                        pltpu.VMEM((N_KV_HEADS, GQA_GROUP * CHUNK, KEY_PAD), F32),
                        pltpu.VMEM((WINDOW_CHUNKS, SUBLANES, KEY_PAD), F32),
                        pltpu.VMEM((N_KV_HEADS, GQA_GROUP * CHUNK, LANES), F32),
                        pltpu.VMEM((SCORE_RING, GQA_GROUP * CHUNK, KEY_PAD), F32),
                        pltpu.VMEM((N_KV_HEADS, KBUF_ROWS, GROUP_W), BF16),
                        pltpu.VMEM((N_KV_HEADS, KBUF_ROWS, V_EXT_W), BF16),
                        pltpu.VMEM((SUBLANES + ROW_TILE, D_LRU), F32),
                        pltpu.VMEM((ROW_TILE, D_LRU), F32),
                        pltpu.VMEM((ROW_TILE, D_LRU), F32),
                        pltpu.VMEM((SUBLANES, D_LRU), F32)],
        compiler_params=pltpu.CompilerParams(
            dimension_semantics=("arbitrary",), vmem_limit_bytes=VMEM_LIMIT_BYTES),
        name="prompt_mixer",
    )(sinks, x3, w_in_bf, b_in_row, cw, cb, wg, bg, lam)


def _out_proj_kernel(ua_ref, ul_ref, x_ref, w_ref, g_ref, b_ref, y_ref):
    y = (DEEPNORM_ALPHA * x_ref[...]
         + jnp.dot(ua_ref[...], w_ref[0:D_ATTN, :], preferred_element_type=F32)
         + jnp.dot(ul_ref[...], w_ref[D_ATTN:, :], preferred_element_type=F32))
    mu = jnp.mean(y, axis=-1, keepdims=True)
    yc = y - mu
    var = jnp.mean(yc * yc, axis=-1, keepdims=True)
    y_ref[...] = yc * lax.rsqrt(var + LN_EPS) * g_ref[...] + b_ref[...]


def _out_proj(ua2d, ul2d, x2d, w_bf, g_row, b_row):
    m = x2d.shape[0]
    tm = min(OUT_ROW_TILE, m)
    return pl.pallas_call(
        _out_proj_kernel,
        grid=(m // tm,),
        in_specs=[pl.BlockSpec((tm, D_ATTN), lambda i: (i, 0)),
                  pl.BlockSpec((tm, D_LRU), lambda i: (i, 0)),
                  pl.BlockSpec((tm, D_MODEL), lambda i: (i, 0)),
                  _resident((D_MODEL, D_MODEL)),
                  _resident((1, D_MODEL)),
                  _resident((1, D_MODEL))],
        out_specs=pl.BlockSpec((tm, D_MODEL), lambda i: (i, 0)),
        out_shape=jax.ShapeDtypeStruct((m, D_MODEL), F32),
        compiler_params=pltpu.CompilerParams(
            dimension_semantics=("arbitrary",), vmem_limit_bytes=VMEM_LIMIT_BYTES),
        name="out_proj",
    )(ua2d, ul2d, x2d, w_bf, g_row, b_row)


def _out_proj_chunked_kernel(ua_ref, ul_ref, x_ref, w_ref, g_ref, b_ref, y_ref, *, n_attn_steps):
    k = pl.program_id(0)

    @pl.when(k == 0)
    def _start_from_residual():
        y_ref[...] = DEEPNORM_ALPHA * x_ref[...]

    def accumulate(u_ref):
        ub = u_ref[...]
        for n in range(0, D_MODEL, N_TILE):
            y_ref[:, n:n + N_TILE] += jnp.dot(ub, w_ref[:, n:n + N_TILE], preferred_element_type=F32)

    pl.when(k < n_attn_steps)(lambda: accumulate(ua_ref))
    pl.when(k >= n_attn_steps)(lambda: accumulate(ul_ref))

    @pl.when(k == pl.num_programs(0) - 1)
    def _layer_norm():
        y = y_ref[...]
        mu = jnp.mean(y, axis=-1, keepdims=True)
        yc = y - mu
        var = jnp.mean(yc * yc, axis=-1, keepdims=True)
        y_ref[...] = yc * lax.rsqrt(var + LN_EPS) * g_ref[...] + b_ref[...]


def _out_proj_chunked(ua2d, ul2d, x2d, w_bf, g_row, b_row):
    m = x2d.shape[0]
    n_attn_steps = D_ATTN // MXU_DIM
    return pl.pallas_call(
        functools.partial(_out_proj_chunked_kernel, n_attn_steps=n_attn_steps),
        grid=(D_MODEL // MXU_DIM,),
        in_specs=[pl.BlockSpec((m, MXU_DIM), lambda k: (0, jnp.minimum(k, n_attn_steps - 1))),
                  pl.BlockSpec((m, MXU_DIM), lambda k: (0, jnp.maximum(k - n_attn_steps, 0))),
                  pl.BlockSpec((m, D_MODEL), lambda k: (0, 0)),
                  pl.BlockSpec((MXU_DIM, D_MODEL), lambda k: (k, 0)),
                  pl.BlockSpec((1, D_MODEL), lambda k: (0, 0)),
                  pl.BlockSpec((1, D_MODEL), lambda k: (0, 0))],
        out_specs=pl.BlockSpec((m, D_MODEL), lambda k: (0, 0)),
        out_shape=jax.ShapeDtypeStruct((m, D_MODEL), F32),
        compiler_params=pltpu.CompilerParams(
            dimension_semantics=("arbitrary",), vmem_limit_bytes=VMEM_LIMIT_BYTES),
        name="out_proj_chunked",
    )(ua2d, ul2d, x2d, w_bf, g_row, b_row)


def _in_proj_kernel(x_ref, w_ref, b_ref, z_ref):
    @pl.when(pl.program_id(0) == 0)
    def _start_from_bias():
        z_ref[...] = jnp.broadcast_to(b_ref[...], z_ref.shape)

    xb = x_ref[...].astype(BF16)
    for n in range(0, D_IN, N_TILE):
        z_ref[:, n:n + N_TILE] += jnp.dot(xb, w_ref[:, n:n + N_TILE], preferred_element_type=F32)


def _in_proj(x2d, w_bf, b_row):
    m = x2d.shape[0]
    return pl.pallas_call(
        _in_proj_kernel,
        grid=(D_MODEL // MXU_DIM,),
        in_specs=[pl.BlockSpec((m, MXU_DIM), lambda k: (0, k)),
                  pl.BlockSpec((MXU_DIM, D_IN), lambda k: (k, 0)),
                  pl.BlockSpec((1, D_IN), lambda k: (0, 0))],
        out_specs=pl.BlockSpec((m, D_IN), lambda k: (0, 0)),
        out_shape=jax.ShapeDtypeStruct((m, D_IN), F32),
        compiler_params=pltpu.CompilerParams(
            dimension_semantics=("arbitrary",), vmem_limit_bytes=VMEM_LIMIT_BYTES),
        name="in_proj",
    )(x2d, w_bf, b_row)


def _sample_mixer_kernel(sinks_ref, z_ref, ck_ref, cv_ref, sc_ref, sh_ref,
                         cw_ref, cb_ref, wg_ref, bg_ref, lam_ref,
                         ua_ref, ul_ref, kwin_ref, vwin_ref, conv_ref, hout_ref,
                         bias_s, sink_s, s_ring, k4_s, v4_s, xbuf, a_s, b_s, hc, *, t_rows, n_cache):
    b = pl.program_id(0)
    n_keys = n_cache + t_rows
    z = _ColGroups([(0, D_IN, z_ref.at[0])])
    ua = ua_ref.at[0]

    @pl.when(b == 0)
    def _init_tables():
        for kv in range(N_KV_HEADS):
            bias_s[kv] = _bias_table(t_rows, kv, PAST_LEN - n_cache, PAST_LEN, n_keys)
            sink_s[kv] = _sink_table(t_rows, kv, sinks_ref)
        k4_s[...] = jnp.zeros_like(k4_s)
        v4_s[:, :, 0:LANES] = jnp.zeros((N_KV_HEADS, KEY_PAD, LANES), BF16)
        v4_s[:, :, LANES:] = jnp.ones((N_KV_HEADS, KEY_PAD, LANES), BF16)

    k_all = jnp.concatenate([ck_ref[0], z[_ALL_ROWS, OFF_K:OFF_K + D_KV]], axis=0)
    v_all = jnp.concatenate([cv_ref[0], z[_ALL_ROWS, OFF_V:OFF_V + D_KV]], axis=0)
    for kv in range(N_KV_HEADS):
        k4_s[kv, 0:n_keys, :] = _tile_heads(k_all, kv, GQA_GROUP)
        v4_s[kv, 0:n_keys, 0:LANES] = _tile_heads(v_all, kv, 2)
    for kv in range(N_KV_HEADS):
        s_ring[kv] = _attn_scores(z[_ALL_ROWS, OFF_Q + kv * GROUP_W:OFF_Q + (kv + 1) * GROUP_W], k4_s[kv])
    for kv in range(N_KV_HEADS):
        o = _attn_finish(s_ring[kv], v4_s[kv], bias_s[kv], sink_s[kv])
        ga = z[_ALL_ROWS, OFF_GA + kv * GROUP_W:OFF_GA + (kv + 1) * GROUP_W]
        ua[:, kv * GROUP_W:(kv + 1) * GROUP_W] = (o * (ga * _sigmoid(ga))).astype(ua.dtype)
    kwin_ref[0] = k_all[n_keys - WINDOW:, :]
    vwin_ref[0] = v_all[n_keys - WINDOW:, :]

    xbuf[0:SUBLANES, :] = jnp.zeros((SUBLANES, D_LRU), F32)
    xbuf[SUBLANES - (CONV_WIDTH - 1):SUBLANES, :] = sc_ref[0]
    hc[...] = jnp.broadcast_to(sh_ref[0], (SUBLANES, D_LRU))
    _lru(z, ul_ref.at[0], t_rows, xbuf, a_s, b_s, hc, cw_ref, cb_ref, wg_ref, bg_ref, lam_ref)
    conv_ref[0] = xbuf[SUBLANES - (CONV_WIDTH - 1):SUBLANES, :]
    hout_ref[0] = hc[0:1, :]


def _sample_mixer(z3, cache_k, cache_v, state_conv, state_h, sinks, cw, cb, wg, bg, lam):
    bsz, t_rows, _ = z3.shape
    n_cache = cache_k.shape[1]
    assert n_cache + t_rows <= KEY_PAD and n_cache + t_rows >= WINDOW
    assert t_rows % SUBLANES == 0 and t_rows & (t_rows - 1) == 0 and t_rows >= CONV_WIDTH - 1
    smem = pl.BlockSpec(memory_space=pltpu.SMEM)
    per_b = lambda shape: pl.BlockSpec((1,) + shape, lambda b: (b, 0, 0))
    return pl.pallas_call(
        functools.partial(_sample_mixer_kernel, t_rows=t_rows, n_cache=n_cache),
        grid=(bsz,),
        in_specs=[smem, per_b((t_rows, D_IN)), per_b((n_cache, D_KV)), per_b((n_cache, D_KV)),
                  per_b((CONV_WIDTH - 1, D_LRU)), per_b((1, D_LRU)),
                  _resident((CONV_WIDTH, D_LRU)), _resident((1, D_LRU)),
                  _resident((N_GATE_TILES, GATE_TILE, 2 * GATE_TILE)),
                  _resident((2, D_LRU)), _resident((1, D_LRU))],
        out_specs=[per_b((t_rows, D_ATTN)), per_b((t_rows, D_LRU)), per_b((WINDOW, D_KV)), per_b((WINDOW, D_KV)),
                   per_b((CONV_WIDTH - 1, D_LRU)), per_b((1, D_LRU))],
        out_shape=[jax.ShapeDtypeStruct((bsz, t_rows, D_ATTN), BF16),
                   jax.ShapeDtypeStruct((bsz, t_rows, D_LRU), BF16),
                   jax.ShapeDtypeStruct((bsz, WINDOW, D_KV), F32),
                   jax.ShapeDtypeStruct((bsz, WINDOW, D_KV), F32),
                   jax.ShapeDtypeStruct((bsz, CONV_WIDTH - 1, D_LRU), F32),
                   jax.ShapeDtypeStruct((bsz, 1, D_LRU), F32)],
        scratch_shapes=[pltpu.VMEM((N_KV_HEADS, GQA_GROUP * t_rows, KEY_PAD), F32),
                        pltpu.VMEM((N_KV_HEADS, GQA_GROUP * t_rows, LANES), F32),
                        pltpu.VMEM((N_KV_HEADS, GQA_GROUP * t_rows, KEY_PAD), F32),
                        pltpu.VMEM((N_KV_HEADS, KEY_PAD, GROUP_W), BF16),
                        pltpu.VMEM((N_KV_HEADS, KEY_PAD, V_EXT_W), BF16),
                        pltpu.VMEM((SUBLANES + t_rows, D_LRU), F32),
                        pltpu.VMEM((t_rows, D_LRU), F32),
                        pltpu.VMEM((t_rows, D_LRU), F32),
                        pltpu.VMEM((SUBLANES, D_LRU), F32)],
        compiler_params=pltpu.CompilerParams(
            dimension_semantics=("arbitrary",), vmem_limit_bytes=VMEM_LIMIT_BYTES),
        name="sample_mixer",
    )(sinks, z3, cache_k, cache_v, state_conv, state_h, cw, cb, wg, bg, lam)


def _gate_weight_tiles(w_a, w_x):
    per = GATE_TILE // LRU_BLOCK
    w5 = jnp.stack([w_a, w_x]).reshape(2, N_GATE_TILES, per, LRU_BLOCK, LRU_BLOCK)
    eye = jnp.eye(per, dtype=w_a.dtype)
    return jnp.einsum('gtade,ab->tadgbe', w5, eye).reshape(N_GATE_TILES, GATE_TILE, 2 * GATE_TILE).astype(BF16)


def kernel(x_prompt, x_sample, cache_k, cache_v, state_conv, state_h, w_in, b_in, conv_w, conv_b,
           w_gate_a, b_gate_a, w_gate_x, b_gate_x, lru_lambda, attn_sinks, w_out, ln_g, ln_b):
    assert w_in.shape[0] == DEPTH == 1
    bsz, seq, _ = x_prompt.shape
    dbsz, dseq, _ = x_sample.shape
    n_cache = cache_k.shape[2]
    n_p, n_s = bsz * seq, dbsz * dseq
    assert seq % ROW_TILE == 0 and n_p % OUT_ROW_TILE == 0

    w_in_bf = w_in[0].astype(BF16)
    w_out_bf = w_out[0].astype(BF16)
    wg = _gate_weight_tiles(w_gate_a[0], w_gate_x[0])
    bg = jnp.stack([b_gate_a[0].reshape(D_LRU), b_gate_x[0].reshape(D_LRU)])
    cw = conv_w[0]
    sinks = attn_sinks[0]

    uap, ulp, kp, vp, cp, hp = _prompt_mixer(x_prompt, sinks, w_in_bf, b_in, cw, conv_b, wg, bg, lru_lambda)
    yp = _out_proj(uap.reshape(n_p, D_ATTN), ulp.reshape(n_p, D_LRU), x_prompt.reshape(n_p, D_MODEL),
                   w_out_bf, ln_g, ln_b).reshape(bsz, seq, D_MODEL)

    xs2 = x_sample.reshape(n_s, D_MODEL)
    zs = _in_proj(xs2, w_in_bf, b_in).reshape(dbsz, dseq, D_IN)
    uas, uls, ks, vs, cs, hs = _sample_mixer(
        zs, cache_k[0].reshape(dbsz, n_cache, D_KV), cache_v[0].reshape(dbsz, n_cache, D_KV),
        state_conv[0], state_h[0].reshape(dbsz, 1, D_LRU),
        sinks, cw, conv_b, wg, bg, lru_lambda)
    ys = _out_proj_chunked(uas.reshape(n_s, D_ATTN), uls.reshape(n_s, D_LRU), xs2,
                           w_out_bf, ln_g, ln_b).reshape(dbsz, dseq, D_MODEL)

    def from_channel_major(w):
        return w.reshape(bsz, N_KV_HEADS, HEAD_DIM, WINDOW).transpose(0, 3, 1, 2)[None]

    def from_row_major(w):
        return w.reshape(1, dbsz, WINDOW, N_KV_HEADS, HEAD_DIM)

    return (yp, ys,
            from_channel_major(kp), from_channel_major(vp), cp.transpose(1, 0, 2)[None], hp[None],
            from_row_major(ks), from_row_major(vs), cs[None], hs.reshape(1, dbsz, D_LRU))
```

```python
import functools

import jax
import jax.numpy as jnp
from jax import lax
from jax.experimental import pallas as pl
from jax.experimental.pallas import tpu as pltpu

F32 = jnp.float32
BF16 = jnp.bfloat16

D_MODEL = 2048
DEPTH = 1
CHUNK = 64
WINDOW = 128
WINDOW_CHUNKS = WINDOW // CHUNK
HEAD_DIM = 64
D_ATTN = D_MODEL // 2
D_LRU = D_MODEL - D_ATTN
N_Q_HEADS = D_ATTN // HEAD_DIM
N_KV_HEADS = 4
GQA_GROUP = N_Q_HEADS // N_KV_HEADS
D_KV = N_KV_HEADS * HEAD_DIM
N_LRU_BLOCKS = 16
LRU_BLOCK = D_LRU // N_LRU_BLOCKS
CONV_WIDTH = 4
LRU_C = 8.0
D_IN = 2 * D_ATTN + 2 * D_KV + 2 * D_LRU
PAST_LEN = 2048
DEEPNORM_ALPHA = (2.0 * DEPTH) ** 0.25
LN_EPS = 1e-5
NEG_INF = -1e30

OFF_Q = 0
OFF_K = D_ATTN
OFF_V = D_ATTN + D_KV
OFF_GA = D_ATTN + 2 * D_KV
OFF_XL = OFF_GA + D_ATTN
OFF_GL = OFF_XL + D_LRU

SUBLANES = 8
LANES = 128
MXU_DIM = 256
VMEM_LIMIT_BYTES = 56 * 1024 * 1024

GROUP_W = GQA_GROUP * HEAD_DIM
KEY_PAD = 2 * LANES
V_EXT_W = 2 * LANES
GATE_TILE = MXU_DIM
N_GATE_TILES = D_LRU // GATE_TILE
ROW_TILE = 256
OUT_ROW_TILE = 1024
UNITS_PER_LATE_PIECE = 4
SCORE_LOOKAHEAD = 2
SCORE_RING = SCORE_LOOKAHEAD + 2
N_TILE = 512
N_CHUNKS_PER_TILE = ROW_TILE // CHUNK
KBUF_ROWS = WINDOW + ROW_TILE + (KEY_PAD - (WINDOW + CHUNK))

_SLOPES = tuple(2.0 ** (-8.0 * h / N_Q_HEADS) for h in range(1, N_Q_HEADS + 1))


def _resident(shape):
    return pl.BlockSpec(shape, lambda *_: (0,) * len(shape), pipeline_mode=pl.Buffered(1))


class _ColGroups:
    def __init__(self, groups):
        self._groups = groups

    def _locate(self, idx):
        rows, cols = idx
        for first, end, ref in self._groups:
            if first <= cols.start and cols.stop <= end:
                return ref, rows, slice(cols.start - first, cols.stop - first)
        raise ValueError(f"columns {cols} straddle projection groups")

    def __getitem__(self, idx):
        ref, rows, cols = self._locate(idx)
        return ref[rows, cols]

    def __setitem__(self, idx, value):
        ref, rows, cols = self._locate(idx)
        ref[rows, cols] = value


_ALL_ROWS = slice(None)
_PROJ_GROUPS = ((OFF_Q, OFF_K), (OFF_K, OFF_GA), (OFF_GA, OFF_XL), (OFF_XL, OFF_GL), (OFF_GL, D_IN))


def _sigmoid(x):
    return 1.0 / (1.0 + jnp.exp(-x))


def _in_proj_into(xb, w_ref, b_ref, z, first=0, end=D_IN, width=N_TILE):
    for n in range(first, end, width):
        z[_ALL_ROWS, n:n + width] = (
            jnp.dot(xb, w_ref[:, n:n + width], preferred_element_type=F32) + b_ref[:, n:n + width])


def _bias_table(nq, kv, key0_pos, q0_pos, n_keys):
    rows = GQA_GROUP * nq
    r = lax.broadcasted_iota(jnp.int32, (rows, KEY_PAD), 0)
    c = lax.broadcasted_iota(jnp.int32, (rows, KEY_PAD), 1)
    shift = nq.bit_length() - 1
    g = jnp.right_shift(r, shift)
    qpos = q0_pos + (r & (nq - 1))
    kpos = key0_pos + c
    slope = jnp.full((rows, KEY_PAD), _SLOPES[kv * GQA_GROUP], F32)
    for gg in range(1, GQA_GROUP):
        slope = jnp.where(g == gg, _SLOPES[kv * GQA_GROUP + gg], slope)
    dist = jnp.abs(qpos - kpos).astype(F32)
    cshift = CHUNK.bit_length() - 1
    qc = jnp.right_shift(qpos, cshift)
    kc = jnp.right_shift(kpos, cshift)
    valid = (kpos >= 0) & (kc <= qc) & (kc >= qc - WINDOW_CHUNKS) & (c < n_keys)
    return jnp.where(valid, -(slope * dist), NEG_INF)


def _tile_heads(x, kv, copies):
    pair = x[:, (kv // 2) * LANES:(kv // 2 + 1) * LANES]
    rolled = pltpu.roll(pair, HEAD_DIM, axis=1)
    lane = lax.broadcasted_iota(jnp.int32, pair.shape, 1)
    low = lane < HEAD_DIM
    both = jnp.where(low, pair, rolled) if kv % 2 == 0 else jnp.where(low, rolled, pair)
    both = both.astype(BF16)
    return jnp.concatenate([both] * (copies * HEAD_DIM // LANES), axis=1)


def _sink_table(nq, kv, sinks_ref):
    return jnp.concatenate(
        [jnp.full((nq, LANES), sinks_ref[kv * GQA_GROUP + g], F32) for g in range(GQA_GROUP)], axis=0)


def _head_of_lane(nq):
    lane = lax.broadcasted_iota(jnp.int32, (nq, GROUP_W), 1)
    return jnp.right_shift(lane, HEAD_DIM.bit_length() - 1)


def _attn_scores(q, k4):
    nq = q.shape[0]
    qb = (q * (HEAD_DIM ** -0.5)).astype(BF16)
    head_of_lane = _head_of_lane(nq)
    zero = jnp.zeros_like(qb)
    qs = jnp.concatenate([jnp.where(head_of_lane == g, qb, zero) for g in range(GQA_GROUP)], axis=0)
    return lax.dot_general(qs, k4, (((1,), (1,)), ((), ())), preferred_element_type=F32)


def _attn_finish(scores, v4e, bias, sink):
    nq = scores.shape[0] // GQA_GROUP
    s = scores + bias
    m = jnp.maximum(jnp.broadcast_to(jnp.max(s, axis=-1, keepdims=True), sink.shape), sink)
    e = jnp.concatenate(
        [jnp.exp(s[:, j * LANES:(j + 1) * LANES] - m) for j in range(KEY_PAD // LANES)], axis=1)
    o4e = jnp.dot(e.astype(BF16), v4e, preferred_element_type=F32)
    o2 = o4e[:, :LANES] * (1.0 / (o4e[:, LANES:] + jnp.exp(sink - m)))
    low = lax.broadcasted_iota(jnp.int32, (nq, LANES), 1) < HEAD_DIM
    return jnp.concatenate(
        [jnp.where(low, o2[g * nq:(g + 1) * nq], o2[(g + 1) * nq:(g + 2) * nq]) for g in range(0, GQA_GROUP, 2)],
        axis=1)


def _lru(z, u_out, t_rows, xbuf, a_s, b_s, hc, cw_ref, cb_ref, wg_ref, bg_ref, lam_ref, fill=None):
    fill = fill or (lambda: None)
    neg = -lam_ref[...]
    softplus = jnp.maximum(neg, 0.0) + jnp.log1p(jnp.exp(-jnp.abs(neg)))
    coef = -LRU_C * softplus
    tiles = [slice(j * GATE_TILE, (j + 1) * GATE_TILE) for j in range(N_GATE_TILES)]

    def conv_and_gate_matmul(cols, j):
        xl = z[_ALL_ROWS, OFF_XL + cols.start:OFF_XL + cols.stop]
        xbuf[SUBLANES:SUBLANES + t_rows, cols] = xl
        xc = cb_ref[:, cols] + cw_ref[CONV_WIDTH - 1:CONV_WIDTH, cols] * xl
        for k in range(CONV_WIDTH - 1):
            off = SUBLANES - (CONV_WIDTH - 1) + k
            xc = xc + cw_ref[k:k + 1, cols] * xbuf[off:off + t_rows, cols]
        pre = jnp.dot(xc.astype(BF16), wg_ref[j], preferred_element_type=F32)
        a_s[:, cols] = pre[:, :GATE_TILE]
        b_s[:, cols] = pre[:, GATE_TILE:]
        return xc

    def gate_math(cols, xc):
        r = _sigmoid(a_s[:, cols] + bg_ref[0:1, cols])
        i = _sigmoid(b_s[:, cols] + bg_ref[1:2, cols])
        a = jnp.exp(coef[:, cols] * r)
        mult = jnp.sqrt(jnp.maximum(1.0 - a * a, 0.0))
        a_s[:, cols] = a
        b_s[:, cols] = mult * (i * xc)

    fill()
    xc_prev = None
    for j, cols in enumerate(tiles):
        xc = conv_and_gate_matmul(cols, j)
        if xc_prev is not None:
            gate_math(tiles[j - 1], xc_prev)
        xc_prev = xc
        fill()
    gate_math(tiles[-1], xc_prev)

    row = lax.broadcasted_iota(jnp.int32, (SUBLANES, D_LRU), 0)
    h_prev = hc[...]
    for gi in range(t_rows // SUBLANES):
        rows = slice(gi * SUBLANES, (gi + 1) * SUBLANES)
        a8 = a_s[rows, :]
        b8 = b_s[rows, :]
        for d in (1, 2, 4):
            keep = row >= d
            a_sh = pltpu.roll(a8, d, axis=0)
            b_sh = pltpu.roll(b8, d, axis=0)
            b8 = jnp.where(keep, a8 * b_sh + b8, b8)
            a8 = jnp.where(keep, a8 * a_sh, a8)
        h8 = a8 * h_prev + b8
        b_s[rows, :] = h8
        h_prev = jnp.broadcast_to(h8[SUBLANES - 1:SUBLANES, :], (SUBLANES, D_LRU))
    hc[...] = h_prev
    gl = z[_ALL_ROWS, OFF_GL:OFF_GL + D_LRU]
    u_out[...] = (b_s[...] * (gl * _sigmoid(gl))).astype(u_out.dtype)
    xbuf[0:SUBLANES, :] = xbuf[t_rows:t_rows + SUBLANES, :]


def _prompt_kernel(sinks_ref, x_ref, win_ref, bin_ref, cw_ref, cb_ref, wg_ref, bg_ref, lam_ref,
                   ua_ref, ul_ref, kwin_ref, vwin_ref, conv_ref, hout_ref,
                   zq, zkv, zga, zxl, zgl, bias_s, mask_s, sink_s, s_ring, k4_s, v4_s, xbuf, a_s, b_s, hc,
                   *, n_t, n_tiles):
    z = _ColGroups([(first, end, ref) for (first, end), ref in zip(_PROJ_GROUPS, (zq, zkv, zga, zxl, zgl))])
    ua = ua_ref.at[0]
    s = pl.program_id(0)
    t_a = lax.rem(jnp.minimum(s, n_tiles - 1), n_t)
    t_l = lax.rem(jnp.maximum(s - 1, 0), n_t)

    @pl.when(s == 0)
    def _init():
        for kv in range(N_KV_HEADS):
            bias_s[kv] = _bias_table(CHUNK, kv, 0, WINDOW, WINDOW + CHUNK)
            sink_s[kv] = _sink_table(CHUNK, kv, sinks_ref)
        col = lax.broadcasted_iota(jnp.int32, (SUBLANES, KEY_PAD), 1)
        for ci in range(WINDOW_CHUNKS):
            mask_s[ci] = jnp.where(col < WINDOW - ci * CHUNK, NEG_INF, 0.0)
        k4_s[...] = jnp.zeros_like(k4_s)
        v4_s[:, :, 0:LANES] = jnp.zeros((N_KV_HEADS, KBUF_ROWS, LANES), BF16)
        v4_s[:, :, LANES:] = jnp.ones((N_KV_HEADS, KBUF_ROWS, LANES), BF16)
        zxl[...] = jnp.zeros_like(zxl)
        zgl[...] = jnp.zeros_like(zgl)

    @pl.when(t_a == 0)
    def _reset_attention_carry():
        k4_s[:, 0:WINDOW, :] = jnp.zeros((N_KV_HEADS, WINDOW, GROUP_W), BF16)
        v4_s[:, 0:WINDOW, 0:LANES] = jnp.zeros((N_KV_HEADS, WINDOW, LANES), BF16)

    @pl.when(t_l == 0)
    def _reset_lru_carry():
        xbuf[0:SUBLANES, :] = jnp.zeros((SUBLANES, D_LRU), F32)
        hc[...] = jnp.zeros_like(hc)

    xb = x_ref[0].astype(BF16)
    early_cols = list(range(OFF_K, OFF_GA, N_TILE)) + list(range(OFF_Q, OFF_K, N_TILE)) + list(
        range(OFF_GA, OFF_XL, N_TILE))
    late_cols = list(range(OFF_XL, D_IN, N_TILE))

    def project(cols):
        if cols:
            c0 = cols.pop(0)
            _in_proj_into(xb, win_ref, bin_ref, z, c0, c0 + N_TILE)

    _lru(z, ul_ref.at[0], ROW_TILE, xbuf, a_s, b_s, hc, cw_ref, cb_ref, wg_ref, bg_ref, lam_ref,
         fill=lambda: project(early_cols))
    while early_cols:
        project(early_cols)

    k_new = z[_ALL_ROWS, OFF_K:OFF_K + D_KV]
    v_new = z[_ALL_ROWS, OFF_V:OFF_V + D_KV]
    for kv in range(N_KV_HEADS):
        k4_s[kv, WINDOW:WINDOW + ROW_TILE, :] = _tile_heads(k_new, kv, GQA_GROUP)
        v4_s[kv, WINDOW:WINDOW + ROW_TILE, 0:LANES] = _tile_heads(v_new, kv, 2)

    units = [(ci, kv) for ci in range(N_CHUNKS_PER_TILE) for kv in range(N_KV_HEADS)]

    def scores_into_ring(idx):
        ci, kv = units[idx]
        s_ring[idx % SCORE_RING] = _attn_scores(
            z[ci * CHUNK:(ci + 1) * CHUNK, OFF_Q + kv * GROUP_W:OFF_Q + (kv + 1) * GROUP_W],
            k4_s[kv, ci * CHUNK:ci * CHUNK + KEY_PAD, :])

    mask_rows = [jnp.where(t_a == 0, mask_s[ci, 0:1, :], 0.0) for ci in range(WINDOW_CHUNKS)]
    for idx in range(SCORE_LOOKAHEAD):
        scores_into_ring(idx)
    for idx, (ci, kv) in enumerate(units):
        if idx + SCORE_LOOKAHEAD < len(units):
            scores_into_ring(idx + SCORE_LOOKAHEAD)
        if idx % UNITS_PER_LATE_PIECE == 1:
            project(late_cols)
        s_cur = s_ring[idx % SCORE_RING]
        rows = slice(ci * CHUNK, (ci + 1) * CHUNK)
        bias = bias_s[kv] + mask_rows[ci] if ci < WINDOW_CHUNKS else bias_s[kv]
        o = _attn_finish(s_cur, v4_s[kv, ci * CHUNK:ci * CHUNK + KEY_PAD, :], bias, sink_s[kv])
        ga = z[rows, OFF_GA + kv * GROUP_W:OFF_GA + (kv + 1) * GROUP_W]
        ua[rows, kv * GROUP_W:(kv + 1) * GROUP_W] = (o * (ga * _sigmoid(ga))).astype(ua.dtype)
    assert not early_cols and not late_cols

    @pl.when(t_l == n_t - 1)
    def _emit_lru_state():
        seq_l = lax.div(jnp.maximum(s - 1, 0), n_t)
        for j in range(CONV_WIDTH - 1):
            row = SUBLANES - (CONV_WIDTH - 1) + j
            conv_ref[j, pl.ds(seq_l, 1), :] = xbuf[row:row + 1, :]
        hout_ref[pl.ds(seq_l, 1), :] = hc[0:1, :]

    @pl.when(t_a == n_t - 1)
    def _emit_kv_window():
        kwin_ref[0] = z[slice(ROW_TILE - WINDOW, ROW_TILE), OFF_K:OFF_K + D_KV].T
        vwin_ref[0] = z[slice(ROW_TILE - WINDOW, ROW_TILE), OFF_V:OFF_V + D_KV].T

    @pl.when(s < n_tiles - 1)
    def _carry_kv():
        k4_s[:, 0:WINDOW, :] = k4_s[:, ROW_TILE:ROW_TILE + WINDOW, :]
        v4_s[:, 0:WINDOW, 0:LANES] = v4_s[:, ROW_TILE:ROW_TILE + WINDOW, 0:LANES]


def _prompt_mixer(x3, sinks, w_in_bf, b_in_row, cw, cb, wg, bg, lam):
    bsz, seq, _ = x3.shape
    n_t = seq // ROW_TILE
    n_tiles = bsz * n_t
    smem = pl.BlockSpec(memory_space=pltpu.SMEM)

    def attn_tile(s):
        ta = jnp.minimum(s, n_tiles - 1)
        return ta // n_t, ta % n_t

    def lru_tile(s):
        tl = jnp.maximum(s - 1, 0)
        return tl // n_t, tl % n_t

    return pl.pallas_call(
        functools.partial(_prompt_kernel, n_t=n_t, n_tiles=n_tiles),
        grid=(n_tiles + 1,),
        in_specs=[smem,
                  pl.BlockSpec((1, ROW_TILE, D_MODEL), lambda s: (*attn_tile(s), 0)),
                  _resident((D_MODEL, D_IN)), _resident((1, D_IN)),
                  _resident((CONV_WIDTH, D_LRU)), _resident((1, D_LRU)),
                  _resident((N_GATE_TILES, GATE_TILE, 2 * GATE_TILE)),
                  _resident((2, D_LRU)), _resident((1, D_LRU))],
        out_specs=[pl.BlockSpec((1, ROW_TILE, D_ATTN), lambda s: (*attn_tile(s), 0)),
                   pl.BlockSpec((1, ROW_TILE, D_LRU), lambda s: (*lru_tile(s), 0)),
                   pl.BlockSpec((1, D_KV, WINDOW), lambda s: (attn_tile(s)[0], 0, 0)),
                   pl.BlockSpec((1, D_KV, WINDOW), lambda s: (attn_tile(s)[0], 0, 0)),
                   pl.BlockSpec((CONV_WIDTH - 1, bsz, D_LRU), lambda s: (0, 0, 0)),
                   pl.BlockSpec((bsz, D_LRU), lambda s: (0, 0))],
        out_shape=[jax.ShapeDtypeStruct((bsz, seq, D_ATTN), BF16),
                   jax.ShapeDtypeStruct((bsz, seq, D_LRU), BF16),
                   jax.ShapeDtypeStruct((bsz, D_KV, WINDOW), F32),
                   jax.ShapeDtypeStruct((bsz, D_KV, WINDOW), F32),
                   jax.ShapeDtypeStruct((CONV_WIDTH - 1, bsz, D_LRU), F32),
                   jax.ShapeDtypeStruct((bsz, D_LRU), F32)],
        scratch_shapes=[pltpu.VMEM((ROW_TILE, end - first), F32) for first, end in _PROJ_GROUPS] + [
                        pltpu.VMEM((N_KV_HEADS, GQA_GROUP * CHUNK, KEY_PAD), F32),
                        pltpu.VMEM((WINDOW_CHUNKS, SUBLANES, KEY_PAD), F32),
                        pltpu.VMEM((N_KV_HEADS, GQA_GROUP * CHUNK, LANES), F32),
                        pltpu.VMEM((SCORE_RING, GQA_GROUP * CHUNK, KEY_PAD), F32),
                        pltpu.VMEM((N_KV_HEADS, KBUF_ROWS, GROUP_W), BF16),
                        pltpu.VMEM((N_KV_HEADS, KBUF_ROWS, V_EXT_W), BF16),
                        pltpu.VMEM((SUBLANES + ROW_TILE, D_LRU), F32),
                        pltpu.VMEM((ROW_TILE, D_LRU), F32),
                        pltpu.VMEM((ROW_TILE, D_LRU), F32),
                        pltpu.VMEM((SUBLANES, D_LRU), F32)],
        compiler_params=pltpu.CompilerParams(
            dimension_semantics=("arbitrary",), vmem_limit_bytes=VMEM_LIMIT_BYTES),
        name="prompt_mixer",
    )(sinks, x3, w_in_bf, b_in_row, cw, cb, wg, bg, lam)


def _out_proj_kernel(ua_ref, ul_ref, x_ref, w_ref, g_ref, b_ref, y_ref):
    tm = y_ref.shape[0]
    part = min(MXU_DIM, tm)
    n_parts = tm // part
    n_chunks = D_MODEL // N_TILE
    ln_rows = part // n_chunks

    def matmul_chunk(p, n):
        rows = slice(p * part, (p + 1) * part)
        cols = slice(n * N_TILE, (n + 1) * N_TILE)
        y_ref[rows, cols] = (
            DEEPNORM_ALPHA * x_ref[rows, cols]
            + jnp.dot(ua_ref[rows, :], w_ref[0:D_ATTN, cols], preferred_element_type=F32)
            + jnp.dot(ul_ref[rows, :], w_ref[D_ATTN:, cols], preferred_element_type=F32))

    def layer_norm_rows(p, r):
        rows = slice(p * part + r * ln_rows, p * part + (r + 1) * ln_rows)
        y = y_ref[rows, :]
        mu = jnp.mean(y, axis=-1, keepdims=True)
        yc = y - mu
        var = jnp.mean(yc * yc, axis=-1, keepdims=True)
        y_ref[rows, :] = yc * lax.rsqrt(var + LN_EPS) * g_ref[...] + b_ref[...]

    for n in range(n_chunks):
        matmul_chunk(0, n)
    for p in range(1, n_parts):
        for n in range(n_chunks):
            matmul_chunk(p, n)
            layer_norm_rows(p - 1, n)
    for r in range(n_chunks):
        layer_norm_rows(n_parts - 1, r)


def _out_proj(ua2d, ul2d, x2d, w_bf, g_row, b_row):
    m = x2d.shape[0]
    tm = min(OUT_ROW_TILE, m)
    return pl.pallas_call(
        _out_proj_kernel,
        grid=(m // tm,),
        in_specs=[pl.BlockSpec((tm, D_ATTN), lambda i: (i, 0)),
                  pl.BlockSpec((tm, D_LRU), lambda i: (i, 0)),
                  pl.BlockSpec((tm, D_MODEL), lambda i: (i, 0)),
                  _resident((D_MODEL, D_MODEL)),
                  _resident((1, D_MODEL)),
                  _resident((1, D_MODEL))],
        out_specs=pl.BlockSpec((tm, D_MODEL), lambda i: (i, 0)),
        out_shape=jax.ShapeDtypeStruct((m, D_MODEL), F32),
        compiler_params=pltpu.CompilerParams(
            dimension_semantics=("arbitrary",), vmem_limit_bytes=VMEM_LIMIT_BYTES),
        name="out_proj",
    )(ua2d, ul2d, x2d, w_bf, g_row, b_row)


def _in_proj_kernel(x_ref, w_ref, b_ref, z_ref):
    @pl.when(pl.program_id(0) == 0)
    def _start_from_bias():
        z_ref[...] = jnp.broadcast_to(b_ref[...], z_ref.shape)

    xb = x_ref[...].astype(BF16)
    for n in range(0, D_IN, N_TILE):
        z_ref[:, n:n + N_TILE] += jnp.dot(xb, w_ref[:, n:n + N_TILE], preferred_element_type=F32)


def _in_proj(x2d, w_bf, b_row):
    m = x2d.shape[0]
    return pl.pallas_call(
        _in_proj_kernel,
        grid=(D_MODEL // MXU_DIM,),
        in_specs=[pl.BlockSpec((m, MXU_DIM), lambda k: (0, k)),
                  pl.BlockSpec((MXU_DIM, D_IN), lambda k: (k, 0)),
                  pl.BlockSpec((1, D_IN), lambda k: (0, 0))],
        out_specs=pl.BlockSpec((m, D_IN), lambda k: (0, 0)),
        out_shape=jax.ShapeDtypeStruct((m, D_IN), F32),
        compiler_params=pltpu.CompilerParams(
            dimension_semantics=("arbitrary",), vmem_limit_bytes=VMEM_LIMIT_BYTES),
        name="in_proj",
    )(x2d, w_bf, b_row)


def _sample_mixer_kernel(sinks_ref, z_ref, ck_ref, cv_ref, sc_ref, sh_ref,
                         cw_ref, cb_ref, wg_ref, bg_ref, lam_ref,
                         ua_ref, ul_ref, kwin_ref, vwin_ref, conv_ref, hout_ref,
                         bias_s, sink_s, s_ring, k4_s, v4_s, xbuf, a_s, b_s, hc, *, t_rows, n_cache):
    b = pl.program_id(0)
    n_keys = n_cache + t_rows
    z = _ColGroups([(0, D_IN, z_ref.at[0])])
    ua = ua_ref.at[0]

    @pl.when(b == 0)
    def _init_tables():
        for kv in range(N_KV_HEADS):
            bias_s[kv] = _bias_table(t_rows, kv, PAST_LEN - n_cache, PAST_LEN, n_keys)
            sink_s[kv] = _sink_table(t_rows, kv, sinks_ref)
        k4_s[...] = jnp.zeros_like(k4_s)
        v4_s[:, :, 0:LANES] = jnp.zeros((N_KV_HEADS, KEY_PAD, LANES), BF16)
        v4_s[:, :, LANES:] = jnp.ones((N_KV_HEADS, KEY_PAD, LANES), BF16)

    k_all = jnp.concatenate([ck_ref[0], z[_ALL_ROWS, OFF_K:OFF_K + D_KV]], axis=0)
    v_all = jnp.concatenate([cv_ref[0], z[_ALL_ROWS, OFF_V:OFF_V + D_KV]], axis=0)
    for kv in range(N_KV_HEADS):
        k4_s[kv, 0:n_keys, :] = _tile_heads(k_all, kv, GQA_GROUP)
        v4_s[kv, 0:n_keys, 0:LANES] = _tile_heads(v_all, kv, 2)
    for kv in range(N_KV_HEADS):
        s_ring[kv] = _attn_scores(z[_ALL_ROWS, OFF_Q + kv * GROUP_W:OFF_Q + (kv + 1) * GROUP_W], k4_s[kv])
    for kv in range(N_KV_HEADS):
        o = _attn_finish(s_ring[kv], v4_s[kv], bias_s[kv], sink_s[kv])
        ga = z[_ALL_ROWS, OFF_GA + kv * GROUP_W:OFF_GA + (kv + 1) * GROUP_W]
        ua[:, kv * GROUP_W:(kv + 1) * GROUP_W] = (o * (ga * _sigmoid(ga))).astype(ua.dtype)
    kwin_ref[0] = k_all[n_keys - WINDOW:, :]
    vwin_ref[0] = v_all[n_keys - WINDOW:, :]

    xbuf[0:SUBLANES, :] = jnp.zeros((SUBLANES, D_LRU), F32)
    xbuf[SUBLANES - (CONV_WIDTH - 1):SUBLANES, :] = sc_ref[0]
    hc[...] = jnp.broadcast_to(sh_ref[0], (SUBLANES, D_LRU))
    _lru(z, ul_ref.at[0], t_rows, xbuf, a_s, b_s, hc, cw_ref, cb_ref, wg_ref, bg_ref, lam_ref)
    conv_ref[0] = xbuf[SUBLANES - (CONV_WIDTH - 1):SUBLANES, :]
    hout_ref[0] = hc[0:1, :]


def _sample_mixer(z3, cache_k, cache_v, state_conv, state_h, sinks, cw, cb, wg, bg, lam):
    bsz, t_rows, _ = z3.shape
    n_cache = cache_k.shape[1]
    assert n_cache + t_rows <= KEY_PAD and n_cache + t_rows >= WINDOW
    assert t_rows % SUBLANES == 0 and t_rows & (t_rows - 1) == 0 and t_rows >= CONV_WIDTH - 1
    smem = pl.BlockSpec(memory_space=pltpu.SMEM)
    per_b = lambda shape: pl.BlockSpec((1,) + shape, lambda b: (b, 0, 0))
    return pl.pallas_call(
        functools.partial(_sample_mixer_kernel, t_rows=t_rows, n_cache=n_cache),
        grid=(bsz,),
        in_specs=[smem, per_b((t_rows, D_IN)), per_b((n_cache, D_KV)), per_b((n_cache, D_KV)),
                  per_b((CONV_WIDTH - 1, D_LRU)), per_b((1, D_LRU)),
                  _resident((CONV_WIDTH, D_LRU)), _resident((1, D_LRU)),
                  _resident((N_GATE_TILES, GATE_TILE, 2 * GATE_TILE)),
                  _resident((2, D_LRU)), _resident((1, D_LRU))],
        out_specs=[per_b((t_rows, D_ATTN)), per_b((t_rows, D_LRU)), per_b((WINDOW, D_KV)), per_b((WINDOW, D_KV)),
                   per_b((CONV_WIDTH - 1, D_LRU)), per_b((1, D_LRU))],
        out_shape=[jax.ShapeDtypeStruct((bsz, t_rows, D_ATTN), BF16),
                   jax.ShapeDtypeStruct((bsz, t_rows, D_LRU), BF16),
                   jax.ShapeDtypeStruct((bsz, WINDOW, D_KV), F32),
                   jax.ShapeDtypeStruct((bsz, WINDOW, D_KV), F32),
                   jax.ShapeDtypeStruct((bsz, CONV_WIDTH - 1, D_LRU), F32),
                   jax.ShapeDtypeStruct((bsz, 1, D_LRU), F32)],
        scratch_shapes=[pltpu.VMEM((N_KV_HEADS, GQA_GROUP * t_rows, KEY_PAD), F32),
                        pltpu.VMEM((N_KV_HEADS, GQA_GROUP * t_rows, LANES), F32),
                        pltpu.VMEM((N_KV_HEADS, GQA_GROUP * t_rows, KEY_PAD), F32),
                        pltpu.VMEM((N_KV_HEADS, KEY_PAD, GROUP_W), BF16),
                        pltpu.VMEM((N_KV_HEADS, KEY_PAD, V_EXT_W), BF16),
                        pltpu.VMEM((SUBLANES + t_rows, D_LRU), F32),
                        pltpu.VMEM((t_rows, D_LRU), F32),
                        pltpu.VMEM((t_rows, D_LRU), F32),
                        pltpu.VMEM((SUBLANES, D_LRU), F32)],
        compiler_params=pltpu.CompilerParams(
            dimension_semantics=("arbitrary",), vmem_limit_bytes=VMEM_LIMIT_BYTES),
        name="sample_mixer",
    )(sinks, z3, cache_k, cache_v, state_conv, state_h, cw, cb, wg, bg, lam)


def _gate_weight_tiles(w_a, w_x):
    per = GATE_TILE // LRU_BLOCK
    w5 = jnp.stack([w_a, w_x]).reshape(2, N_GATE_TILES, per, LRU_BLOCK, LRU_BLOCK)
    eye = jnp.eye(per, dtype=w_a.dtype)
    return jnp.einsum('gtade,ab->tadgbe', w5, eye).reshape(N_GATE_TILES, GATE_TILE, 2 * GATE_TILE).astype(BF16)


def kernel(x_prompt, x_sample, cache_k, cache_v, state_conv, state_h, w_in, b_in, conv_w, conv_b,
           w_gate_a, b_gate_a, w_gate_x, b_gate_x, lru_lambda, attn_sinks, w_out, ln_g, ln_b):
    assert w_in.shape[0] == DEPTH == 1
    bsz, seq, _ = x_prompt.shape
    dbsz, dseq, _ = x_sample.shape
    n_cache = cache_k.shape[2]
    n_p, n_s = bsz * seq, dbsz * dseq
    assert seq % ROW_TILE == 0 and n_p % OUT_ROW_TILE == 0

    w_in_bf = w_in[0].astype(BF16)
    w_out_bf = w_out[0].astype(BF16)
    wg = _gate_weight_tiles(w_gate_a[0], w_gate_x[0])
    bg = jnp.stack([b_gate_a[0].reshape(D_LRU), b_gate_x[0].reshape(D_LRU)])
    cw = conv_w[0]
    sinks = attn_sinks[0]

    uap, ulp, kp, vp, cp, hp = _prompt_mixer(x_prompt, sinks, w_in_bf, b_in, cw, conv_b, wg, bg, lru_lambda)
    yp = _out_proj(uap.reshape(n_p, D_ATTN), ulp.reshape(n_p, D_LRU), x_prompt.reshape(n_p, D_MODEL),
                   w_out_bf, ln_g, ln_b).reshape(bsz, seq, D_MODEL)

    xs2 = x_sample.reshape(n_s, D_MODEL)
    zs = _in_proj(xs2, w_in_bf, b_in).reshape(dbsz, dseq, D_IN)
    uas, uls, ks, vs, cs, hs = _sample_mixer(
        zs, cache_k[0].reshape(dbsz, n_cache, D_KV), cache_v[0].reshape(dbsz, n_cache, D_KV),
        state_conv[0], state_h[0].reshape(dbsz, 1, D_LRU),
        sinks, cw, conv_b, wg, bg, lru_lambda)
    ys = _out_proj(uas.reshape(n_s, D_ATTN), uls.reshape(n_s, D_LRU), xs2,
                   w_out_bf, ln_g, ln_b).reshape(dbsz, dseq, D_MODEL)

    def from_channel_major(w):
        return w.reshape(bsz, N_KV_HEADS, HEAD_DIM, WINDOW).transpose(0, 3, 1, 2)[None]

    def from_row_major(w):
        return w.reshape(1, dbsz, WINDOW, N_KV_HEADS, HEAD_DIM)

    return (yp, ys,
            from_channel_major(kp), from_channel_major(vp), cp.transpose(1, 0, 2)[None], hp[None],
            from_row_major(ks), from_row_major(vs), cs[None], hs.reshape(1, dbsz, D_LRU))
```

```python
import functools

import jax
import jax.numpy as jnp
from jax import lax
from jax.experimental import pallas as pl
from jax.experimental.pallas import tpu as pltpu

F32 = jnp.float32
BF16 = jnp.bfloat16

D_MODEL = 2048
DEPTH = 1
CHUNK = 64
WINDOW = 128
WINDOW_CHUNKS = WINDOW // CHUNK
HEAD_DIM = 64
D_ATTN = D_MODEL // 2
D_LRU = D_MODEL - D_ATTN
N_Q_HEADS = D_ATTN // HEAD_DIM
N_KV_HEADS = 4
GQA_GROUP = N_Q_HEADS // N_KV_HEADS
D_KV = N_KV_HEADS * HEAD_DIM
N_LRU_BLOCKS = 16
LRU_BLOCK = D_LRU // N_LRU_BLOCKS
CONV_WIDTH = 4
LRU_C = 8.0
D_IN = 2 * D_ATTN + 2 * D_KV + 2 * D_LRU
PAST_LEN = 2048
DEEPNORM_ALPHA = (2.0 * DEPTH) ** 0.25
LN_EPS = 1e-5
NEG_INF = -1e30

OFF_Q = 0
OFF_K = D_ATTN
OFF_V = D_ATTN + D_KV
OFF_GA = D_ATTN + 2 * D_KV
OFF_XL = OFF_GA + D_ATTN
OFF_GL = OFF_XL + D_LRU

SUBLANES = 8
LANES = 128
MXU_DIM = 256
VMEM_LIMIT_BYTES = 56 * 1024 * 1024

GROUP_W = GQA_GROUP * HEAD_DIM
KEY_PAD = 2 * LANES
V_EXT_W = 2 * LANES
GATE_TILE = MXU_DIM
N_GATE_TILES = D_LRU // GATE_TILE
ROW_TILE = 256
OUT_ROW_TILE = 1024
LATE_PIECE_UNITS = (1, 5, 9, 13)
SCORE_LOOKAHEAD = 2
SCORE_RING = SCORE_LOOKAHEAD + 2
N_TILE = 512
WEIGHT_STAGE_ROWS = 128
N_CHUNKS_PER_TILE = ROW_TILE // CHUNK
KBUF_ROWS = WINDOW + ROW_TILE + (KEY_PAD - (WINDOW + CHUNK))

_SLOPES = tuple(2.0 ** (-8.0 * h / N_Q_HEADS) for h in range(1, N_Q_HEADS + 1))


def _resident(shape):
    return pl.BlockSpec(shape, lambda *_: (0,) * len(shape), pipeline_mode=pl.Buffered(1))


class _ColGroups:
    def __init__(self, groups):
        self._groups = groups

    def _locate(self, idx):
        rows, cols = idx
        for first, end, ref in self._groups:
            if first <= cols.start and cols.stop <= end:
                return ref, rows, slice(cols.start - first, cols.stop - first)
        raise ValueError(f"columns {cols} straddle projection groups")

    def __getitem__(self, idx):
        ref, rows, cols = self._locate(idx)
        return ref[rows, cols]

    def __setitem__(self, idx, value):
        ref, rows, cols = self._locate(idx)
        ref[rows, cols] = value


_ALL_ROWS = slice(None)
_PROJ_GROUPS = ((OFF_Q, OFF_K), (OFF_K, OFF_GA), (OFF_GA, OFF_XL), (OFF_XL, OFF_GL), (OFF_GL, D_IN))


def _sigmoid(x):
    return 1.0 / (1.0 + jnp.exp(-x))


def _in_proj_into(xb, w_ref, b_ref, z, first=0, end=D_IN, width=N_TILE):
    for n in range(first, end, width):
        z[_ALL_ROWS, n:n + width] = (
            jnp.dot(xb, w_ref[:, n:n + width], preferred_element_type=F32) + b_ref[:, n:n + width])


def _load_weight_as_bf16(w_hbm, w_bf, stage, sem):
    rows = stage.shape[1]
    n_chunks = w_hbm.shape[0] // rows

    def chunk_copy(k):
        return pltpu.make_async_copy(w_hbm.at[pl.ds(k * rows, rows), :], stage.at[k % 2], sem.at[k % 2])

    chunk_copy(0).start()
    for k in range(n_chunks):
        if k + 1 < n_chunks:
            chunk_copy(k + 1).start()
        chunk_copy(k).wait()
        w_bf[k * rows:(k + 1) * rows, :] = stage[k % 2].astype(BF16)


def _bias_table(nq, kv, key0_pos, q0_pos, n_keys):
    rows = GQA_GROUP * nq
    r = lax.broadcasted_iota(jnp.int32, (rows, KEY_PAD), 0)
    c = lax.broadcasted_iota(jnp.int32, (rows, KEY_PAD), 1)
    shift = nq.bit_length() - 1
    g = jnp.right_shift(r, shift)
    qpos = q0_pos + (r & (nq - 1))
    kpos = key0_pos + c
    slope = jnp.full((rows, KEY_PAD), _SLOPES[kv * GQA_GROUP], F32)
    for gg in range(1, GQA_GROUP):
        slope = jnp.where(g == gg, _SLOPES[kv * GQA_GROUP + gg], slope)
    dist = jnp.abs(qpos - kpos).astype(F32)
    cshift = CHUNK.bit_length() - 1
    qc = jnp.right_shift(qpos, cshift)
    kc = jnp.right_shift(kpos, cshift)
    valid = (kpos >= 0) & (kc <= qc) & (kc >= qc - WINDOW_CHUNKS) & (c < n_keys)
    return jnp.where(valid, -(slope * dist), NEG_INF)


def _tile_heads(x, kv, copies):
    pair = x[:, (kv // 2) * LANES:(kv // 2 + 1) * LANES]
    rolled = pltpu.roll(pair, HEAD_DIM, axis=1)
    lane = lax.broadcasted_iota(jnp.int32, pair.shape, 1)
    low = lane < HEAD_DIM
    both = jnp.where(low, pair, rolled) if kv % 2 == 0 else jnp.where(low, rolled, pair)
    both = both.astype(BF16)
    return jnp.concatenate([both] * (copies * HEAD_DIM // LANES), axis=1)


def _sink_table(nq, kv, sinks_ref):
    return jnp.concatenate(
        [jnp.full((nq, LANES), sinks_ref[kv * GQA_GROUP + g], F32) for g in range(GQA_GROUP)], axis=0)


def _head_of_lane(nq):
    lane = lax.broadcasted_iota(jnp.int32, (nq, GROUP_W), 1)
    return jnp.right_shift(lane, HEAD_DIM.bit_length() - 1)


def _attn_scores(q, k4):
    nq = q.shape[0]
    qb = (q * (HEAD_DIM ** -0.5)).astype(BF16)
    head_of_lane = _head_of_lane(nq)
    zero = jnp.zeros_like(qb)
    qs = jnp.concatenate([jnp.where(head_of_lane == g, qb, zero) for g in range(GQA_GROUP)], axis=0)
    return lax.dot_general(qs, k4, (((1,), (1,)), ((), ())), preferred_element_type=F32)


def _attn_finish(scores, v4e, bias, sink):
    nq = scores.shape[0] // GQA_GROUP
    s = scores + bias
    m = jnp.maximum(jnp.broadcast_to(jnp.max(s, axis=-1, keepdims=True), sink.shape), sink)
    e = jnp.concatenate(
        [jnp.exp(s[:, j * LANES:(j + 1) * LANES] - m) for j in range(KEY_PAD // LANES)], axis=1)
    o4e = jnp.dot(e.astype(BF16), v4e, preferred_element_type=F32)
    o2 = o4e[:, :LANES] * (1.0 / (o4e[:, LANES:] + jnp.exp(sink - m)))
    low = lax.broadcasted_iota(jnp.int32, (nq, LANES), 1) < HEAD_DIM
    return jnp.concatenate(
        [jnp.where(low, o2[g * nq:(g + 1) * nq], o2[(g + 1) * nq:(g + 2) * nq]) for g in range(0, GQA_GROUP, 2)],
        axis=1)


def _lru(z, u_out, t_rows, xbuf, a_s, b_s, hc, cw_ref, cb_ref, wg_ref, bg_ref, lam_ref, fill=None):
    fill = fill or (lambda: None)
    neg = -lam_ref[...]
    softplus = jnp.maximum(neg, 0.0) + jnp.log1p(jnp.exp(-jnp.abs(neg)))
    coef = -LRU_C * softplus
    tiles = [slice(j * GATE_TILE, (j + 1) * GATE_TILE) for j in range(N_GATE_TILES)]

    def conv_and_gate_matmul(cols, j):
        xl = z[_ALL_ROWS, OFF_XL + cols.start:OFF_XL + cols.stop]
        xbuf[SUBLANES:SUBLANES + t_rows, cols] = xl
        xc = cb_ref[:, cols] + cw_ref[CONV_WIDTH - 1:CONV_WIDTH, cols] * xl
        for k in range(CONV_WIDTH - 1):
            off = SUBLANES - (CONV_WIDTH - 1) + k
            xc = xc + cw_ref[k:k + 1, cols] * xbuf[off:off + t_rows, cols]
        pre = jnp.dot(xc.astype(BF16), wg_ref[j], preferred_element_type=F32)
        a_s[:, cols] = pre[:, :GATE_TILE]
        b_s[:, cols] = pre[:, GATE_TILE:]
        return xc

    def gate_math(cols, xc):
        r = _sigmoid(a_s[:, cols] + bg_ref[0:1, cols])
        i = _sigmoid(b_s[:, cols] + bg_ref[1:2, cols])
        a = jnp.exp(coef[:, cols] * r)
        mult = jnp.sqrt(jnp.maximum(1.0 - a * a, 0.0))
        a_s[:, cols] = a
        b_s[:, cols] = mult * (i * xc)

    fill()
    xc_prev = None
    for j, cols in enumerate(tiles):
        xc = conv_and_gate_matmul(cols, j)
        if xc_prev is not None:
            gate_math(tiles[j - 1], xc_prev)
        xc_prev = xc
        fill()
    gate_math(tiles[-1], xc_prev)

    row = lax.broadcasted_iota(jnp.int32, (SUBLANES, D_LRU), 0)
    h_prev = hc[...]
    for gi in range(t_rows // SUBLANES):
        rows = slice(gi * SUBLANES, (gi + 1) * SUBLANES)
        a8 = a_s[rows, :]
        b8 = b_s[rows, :]
        for d in (1, 2, 4):
            keep = row >= d
            a_sh = pltpu.roll(a8, d, axis=0)
            b_sh = pltpu.roll(b8, d, axis=0)
            b8 = jnp.where(keep, a8 * b_sh + b8, b8)
            a8 = jnp.where(keep, a8 * a_sh, a8)
        h8 = a8 * h_prev + b8
        b_s[rows, :] = h8
        h_prev = jnp.broadcast_to(h8[SUBLANES - 1:SUBLANES, :], (SUBLANES, D_LRU))
    hc[...] = h_prev
    gl = z[_ALL_ROWS, OFF_GL:OFF_GL + D_LRU]
    u_out[...] = (b_s[...] * (gl * _sigmoid(gl))).astype(u_out.dtype)
    xbuf[0:SUBLANES, :] = xbuf[t_rows:t_rows + SUBLANES, :]


def _prompt_kernel(sinks_ref, x_ref, win_ref, bin_ref, cw_ref, cb_ref, wg_ref, bg_ref, lam_ref,
                   ua_ref, ul_ref, kwin_ref, vwin_ref, conv_ref, hout_ref,
                   zq, zkv, zga, zxl, zgl, bias_s, mask_s, sink_s, s_ring, k4_s, v4_s, xbuf, a_s, b_s, hc,
                   w_bf, w_stage, w_sem, *, n_t, n_tiles):
    z = _ColGroups([(first, end, ref) for (first, end), ref in zip(_PROJ_GROUPS, (zq, zkv, zga, zxl, zgl))])
    ua = ua_ref.at[0]
    s = pl.program_id(0)
    t_a = lax.rem(jnp.minimum(s, n_tiles - 1), n_t)
    t_l = lax.rem(jnp.maximum(s - 1, 0), n_t)

    @pl.when(s == 0)
    def _init():
        _load_weight_as_bf16(win_ref, w_bf, w_stage, w_sem)
        for kv in range(N_KV_HEADS):
            bias_s[kv] = _bias_table(CHUNK, kv, 0, WINDOW, WINDOW + CHUNK)
            sink_s[kv] = _sink_table(CHUNK, kv, sinks_ref)
        col = lax.broadcasted_iota(jnp.int32, (SUBLANES, KEY_PAD), 1)
        for ci in range(WINDOW_CHUNKS):
            mask_s[ci] = jnp.where(col < WINDOW - ci * CHUNK, NEG_INF, 0.0)
        k4_s[...] = jnp.zeros_like(k4_s)
        v4_s[:, :, 0:LANES] = jnp.zeros((N_KV_HEADS, KBUF_ROWS, LANES), BF16)
        v4_s[:, :, LANES:] = jnp.ones((N_KV_HEADS, KBUF_ROWS, LANES), BF16)
        zxl[...] = jnp.zeros_like(zxl)
        zgl[...] = jnp.zeros_like(zgl)

    @pl.when(t_a == 0)
    def _reset_attention_carry():
        k4_s[:, 0:WINDOW, :] = jnp.zeros((N_KV_HEADS, WINDOW, GROUP_W), BF16)
        v4_s[:, 0:WINDOW, 0:LANES] = jnp.zeros((N_KV_HEADS, WINDOW, LANES), BF16)

    @pl.when(t_l == 0)
    def _reset_lru_carry():
        xbuf[0:SUBLANES, :] = jnp.zeros((SUBLANES, D_LRU), F32)
        hc[...] = jnp.zeros_like(hc)

    xb = x_ref[0].astype(BF16)
    early_cols = list(range(OFF_K, OFF_GA, N_TILE)) + list(range(OFF_Q, OFF_K, N_TILE)) + list(
        range(OFF_GA, OFF_XL, N_TILE))
    late_cols = list(range(OFF_XL, D_IN, N_TILE))

    def project(cols):
        if cols:
            c0 = cols.pop(0)
            _in_proj_into(xb, w_bf, bin_ref, z, c0, c0 + N_TILE)

    _lru(z, ul_ref.at[0], ROW_TILE, xbuf, a_s, b_s, hc, cw_ref, cb_ref, wg_ref, bg_ref, lam_ref,
         fill=lambda: project(early_cols))
    while early_cols:
        project(early_cols)

    k_new = z[_ALL_ROWS, OFF_K:OFF_K + D_KV]
    v_new = z[_ALL_ROWS, OFF_V:OFF_V + D_KV]
    for kv in range(N_KV_HEADS):
        k4_s[kv, WINDOW:WINDOW + ROW_TILE, :] = _tile_heads(k_new, kv, GQA_GROUP)
        v4_s[kv, WINDOW:WINDOW + ROW_TILE, 0:LANES] = _tile_heads(v_new, kv, 2)

    units = [(ci, kv) for ci in range(N_CHUNKS_PER_TILE) for kv in range(N_KV_HEADS)]

    def scores_into_ring(idx):
        ci, kv = units[idx]
        s_ring[idx % SCORE_RING] = _attn_scores(
            z[ci * CHUNK:(ci + 1) * CHUNK, OFF_Q + kv * GROUP_W:OFF_Q + (kv + 1) * GROUP_W],
            k4_s[kv, ci * CHUNK:ci * CHUNK + KEY_PAD, :])

    mask_rows = [jnp.where(t_a == 0, mask_s[ci, 0:1, :], 0.0) for ci in range(WINDOW_CHUNKS)]
    for idx in range(SCORE_LOOKAHEAD):
        scores_into_ring(idx)
    for idx, (ci, kv) in enumerate(units):
        if idx + SCORE_LOOKAHEAD < len(units):
            scores_into_ring(idx + SCORE_LOOKAHEAD)
        for _ in range(LATE_PIECE_UNITS.count(idx)):
            project(late_cols)
        s_cur = s_ring[idx % SCORE_RING]
        rows = slice(ci * CHUNK, (ci + 1) * CHUNK)
        bias = bias_s[kv] + mask_rows[ci] if ci < WINDOW_CHUNKS else bias_s[kv]
        o = _attn_finish(s_cur, v4_s[kv, ci * CHUNK:ci * CHUNK + KEY_PAD, :], bias, sink_s[kv])
        ga = z[rows, OFF_GA + kv * GROUP_W:OFF_GA + (kv + 1) * GROUP_W]
        ua[rows, kv * GROUP_W:(kv + 1) * GROUP_W] = (o * (ga * _sigmoid(ga))).astype(ua.dtype)
    assert not early_cols and not late_cols

    @pl.when(t_l == n_t - 1)
    def _emit_lru_state():
        seq_l = lax.div(jnp.maximum(s - 1, 0), n_t)
        for j in range(CONV_WIDTH - 1):
            row = SUBLANES - (CONV_WIDTH - 1) + j
            conv_ref[j, pl.ds(seq_l, 1), :] = xbuf[row:row + 1, :]
        hout_ref[pl.ds(seq_l, 1), :] = hc[0:1, :]

    @pl.when(t_a == n_t - 1)
    def _emit_kv_window():
        kwin_ref[0] = z[slice(ROW_TILE - WINDOW, ROW_TILE), OFF_K:OFF_K + D_KV].T
        vwin_ref[0] = z[slice(ROW_TILE - WINDOW, ROW_TILE), OFF_V:OFF_V + D_KV].T

    @pl.when(s < n_tiles - 1)
    def _carry_kv():
        k4_s[:, 0:WINDOW, :] = k4_s[:, ROW_TILE:ROW_TILE + WINDOW, :]
        v4_s[:, 0:WINDOW, 0:LANES] = v4_s[:, ROW_TILE:ROW_TILE + WINDOW, 0:LANES]


def _prompt_mixer(x3, sinks, w_in, b_in_row, cw, cb, wg, bg, lam):
    bsz, seq, _ = x3.shape
    n_t = seq // ROW_TILE
    n_tiles = bsz * n_t
    smem = pl.BlockSpec(memory_space=pltpu.SMEM)

    def attn_tile(s):
        ta = jnp.minimum(s, n_tiles - 1)
        return ta // n_t, ta % n_t

    def lru_tile(s):
        tl = jnp.maximum(s - 1, 0)
        return tl // n_t, tl % n_t

    return pl.pallas_call(
        functools.partial(_prompt_kernel, n_t=n_t, n_tiles=n_tiles),
        grid=(n_tiles + 1,),
        in_specs=[smem,
                  pl.BlockSpec((1, ROW_TILE, D_MODEL), lambda s: (*attn_tile(s), 0)),
                  pl.BlockSpec(memory_space=pl.ANY), _resident((1, D_IN)),
                  _resident((CONV_WIDTH, D_LRU)), _resident((1, D_LRU)),
                  _resident((N_GATE_TILES, GATE_TILE, 2 * GATE_TILE)),
                  _resident((2, D_LRU)), _resident((1, D_LRU))],
        out_specs=[pl.BlockSpec((1, ROW_TILE, D_ATTN), lambda s: (*attn_tile(s), 0)),
                   pl.BlockSpec((1, ROW_TILE, D_LRU), lambda s: (*lru_tile(s), 0)),
                   pl.BlockSpec((1, D_KV, WINDOW), lambda s: (attn_tile(s)[0], 0, 0)),
                   pl.BlockSpec((1, D_KV, WINDOW), lambda s: (attn_tile(s)[0], 0, 0)),
                   pl.BlockSpec((CONV_WIDTH - 1, bsz, D_LRU), lambda s: (0, 0, 0)),
                   pl.BlockSpec((bsz, D_LRU), lambda s: (0, 0))],
        out_shape=[jax.ShapeDtypeStruct((bsz, seq, D_ATTN), BF16),
                   jax.ShapeDtypeStruct((bsz, seq, D_LRU), BF16),
                   jax.ShapeDtypeStruct((bsz, D_KV, WINDOW), F32),
                   jax.ShapeDtypeStruct((bsz, D_KV, WINDOW), F32),
                   jax.ShapeDtypeStruct((CONV_WIDTH - 1, bsz, D_LRU), F32),
                   jax.ShapeDtypeStruct((bsz, D_LRU), F32)],
        scratch_shapes=[pltpu.VMEM((ROW_TILE, end - first), F32) for first, end in _PROJ_GROUPS] + [
                        pltpu.VMEM((N_KV_HEADS, GQA_GROUP * CHUNK, KEY_PAD), F32),
                        pltpu.VMEM((WINDOW_CHUNKS, SUBLANES, KEY_PAD), F32),
                        pltpu.VMEM((N_KV_HEADS, GQA_GROUP * CHUNK, LANES), F32),
                        pltpu.VMEM((SCORE_RING, GQA_GROUP * CHUNK, KEY_PAD), F32),
                        pltpu.VMEM((N_KV_HEADS, KBUF_ROWS, GROUP_W), BF16),
                        pltpu.VMEM((N_KV_HEADS, KBUF_ROWS, V_EXT_W), BF16),
                        pltpu.VMEM((SUBLANES + ROW_TILE, D_LRU), F32),
                        pltpu.VMEM((ROW_TILE, D_LRU), F32),
                        pltpu.VMEM((ROW_TILE, D_LRU), F32),
                        pltpu.VMEM((SUBLANES, D_LRU), F32),
                        pltpu.VMEM((D_MODEL, D_IN), BF16),
                        pltpu.VMEM((2, WEIGHT_STAGE_ROWS, D_IN), F32),
                        pltpu.SemaphoreType.DMA((2,))],
        compiler_params=pltpu.CompilerParams(
            dimension_semantics=("arbitrary",), vmem_limit_bytes=VMEM_LIMIT_BYTES),
        name="prompt_mixer",
    )(sinks, x3, w_in, b_in_row, cw, cb, wg, bg, lam)


def _out_proj_kernel(ua_ref, ul_ref, x_ref, w_hbm, g_ref, b_ref, y_ref, w_ref, w_stage, w_sem):
    @pl.when(pl.program_id(0) == 0)
    def _load_weight():
        _load_weight_as_bf16(w_hbm, w_ref, w_stage, w_sem)

    tm = y_ref.shape[0]
    part = min(MXU_DIM, tm)
    n_parts = tm // part
    n_chunks = D_MODEL // N_TILE
    ln_rows = part // n_chunks

    def matmul_chunk(p, n):
        rows = slice(p * part, (p + 1) * part)
        cols = slice(n * N_TILE, (n + 1) * N_TILE)
        y_ref[rows, cols] = (
            DEEPNORM_ALPHA * x_ref[rows, cols]
            + jnp.dot(ua_ref[rows, :], w_ref[0:D_ATTN, cols], preferred_element_type=F32)
            + jnp.dot(ul_ref[rows, :], w_ref[D_ATTN:, cols], preferred_element_type=F32))

    def layer_norm_rows(p, r):
        rows = slice(p * part + r * ln_rows, p * part + (r + 1) * ln_rows)
        y = y_ref[rows, :]
        mu = jnp.mean(y, axis=-1, keepdims=True)
        yc = y - mu
        var = jnp.mean(yc * yc, axis=-1, keepdims=True)
        y_ref[rows, :] = yc * lax.rsqrt(var + LN_EPS) * g_ref[...] + b_ref[...]

    for n in range(n_chunks):
        matmul_chunk(0, n)
    for p in range(1, n_parts):
        for n in range(n_chunks):
            matmul_chunk(p, n)
            layer_norm_rows(p - 1, n)
    for r in range(n_chunks):
        layer_norm_rows(n_parts - 1, r)


def _out_proj(ua2d, ul2d, x2d, w_out, g_row, b_row):
    m = x2d.shape[0]
    tm = min(OUT_ROW_TILE, m)
    return pl.pallas_call(
        _out_proj_kernel,
        grid=(m // tm,),
        in_specs=[pl.BlockSpec((tm, D_ATTN), lambda i: (i, 0)),
                  pl.BlockSpec((tm, D_LRU), lambda i: (i, 0)),
                  pl.BlockSpec((tm, D_MODEL), lambda i: (i, 0)),
                  pl.BlockSpec(memory_space=pl.ANY),
                  _resident((1, D_MODEL)),
                  _resident((1, D_MODEL))],
        out_specs=pl.BlockSpec((tm, D_MODEL), lambda i: (i, 0)),
        out_shape=jax.ShapeDtypeStruct((m, D_MODEL), F32),
        scratch_shapes=[pltpu.VMEM((D_MODEL, D_MODEL), BF16),
                        pltpu.VMEM((2, WEIGHT_STAGE_ROWS, D_MODEL), F32),
                        pltpu.SemaphoreType.DMA((2,))],
        compiler_params=pltpu.CompilerParams(
            dimension_semantics=("arbitrary",), vmem_limit_bytes=VMEM_LIMIT_BYTES),
        name="out_proj",
    )(ua2d, ul2d, x2d, w_out, g_row, b_row)


def _in_proj_kernel(x_ref, w_ref, b_ref, z_ref):
    @pl.when(pl.program_id(0) == 0)
    def _start_from_bias():
        z_ref[...] = jnp.broadcast_to(b_ref[...], z_ref.shape)

    xb = x_ref[...].astype(BF16)
    for n in range(0, D_IN, N_TILE):
        z_ref[:, n:n + N_TILE] += jnp.dot(xb, w_ref[:, n:n + N_TILE].astype(BF16), preferred_element_type=F32)


def _in_proj(x2d, w_in, b_row):
    m = x2d.shape[0]
    return pl.pallas_call(
        _in_proj_kernel,
        grid=(D_MODEL // MXU_DIM,),
        in_specs=[pl.BlockSpec((m, MXU_DIM), lambda k: (0, k)),
                  pl.BlockSpec((MXU_DIM, D_IN), lambda k: (k, 0)),
                  pl.BlockSpec((1, D_IN), lambda k: (0, 0))],
        out_specs=pl.BlockSpec((m, D_IN), lambda k: (0, 0)),
        out_shape=jax.ShapeDtypeStruct((m, D_IN), F32),
        compiler_params=pltpu.CompilerParams(
            dimension_semantics=("arbitrary",), vmem_limit_bytes=VMEM_LIMIT_BYTES),
        name="in_proj",
    )(x2d, w_in, b_row)


def _sample_mixer_kernel(sinks_ref, z_ref, ck_ref, cv_ref, sc_ref, sh_ref,
                         cw_ref, cb_ref, wg_ref, bg_ref, lam_ref,
                         ua_ref, ul_ref, kwin_ref, vwin_ref, conv_ref, hout_ref,
                         bias_s, sink_s, s_ring, k4_s, v4_s, xbuf, a_s, b_s, hc, *, t_rows, n_cache):
    b = pl.program_id(0)
    n_keys = n_cache + t_rows
    z = _ColGroups([(0, D_IN, z_ref.at[0])])
    ua = ua_ref.at[0]

    @pl.when(b == 0)
    def _init_tables():
        for kv in range(N_KV_HEADS):
            bias_s[kv] = _bias_table(t_rows, kv, PAST_LEN - n_cache, PAST_LEN, n_keys)
            sink_s[kv] = _sink_table(t_rows, kv, sinks_ref)
        k4_s[...] = jnp.zeros_like(k4_s)
        v4_s[:, :, 0:LANES] = jnp.zeros((N_KV_HEADS, KEY_PAD, LANES), BF16)
        v4_s[:, :, LANES:] = jnp.ones((N_KV_HEADS, KEY_PAD, LANES), BF16)

    k_all = jnp.concatenate([ck_ref[0], z[_ALL_ROWS, OFF_K:OFF_K + D_KV]], axis=0)
    v_all = jnp.concatenate([cv_ref[0], z[_ALL_ROWS, OFF_V:OFF_V + D_KV]], axis=0)
    for kv in range(N_KV_HEADS):
        k4_s[kv, 0:n_keys, :] = _tile_heads(k_all, kv, GQA_GROUP)
        v4_s[kv, 0:n_keys, 0:LANES] = _tile_heads(v_all, kv, 2)
    for kv in range(N_KV_HEADS):
        s_ring[kv] = _attn_scores(z[_ALL_ROWS, OFF_Q + kv * GROUP_W:OFF_Q + (kv + 1) * GROUP_W], k4_s[kv])
    for kv in range(N_KV_HEADS):
        o = _attn_finish(s_ring[kv], v4_s[kv], bias_s[kv], sink_s[kv])
        ga = z[_ALL_ROWS, OFF_GA + kv * GROUP_W:OFF_GA + (kv + 1) * GROUP_W]
        ua[:, kv * GROUP_W:(kv + 1) * GROUP_W] = (o * (ga * _sigmoid(ga))).astype(ua.dtype)
    kwin_ref[0] = k_all[n_keys - WINDOW:, :]
    vwin_ref[0] = v_all[n_keys - WINDOW:, :]

    xbuf[0:SUBLANES, :] = jnp.zeros((SUBLANES, D_LRU), F32)
    xbuf[SUBLANES - (CONV_WIDTH - 1):SUBLANES, :] = sc_ref[0]
    hc[...] = jnp.broadcast_to(sh_ref[0], (SUBLANES, D_LRU))
    _lru(z, ul_ref.at[0], t_rows, xbuf, a_s, b_s, hc, cw_ref, cb_ref, wg_ref, bg_ref, lam_ref)
    conv_ref[0] = xbuf[SUBLANES - (CONV_WIDTH - 1):SUBLANES, :]
    hout_ref[0] = hc[0:1, :]


def _sample_mixer(z3, cache_k, cache_v, state_conv, state_h, sinks, cw, cb, wg, bg, lam):
    bsz, t_rows, _ = z3.shape
    n_cache = cache_k.shape[1]
    assert n_cache + t_rows <= KEY_PAD and n_cache + t_rows >= WINDOW
    assert t_rows % SUBLANES == 0 and t_rows & (t_rows - 1) == 0 and t_rows >= CONV_WIDTH - 1
    smem = pl.BlockSpec(memory_space=pltpu.SMEM)
    per_b = lambda shape: pl.BlockSpec((1,) + shape, lambda b: (b, 0, 0))
    return pl.pallas_call(
        functools.partial(_sample_mixer_kernel, t_rows=t_rows, n_cache=n_cache),
        grid=(bsz,),
        in_specs=[smem, per_b((t_rows, D_IN)), per_b((n_cache, D_KV)), per_b((n_cache, D_KV)),
                  per_b((CONV_WIDTH - 1, D_LRU)), per_b((1, D_LRU)),
                  _resident((CONV_WIDTH, D_LRU)), _resident((1, D_LRU)),
                  _resident((N_GATE_TILES, GATE_TILE, 2 * GATE_TILE)),
                  _resident((2, D_LRU)), _resident((1, D_LRU))],
        out_specs=[per_b((t_rows, D_ATTN)), per_b((t_rows, D_LRU)), per_b((WINDOW, D_KV)), per_b((WINDOW, D_KV)),
                   per_b((CONV_WIDTH - 1, D_LRU)), per_b((1, D_LRU))],
        out_shape=[jax.ShapeDtypeStruct((bsz, t_rows, D_ATTN), BF16),
                   jax.ShapeDtypeStruct((bsz, t_rows, D_LRU), BF16),
                   jax.ShapeDtypeStruct((bsz, WINDOW, D_KV), F32),
                   jax.ShapeDtypeStruct((bsz, WINDOW, D_KV), F32),
                   jax.ShapeDtypeStruct((bsz, CONV_WIDTH - 1, D_LRU), F32),
                   jax.ShapeDtypeStruct((bsz, 1, D_LRU), F32)],
        scratch_shapes=[pltpu.VMEM((N_KV_HEADS, GQA_GROUP * t_rows, KEY_PAD), F32),
                        pltpu.VMEM((N_KV_HEADS, GQA_GROUP * t_rows, LANES), F32),
                        pltpu.VMEM((N_KV_HEADS, GQA_GROUP * t_rows, KEY_PAD), F32),
                        pltpu.VMEM((N_KV_HEADS, KEY_PAD, GROUP_W), BF16),
                        pltpu.VMEM((N_KV_HEADS, KEY_PAD, V_EXT_W), BF16),
                        pltpu.VMEM((SUBLANES + t_rows, D_LRU), F32),
                        pltpu.VMEM((t_rows, D_LRU), F32),
                        pltpu.VMEM((t_rows, D_LRU), F32),
                        pltpu.VMEM((SUBLANES, D_LRU), F32)],
        compiler_params=pltpu.CompilerParams(
            dimension_semantics=("arbitrary",), vmem_limit_bytes=VMEM_LIMIT_BYTES),
        name="sample_mixer",
    )(sinks, z3, cache_k, cache_v, state_conv, state_h, cw, cb, wg, bg, lam)


def _gate_weight_tiles(w_a, w_x):
    per = GATE_TILE // LRU_BLOCK
    w5 = jnp.stack([w_a, w_x]).reshape(2, N_GATE_TILES, per, LRU_BLOCK, LRU_BLOCK)
    eye = jnp.eye(per, dtype=w_a.dtype)
    return jnp.einsum('gtade,ab->tadgbe', w5, eye).reshape(N_GATE_TILES, GATE_TILE, 2 * GATE_TILE).astype(BF16)


def kernel(x_prompt, x_sample, cache_k, cache_v, state_conv, state_h, w_in, b_in, conv_w, conv_b,
           w_gate_a, b_gate_a, w_gate_x, b_gate_x, lru_lambda, attn_sinks, w_out, ln_g, ln_b):
    assert w_in.shape[0] == DEPTH == 1
    bsz, seq, _ = x_prompt.shape
    dbsz, dseq, _ = x_sample.shape
    n_cache = cache_k.shape[2]
    n_p, n_s = bsz * seq, dbsz * dseq
    assert seq % ROW_TILE == 0 and n_p % OUT_ROW_TILE == 0

    wg = _gate_weight_tiles(w_gate_a[0], w_gate_x[0])
    bg = jnp.stack([b_gate_a[0].reshape(D_LRU), b_gate_x[0].reshape(D_LRU)])
    cw = conv_w[0]
    sinks = attn_sinks[0]

    uap, ulp, kp, vp, cp, hp = _prompt_mixer(x_prompt, sinks, w_in[0], b_in, cw, conv_b, wg, bg, lru_lambda)
    yp = _out_proj(uap.reshape(n_p, D_ATTN), ulp.reshape(n_p, D_LRU), x_prompt.reshape(n_p, D_MODEL),
                   w_out[0], ln_g, ln_b).reshape(bsz, seq, D_MODEL)

    xs2 = x_sample.reshape(n_s, D_MODEL)
    zs = _in_proj(xs2, w_in[0], b_in).reshape(dbsz, dseq, D_IN)
    uas, uls, ks, vs, cs, hs = _sample_mixer(
        zs, cache_k[0].reshape(dbsz, n_cache, D_KV), cache_v[0].reshape(dbsz, n_cache, D_KV),
        state_conv[0], state_h[0].reshape(dbsz, 1, D_LRU),
        sinks, cw, conv_b, wg, bg, lru_lambda)
    ys = _out_proj(uas.reshape(n_s, D_ATTN), uls.reshape(n_s, D_LRU), xs2,
                   w_out[0], ln_g, ln_b).reshape(dbsz, dseq, D_MODEL)

    def from_channel_major(w):
        return w.reshape(bsz, N_KV_HEADS, HEAD_DIM, WINDOW).transpose(0, 3, 1, 2)[None]

    def from_row_major(w):
        return w.reshape(1, dbsz, WINDOW, N_KV_HEADS, HEAD_DIM)

    return (yp, ys,
            from_channel_major(kp), from_channel_major(vp), cp.transpose(1, 0, 2)[None], hp[None],
            from_row_major(ks), from_row_major(vs), cs[None], hs.reshape(1, dbsz, D_LRU))
```

```python
import functools

import jax
import jax.numpy as jnp
from jax import lax
from jax.experimental import pallas as pl
from jax.experimental.pallas import tpu as pltpu

F32 = jnp.float32
BF16 = jnp.bfloat16

D_MODEL = 2048
DEPTH = 1
CHUNK = 64
WINDOW = 128
WINDOW_CHUNKS = WINDOW // CHUNK
HEAD_DIM = 64
D_ATTN = D_MODEL // 2
D_LRU = D_MODEL - D_ATTN
N_Q_HEADS = D_ATTN // HEAD_DIM
N_KV_HEADS = 4
GQA_GROUP = N_Q_HEADS // N_KV_HEADS
D_KV = N_KV_HEADS * HEAD_DIM
N_LRU_BLOCKS = 16
LRU_BLOCK = D_LRU // N_LRU_BLOCKS
CONV_WIDTH = 4
LRU_C = 8.0
D_IN = 2 * D_ATTN + 2 * D_KV + 2 * D_LRU
PAST_LEN = 2048
DEEPNORM_ALPHA = (2.0 * DEPTH) ** 0.25
LN_EPS = 1e-5
NEG_INF = -1e30

OFF_Q = 0
OFF_K = D_ATTN
OFF_V = D_ATTN + D_KV
OFF_GA = D_ATTN + 2 * D_KV
OFF_XL = OFF_GA + D_ATTN
OFF_GL = OFF_XL + D_LRU

SUBLANES = 8
LANES = 128
MXU_DIM = 256
VMEM_LIMIT_BYTES = 56 * 1024 * 1024

GROUP_W = GQA_GROUP * HEAD_DIM
KEY_PAD = 2 * LANES
V_EXT_W = 2 * LANES
GATE_TILE = MXU_DIM
N_GATE_TILES = D_LRU // GATE_TILE
ROW_TILE = 256
OUT_ROW_TILE = 1024
UNITS_PER_LATE_PIECE = 4
SCORE_LOOKAHEAD = 2
SCORE_RING = SCORE_LOOKAHEAD + 2
N_TILE = 512
N_CHUNKS_PER_TILE = ROW_TILE // CHUNK
KBUF_ROWS = WINDOW + ROW_TILE + (KEY_PAD - (WINDOW + CHUNK))

_SLOPES = tuple(2.0 ** (-8.0 * h / N_Q_HEADS) for h in range(1, N_Q_HEADS + 1))


def _resident(shape):
    return pl.BlockSpec(shape, lambda *_: (0,) * len(shape), pipeline_mode=pl.Buffered(1))


class _ColGroups:
    def __init__(self, groups):
        self._groups = groups

    def _locate(self, idx):
        rows, cols = idx
        for first, end, ref in self._groups:
            if first <= cols.start and cols.stop <= end:
                return ref, rows, slice(cols.start - first, cols.stop - first)
        raise ValueError(f"columns {cols} straddle projection groups")

    def __getitem__(self, idx):
        ref, rows, cols = self._locate(idx)
        return ref[rows, cols]

    def __setitem__(self, idx, value):
        ref, rows, cols = self._locate(idx)
        ref[rows, cols] = value


_ALL_ROWS = slice(None)
_PROJ_GROUPS = ((OFF_Q, OFF_K), (OFF_K, OFF_GA), (OFF_GA, OFF_XL), (OFF_XL, OFF_GL), (OFF_GL, D_IN))


def _sigmoid(x):
    return 1.0 / (1.0 + jnp.exp(-x))


def _in_proj_into(xb, w_ref, b_ref, z, first=0, end=D_IN, width=N_TILE):
    for n in range(first, end, width):
        z[_ALL_ROWS, n:n + width] = (
            jnp.dot(xb, w_ref[:, n:n + width], preferred_element_type=F32) + b_ref[:, n:n + width])


def _bias_table(nq, kv, key0_pos, q0_pos, n_keys):
    rows = GQA_GROUP * nq
    r = lax.broadcasted_iota(jnp.int32, (rows, KEY_PAD), 0)
    c = lax.broadcasted_iota(jnp.int32, (rows, KEY_PAD), 1)
    shift = nq.bit_length() - 1
    g = jnp.right_shift(r, shift)
    qpos = q0_pos + (r & (nq - 1))
    kpos = key0_pos + c
    slope = jnp.full((rows, KEY_PAD), _SLOPES[kv * GQA_GROUP], F32)
    for gg in range(1, GQA_GROUP):
        slope = jnp.where(g == gg, _SLOPES[kv * GQA_GROUP + gg], slope)
    dist = jnp.abs(qpos - kpos).astype(F32)
    cshift = CHUNK.bit_length() - 1
    qc = jnp.right_shift(qpos, cshift)
    kc = jnp.right_shift(kpos, cshift)
    valid = (kpos >= 0) & (kc <= qc) & (kc >= qc - WINDOW_CHUNKS) & (c < n_keys)
    return jnp.where(valid, -(slope * dist), NEG_INF)


def _tile_heads(x, kv, copies):
    pair = x[:, (kv // 2) * LANES:(kv // 2 + 1) * LANES]
    rolled = pltpu.roll(pair, HEAD_DIM, axis=1)
    lane = lax.broadcasted_iota(jnp.int32, pair.shape, 1)
    low = lane < HEAD_DIM
    both = jnp.where(low, pair, rolled) if kv % 2 == 0 else jnp.where(low, rolled, pair)
    both = both.astype(BF16)
    return jnp.concatenate([both] * (copies * HEAD_DIM // LANES), axis=1)


def _sink_table(nq, kv, sinks_ref):
    return jnp.concatenate(
        [jnp.full((nq, LANES), sinks_ref[kv * GQA_GROUP + g], F32) for g in range(GQA_GROUP)], axis=0)


def _head_of_lane(nq):
    lane = lax.broadcasted_iota(jnp.int32, (nq, GROUP_W), 1)
    return jnp.right_shift(lane, HEAD_DIM.bit_length() - 1)


def _attn_scores(q, k4):
    nq = q.shape[0]
    qb = (q * (HEAD_DIM ** -0.5)).astype(BF16)
    head_of_lane = _head_of_lane(nq)
    zero = jnp.zeros_like(qb)
    qs = jnp.concatenate([jnp.where(head_of_lane == g, qb, zero) for g in range(GQA_GROUP)], axis=0)
    return lax.dot_general(qs, k4, (((1,), (1,)), ((), ())), preferred_element_type=F32)


def _attn_finish(scores, v4e, bias, sink):
    nq = scores.shape[0] // GQA_GROUP
    s = scores + bias
    m = jnp.maximum(jnp.broadcast_to(jnp.max(s, axis=-1, keepdims=True), sink.shape), sink)
    e = jnp.concatenate(
        [jnp.exp(s[:, j * LANES:(j + 1) * LANES] - m) for j in range(KEY_PAD // LANES)], axis=1)
    o4e = jnp.dot(e.astype(BF16), v4e, preferred_element_type=F32)
    o2 = o4e[:, :LANES] * (1.0 / (o4e[:, LANES:] + jnp.exp(sink - m)))
    low = lax.broadcasted_iota(jnp.int32, (nq, LANES), 1) < HEAD_DIM
    return jnp.concatenate(
        [jnp.where(low, o2[g * nq:(g + 1) * nq], o2[(g + 1) * nq:(g + 2) * nq]) for g in range(0, GQA_GROUP, 2)],
        axis=1)


def _lru(z, u_out, t_rows, xbuf, a_s, b_s, hc, cw_ref, cb_ref, wg_ref, bg_ref, lam_ref, fill=None):
    fill = fill or (lambda: None)
    neg = -lam_ref[...]
    softplus = jnp.maximum(neg, 0.0) + jnp.log1p(jnp.exp(-jnp.abs(neg)))
    coef = -LRU_C * softplus
    tiles = [slice(j * GATE_TILE, (j + 1) * GATE_TILE) for j in range(N_GATE_TILES)]

    def conv_and_gate_matmul(cols, j):
        xl = z[_ALL_ROWS, OFF_XL + cols.start:OFF_XL + cols.stop]
        xbuf[SUBLANES:SUBLANES + t_rows, cols] = xl
        xc = cb_ref[:, cols] + cw_ref[CONV_WIDTH - 1:CONV_WIDTH, cols] * xl
        for k in range(CONV_WIDTH - 1):
            off = SUBLANES - (CONV_WIDTH - 1) + k
            xc = xc + cw_ref[k:k + 1, cols] * xbuf[off:off + t_rows, cols]
        pre = jnp.dot(xc.astype(BF16), wg_ref[j], preferred_element_type=F32)
        a_s[:, cols] = pre[:, :GATE_TILE]
        b_s[:, cols] = pre[:, GATE_TILE:]
        return xc

    def gate_math(cols, xc):
        r = _sigmoid(a_s[:, cols] + bg_ref[0:1, cols])
        i = _sigmoid(b_s[:, cols] + bg_ref[1:2, cols])
        a = jnp.exp(coef[:, cols] * r)
        mult = jnp.sqrt(jnp.maximum(1.0 - a * a, 0.0))
        a_s[:, cols] = a
        b_s[:, cols] = mult * (i * xc)

    fill()
    xc_prev = None
    for j, cols in enumerate(tiles):
        xc = conv_and_gate_matmul(cols, j)
        if xc_prev is not None:
            gate_math(tiles[j - 1], xc_prev)
        xc_prev = xc
        fill()
    gate_math(tiles[-1], xc_prev)

    row = lax.broadcasted_iota(jnp.int32, (SUBLANES, D_LRU), 0)
    h_prev = hc[...]
    for gi in range(t_rows // SUBLANES):
        rows = slice(gi * SUBLANES, (gi + 1) * SUBLANES)
        a8 = a_s[rows, :]
        b8 = b_s[rows, :]
        for d in (1, 2, 4):
            keep = row >= d
            a_sh = pltpu.roll(a8, d, axis=0)
            b_sh = pltpu.roll(b8, d, axis=0)
            b8 = jnp.where(keep, a8 * b_sh + b8, b8)
            a8 = jnp.where(keep, a8 * a_sh, a8)
        h8 = a8 * h_prev + b8
        b_s[rows, :] = h8
        h_prev = jnp.broadcast_to(h8[SUBLANES - 1:SUBLANES, :], (SUBLANES, D_LRU))
    hc[...] = h_prev
    gl = z[_ALL_ROWS, OFF_GL:OFF_GL + D_LRU]
    u_out[...] = (b_s[...] * (gl * _sigmoid(gl))).astype(u_out.dtype)
    xbuf[0:SUBLANES, :] = xbuf[t_rows:t_rows + SUBLANES, :]


def _prompt_kernel(sinks_ref, x_ref, win_ref, bin_ref, cw_ref, cb_ref, wg_ref, bg_ref, lam_ref, wout_ref,
                   ua_ref, ul_ref, kwin_ref, vwin_ref, conv_ref, hout_ref, wout_bf_ref,
                   zq, zkv, zga, zxl, zgl, bias_s, mask_s, sink_s, s_ring, k4_s, v4_s, xbuf, a_s, b_s, hc,
                   *, n_t, n_tiles):
    wout_bf_ref[...] = wout_ref[...].astype(BF16)
    z = _ColGroups([(first, end, ref) for (first, end), ref in zip(_PROJ_GROUPS, (zq, zkv, zga, zxl, zgl))])
    ua = ua_ref.at[0]
    s = pl.program_id(0)
    t_a = lax.rem(jnp.minimum(s, n_tiles - 1), n_t)
    t_l = lax.rem(jnp.maximum(s - 1, 0), n_t)

    @pl.when(s == 0)
    def _init():
        for kv in range(N_KV_HEADS):
            bias_s[kv] = _bias_table(CHUNK, kv, 0, WINDOW, WINDOW + CHUNK)
            sink_s[kv] = _sink_table(CHUNK, kv, sinks_ref)
        col = lax.broadcasted_iota(jnp.int32, (SUBLANES, KEY_PAD), 1)
        for ci in range(WINDOW_CHUNKS):
            mask_s[ci] = jnp.where(col < WINDOW - ci * CHUNK, NEG_INF, 0.0)
        k4_s[...] = jnp.zeros_like(k4_s)
        v4_s[:, :, 0:LANES] = jnp.zeros((N_KV_HEADS, KBUF_ROWS, LANES), BF16)
        v4_s[:, :, LANES:] = jnp.ones((N_KV_HEADS, KBUF_ROWS, LANES), BF16)
        zxl[...] = jnp.zeros_like(zxl)
        zgl[...] = jnp.zeros_like(zgl)

    @pl.when(t_a == 0)
    def _reset_attention_carry():
        k4_s[:, 0:WINDOW, :] = jnp.zeros((N_KV_HEADS, WINDOW, GROUP_W), BF16)
        v4_s[:, 0:WINDOW, 0:LANES] = jnp.zeros((N_KV_HEADS, WINDOW, LANES), BF16)

    @pl.when(t_l == 0)
    def _reset_lru_carry():
        xbuf[0:SUBLANES, :] = jnp.zeros((SUBLANES, D_LRU), F32)
        hc[...] = jnp.zeros_like(hc)

    xb = x_ref[0].astype(BF16)
    early_cols = list(range(OFF_K, OFF_GA, N_TILE)) + list(range(OFF_Q, OFF_K, N_TILE)) + list(
        range(OFF_GA, OFF_XL, N_TILE))
    late_cols = list(range(OFF_XL, D_IN, N_TILE))

    def project(cols):
        if cols:
            c0 = cols.pop(0)
            _in_proj_into(xb, win_ref, bin_ref, z, c0, c0 + N_TILE)

    _lru(z, ul_ref.at[0], ROW_TILE, xbuf, a_s, b_s, hc, cw_ref, cb_ref, wg_ref, bg_ref, lam_ref,
         fill=lambda: project(early_cols))
    while early_cols:
        project(early_cols)

    k_new = z[_ALL_ROWS, OFF_K:OFF_K + D_KV]
    v_new = z[_ALL_ROWS, OFF_V:OFF_V + D_KV]
    for kv in range(N_KV_HEADS):
        k4_s[kv, WINDOW:WINDOW + ROW_TILE, :] = _tile_heads(k_new, kv, GQA_GROUP)
        v4_s[kv, WINDOW:WINDOW + ROW_TILE, 0:LANES] = _tile_heads(v_new, kv, 2)

    units = [(ci, kv) for ci in range(N_CHUNKS_PER_TILE) for kv in range(N_KV_HEADS)]

    def scores_into_ring(idx):
        ci, kv = units[idx]
        s_ring[idx % SCORE_RING] = _attn_scores(
            z[ci * CHUNK:(ci + 1) * CHUNK, OFF_Q + kv * GROUP_W:OFF_Q + (kv + 1) * GROUP_W],
            k4_s[kv, ci * CHUNK:ci * CHUNK + KEY_PAD, :])

    mask_rows = [jnp.where(t_a == 0, mask_s[ci, 0:1, :], 0.0) for ci in range(WINDOW_CHUNKS)]
    for idx in range(SCORE_LOOKAHEAD):
        scores_into_ring(idx)
    for idx, (ci, kv) in enumerate(units):
        if idx + SCORE_LOOKAHEAD < len(units):
            scores_into_ring(idx + SCORE_LOOKAHEAD)
        if idx % UNITS_PER_LATE_PIECE == 1:
            project(late_cols)
        s_cur = s_ring[idx % SCORE_RING]
        rows = slice(ci * CHUNK, (ci + 1) * CHUNK)
        bias = bias_s[kv] + mask_rows[ci] if ci < WINDOW_CHUNKS else bias_s[kv]
        o = _attn_finish(s_cur, v4_s[kv, ci * CHUNK:ci * CHUNK + KEY_PAD, :], bias, sink_s[kv])
        ga = z[rows, OFF_GA + kv * GROUP_W:OFF_GA + (kv + 1) * GROUP_W]
        ua[rows, kv * GROUP_W:(kv + 1) * GROUP_W] = (o * (ga * _sigmoid(ga))).astype(ua.dtype)
    assert not early_cols and not late_cols

    @pl.when(t_l == n_t - 1)
    def _emit_lru_state():
        seq_l = lax.div(jnp.maximum(s - 1, 0), n_t)
        for j in range(CONV_WIDTH - 1):
            row = SUBLANES - (CONV_WIDTH - 1) + j
            conv_ref[j, pl.ds(seq_l, 1), :] = xbuf[row:row + 1, :]
        hout_ref[pl.ds(seq_l, 1), :] = hc[0:1, :]

    @pl.when(t_a == n_t - 1)
    def _emit_kv_window():
        kwin_ref[0] = z[slice(ROW_TILE - WINDOW, ROW_TILE), OFF_K:OFF_K + D_KV].T
        vwin_ref[0] = z[slice(ROW_TILE - WINDOW, ROW_TILE), OFF_V:OFF_V + D_KV].T

    @pl.when(s < n_tiles - 1)
    def _carry_kv():
        k4_s[:, 0:WINDOW, :] = k4_s[:, ROW_TILE:ROW_TILE + WINDOW, :]
        v4_s[:, 0:WINDOW, 0:LANES] = v4_s[:, ROW_TILE:ROW_TILE + WINDOW, 0:LANES]


def _prompt_mixer(x3, sinks, w_in_bf, b_in_row, cw, cb, wg, bg, lam, w_out):
    bsz, seq, _ = x3.shape
    n_t = seq // ROW_TILE
    n_tiles = bsz * n_t
    smem = pl.BlockSpec(memory_space=pltpu.SMEM)
    assert D_MODEL % n_tiles == 0 and (D_MODEL // n_tiles) % (2 * SUBLANES) == 0
    w_out_slab = pl.BlockSpec((D_MODEL // n_tiles, D_MODEL), lambda s: (jnp.minimum(s, n_tiles - 1), 0))

    def attn_tile(s):
        ta = jnp.minimum(s, n_tiles - 1)
        return ta // n_t, ta % n_t

    def lru_tile(s):
        tl = jnp.maximum(s - 1, 0)
        return tl // n_t, tl % n_t

    return pl.pallas_call(
        functools.partial(_prompt_kernel, n_t=n_t, n_tiles=n_tiles),
        grid=(n_tiles + 1,),
        in_specs=[smem,
                  pl.BlockSpec((1, ROW_TILE, D_MODEL), lambda s: (*attn_tile(s), 0)),
                  _resident((D_MODEL, D_IN)), _resident((1, D_IN)),
                  _resident((CONV_WIDTH, D_LRU)), _resident((1, D_LRU)),
                  _resident((N_GATE_TILES, GATE_TILE, 2 * GATE_TILE)),
                  _resident((2, D_LRU)), _resident((1, D_LRU)), w_out_slab],
        out_specs=[pl.BlockSpec((1, ROW_TILE, D_ATTN), lambda s: (*attn_tile(s), 0)),
                   pl.BlockSpec((1, ROW_TILE, D_LRU), lambda s: (*lru_tile(s), 0)),
                   pl.BlockSpec((1, D_KV, WINDOW), lambda s: (attn_tile(s)[0], 0, 0)),
                   pl.BlockSpec((1, D_KV, WINDOW), lambda s: (attn_tile(s)[0], 0, 0)),
                   pl.BlockSpec((CONV_WIDTH - 1, bsz, D_LRU), lambda s: (0, 0, 0)),
                   pl.BlockSpec((bsz, D_LRU), lambda s: (0, 0)), w_out_slab],
        out_shape=[jax.ShapeDtypeStruct((bsz, seq, D_ATTN), BF16),
                   jax.ShapeDtypeStruct((bsz, seq, D_LRU), BF16),
                   jax.ShapeDtypeStruct((bsz, D_KV, WINDOW), F32),
                   jax.ShapeDtypeStruct((bsz, D_KV, WINDOW), F32),
                   jax.ShapeDtypeStruct((CONV_WIDTH - 1, bsz, D_LRU), F32),
                   jax.ShapeDtypeStruct((bsz, D_LRU), F32),
                   jax.ShapeDtypeStruct((D_MODEL, D_MODEL), BF16)],
        scratch_shapes=[pltpu.VMEM((ROW_TILE, end - first), F32) for first, end in _PROJ_GROUPS] + [
                        pltpu.VMEM((N_KV_HEADS, GQA_GROUP * CHUNK, KEY_PAD), F32),
                        pltpu.VMEM((WINDOW_CHUNKS, SUBLANES, KEY_PAD), F32),
                        pltpu.VMEM((N_KV_HEADS, GQA_GROUP * CHUNK, LANES), F32),
                        pltpu.VMEM((SCORE_RING, GQA_GROUP * CHUNK, KEY_PAD), F32),
                        pltpu.VMEM((N_KV_HEADS, KBUF_ROWS, GROUP_W), BF16),
                        pltpu.VMEM((N_KV_HEADS, KBUF_ROWS, V_EXT_W), BF16),
                        pltpu.VMEM((SUBLANES + ROW_TILE, D_LRU), F32),
                        pltpu.VMEM((ROW_TILE, D_LRU), F32),
                        pltpu.VMEM((ROW_TILE, D_LRU), F32),
                        pltpu.VMEM((SUBLANES, D_LRU), F32)],
        compiler_params=pltpu.CompilerParams(
            dimension_semantics=("arbitrary",), vmem_limit_bytes=VMEM_LIMIT_BYTES),
        name="prompt_mixer",
    )(sinks, x3, w_in_bf, b_in_row, cw, cb, wg, bg, lam, w_out)


def _out_proj_kernel(ua_ref, ul_ref, x_ref, w_ref, g_ref, b_ref, y_ref):
    tm = y_ref.shape[0]
    part = min(MXU_DIM, tm)
    n_parts = tm // part
    n_chunks = D_MODEL // N_TILE
    ln_rows = part // n_chunks

    def matmul_chunk(p, n):
        rows = slice(p * part, (p + 1) * part)
        cols = slice(n * N_TILE, (n + 1) * N_TILE)
        y_ref[rows, cols] = (
            DEEPNORM_ALPHA * x_ref[rows, cols]
            + jnp.dot(ua_ref[rows, :], w_ref[0:D_ATTN, cols], preferred_element_type=F32)
            + jnp.dot(ul_ref[rows, :], w_ref[D_ATTN:, cols], preferred_element_type=F32))

    def layer_norm_rows(p, r):
        rows = slice(p * part + r * ln_rows, p * part + (r + 1) * ln_rows)
        y = y_ref[rows, :]
        mu = jnp.mean(y, axis=-1, keepdims=True)
        yc = y - mu
        var = jnp.mean(yc * yc, axis=-1, keepdims=True)
        y_ref[rows, :] = yc * lax.rsqrt(var + LN_EPS) * g_ref[...] + b_ref[...]

    for n in range(n_chunks):
        matmul_chunk(0, n)
    for p in range(1, n_parts):
        for n in range(n_chunks):
            matmul_chunk(p, n)
            layer_norm_rows(p - 1, n)
    for r in range(n_chunks):
        layer_norm_rows(n_parts - 1, r)


def _out_proj(ua2d, ul2d, x2d, w_bf, g_row, b_row):
    m = x2d.shape[0]
    tm = min(OUT_ROW_TILE, m)
    return pl.pallas_call(
        _out_proj_kernel,
        grid=(m // tm,),
        in_specs=[pl.BlockSpec((tm, D_ATTN), lambda i: (i, 0)),
                  pl.BlockSpec((tm, D_LRU), lambda i: (i, 0)),
                  pl.BlockSpec((tm, D_MODEL), lambda i: (i, 0)),
                  _resident((D_MODEL, D_MODEL)),
                  _resident((1, D_MODEL)),
                  _resident((1, D_MODEL))],
        out_specs=pl.BlockSpec((tm, D_MODEL), lambda i: (i, 0)),
        out_shape=jax.ShapeDtypeStruct((m, D_MODEL), F32),
        compiler_params=pltpu.CompilerParams(
            dimension_semantics=("arbitrary",), vmem_limit_bytes=VMEM_LIMIT_BYTES),
        name="out_proj",
    )(ua2d, ul2d, x2d, w_bf, g_row, b_row)


def _in_proj_kernel(x_ref, w_ref, b_ref, z_ref, w_bf_ref):
    @pl.when(pl.program_id(0) == 0)
    def _start_from_bias():
        z_ref[...] = jnp.broadcast_to(b_ref[...], z_ref.shape)

    xb = x_ref[...].astype(BF16)
    for n in range(0, D_IN, N_TILE):
        wb = w_ref[:, n:n + N_TILE].astype(BF16)
        w_bf_ref[:, n:n + N_TILE] = wb
        z_ref[:, n:n + N_TILE] += jnp.dot(xb, wb, preferred_element_type=F32)


def _in_proj(x2d, w, b_row):
    m = x2d.shape[0]
    return pl.pallas_call(
        _in_proj_kernel,
        grid=(D_MODEL // MXU_DIM,),
        in_specs=[pl.BlockSpec((m, MXU_DIM), lambda k: (0, k)),
                  pl.BlockSpec((MXU_DIM, D_IN), lambda k: (k, 0)),
                  pl.BlockSpec((1, D_IN), lambda k: (0, 0))],
        out_specs=[pl.BlockSpec((m, D_IN), lambda k: (0, 0)),
                   pl.BlockSpec((MXU_DIM, D_IN), lambda k: (k, 0))],
        out_shape=[jax.ShapeDtypeStruct((m, D_IN), F32),
                   jax.ShapeDtypeStruct((D_MODEL, D_IN), BF16)],
        compiler_params=pltpu.CompilerParams(
            dimension_semantics=("arbitrary",), vmem_limit_bytes=VMEM_LIMIT_BYTES),
        name="in_proj",
    )(x2d, w, b_row)


def _sample_mixer_kernel(sinks_ref, z_ref, ck_ref, cv_ref, sc_ref, sh_ref,
                         cw_ref, cb_ref, wg_ref, bg_ref, lam_ref,
                         ua_ref, ul_ref, kwin_ref, vwin_ref, conv_ref, hout_ref,
                         bias_s, sink_s, s_ring, k4_s, v4_s, xbuf, a_s, b_s, hc, *, t_rows, n_cache):
    b = pl.program_id(0)
    n_keys = n_cache + t_rows
    z = _ColGroups([(0, D_IN, z_ref.at[0])])
    ua = ua_ref.at[0]

    @pl.when(b == 0)
    def _init_tables():
        for kv in range(N_KV_HEADS):
            bias_s[kv] = _bias_table(t_rows, kv, PAST_LEN - n_cache, PAST_LEN, n_keys)
            sink_s[kv] = _sink_table(t_rows, kv, sinks_ref)
        k4_s[...] = jnp.zeros_like(k4_s)
        v4_s[:, :, 0:LANES] = jnp.zeros((N_KV_HEADS, KEY_PAD, LANES), BF16)
        v4_s[:, :, LANES:] = jnp.ones((N_KV_HEADS, KEY_PAD, LANES), BF16)

    k_all = jnp.concatenate([ck_ref[0], z[_ALL_ROWS, OFF_K:OFF_K + D_KV]], axis=0)
    v_all = jnp.concatenate([cv_ref[0], z[_ALL_ROWS, OFF_V:OFF_V + D_KV]], axis=0)
    for kv in range(N_KV_HEADS):
        k4_s[kv, 0:n_keys, :] = _tile_heads(k_all, kv, GQA_GROUP)
        v4_s[kv, 0:n_keys, 0:LANES] = _tile_heads(v_all, kv, 2)
    for kv in range(N_KV_HEADS):
        s_ring[kv] = _attn_scores(z[_ALL_ROWS, OFF_Q + kv * GROUP_W:OFF_Q + (kv + 1) * GROUP_W], k4_s[kv])
    for kv in range(N_KV_HEADS):
        o = _attn_finish(s_ring[kv], v4_s[kv], bias_s[kv], sink_s[kv])
        ga = z[_ALL_ROWS, OFF_GA + kv * GROUP_W:OFF_GA + (kv + 1) * GROUP_W]
        ua[:, kv * GROUP_W:(kv + 1) * GROUP_W] = (o * (ga * _sigmoid(ga))).astype(ua.dtype)
    kwin_ref[0] = k_all[n_keys - WINDOW:, :]
    vwin_ref[0] = v_all[n_keys - WINDOW:, :]

    xbuf[0:SUBLANES, :] = jnp.zeros((SUBLANES, D_LRU), F32)
    xbuf[SUBLANES - (CONV_WIDTH - 1):SUBLANES, :] = sc_ref[0]
    hc[...] = jnp.broadcast_to(sh_ref[0], (SUBLANES, D_LRU))
    _lru(z, ul_ref.at[0], t_rows, xbuf, a_s, b_s, hc, cw_ref, cb_ref, wg_ref, bg_ref, lam_ref)
    conv_ref[0] = xbuf[SUBLANES - (CONV_WIDTH - 1):SUBLANES, :]
    hout_ref[0] = hc[0:1, :]


def _sample_mixer(z3, cache_k, cache_v, state_conv, state_h, sinks, cw, cb, wg, bg, lam):
    bsz, t_rows, _ = z3.shape
    n_cache = cache_k.shape[1]
    assert n_cache + t_rows <= KEY_PAD and n_cache + t_rows >= WINDOW
    assert t_rows % SUBLANES == 0 and t_rows & (t_rows - 1) == 0 and t_rows >= CONV_WIDTH - 1
    smem = pl.BlockSpec(memory_space=pltpu.SMEM)
    per_b = lambda shape: pl.BlockSpec((1,) + shape, lambda b: (b, 0, 0))
    return pl.pallas_call(
        functools.partial(_sample_mixer_kernel, t_rows=t_rows, n_cache=n_cache),
        grid=(bsz,),
        in_specs=[smem, per_b((t_rows, D_IN)), per_b((n_cache, D_KV)), per_b((n_cache, D_KV)),
                  per_b((CONV_WIDTH - 1, D_LRU)), per_b((1, D_LRU)),
                  _resident((CONV_WIDTH, D_LRU)), _resident((1, D_LRU)),
                  _resident((N_GATE_TILES, GATE_TILE, 2 * GATE_TILE)),
                  _resident((2, D_LRU)), _resident((1, D_LRU))],
        out_specs=[per_b((t_rows, D_ATTN)), per_b((t_rows, D_LRU)), per_b((WINDOW, D_KV)), per_b((WINDOW, D_KV)),
                   per_b((CONV_WIDTH - 1, D_LRU)), per_b((1, D_LRU))],
        out_shape=[jax.ShapeDtypeStruct((bsz, t_rows, D_ATTN), BF16),
                   jax.ShapeDtypeStruct((bsz, t_rows, D_LRU), BF16),
                   jax.ShapeDtypeStruct((bsz, WINDOW, D_KV), F32),
                   jax.ShapeDtypeStruct((bsz, WINDOW, D_KV), F32),
                   jax.ShapeDtypeStruct((bsz, CONV_WIDTH - 1, D_LRU), F32),
                   jax.ShapeDtypeStruct((bsz, 1, D_LRU), F32)],
        scratch_shapes=[pltpu.VMEM((N_KV_HEADS, GQA_GROUP * t_rows, KEY_PAD), F32),
                        pltpu.VMEM((N_KV_HEADS, GQA_GROUP * t_rows, LANES), F32),
                        pltpu.VMEM((N_KV_HEADS, GQA_GROUP * t_rows, KEY_PAD), F32),
                        pltpu.VMEM((N_KV_HEADS, KEY_PAD, GROUP_W), BF16),
                        pltpu.VMEM((N_KV_HEADS, KEY_PAD, V_EXT_W), BF16),
                        pltpu.VMEM((SUBLANES + t_rows, D_LRU), F32),
                        pltpu.VMEM((t_rows, D_LRU), F32),
                        pltpu.VMEM((t_rows, D_LRU), F32),
                        pltpu.VMEM((SUBLANES, D_LRU), F32)],
        compiler_params=pltpu.CompilerParams(
            dimension_semantics=("arbitrary",), vmem_limit_bytes=VMEM_LIMIT_BYTES),
        name="sample_mixer",
    )(sinks, z3, cache_k, cache_v, state_conv, state_h, cw, cb, wg, bg, lam)


def _gate_weight_tiles(w_a, w_x):
    per = GATE_TILE // LRU_BLOCK
    w5 = jnp.stack([w_a, w_x]).reshape(2, N_GATE_TILES, per, LRU_BLOCK, LRU_BLOCK)
    eye = jnp.eye(per, dtype=w_a.dtype)
    return jnp.einsum('gtade,ab->tadgbe', w5, eye).reshape(N_GATE_TILES, GATE_TILE, 2 * GATE_TILE).astype(BF16)


def kernel(x_prompt, x_sample, cache_k, cache_v, state_conv, state_h, w_in, b_in, conv_w, conv_b,
           w_gate_a, b_gate_a, w_gate_x, b_gate_x, lru_lambda, attn_sinks, w_out, ln_g, ln_b):
    assert w_in.shape[0] == DEPTH == 1
    bsz, seq, _ = x_prompt.shape
    dbsz, dseq, _ = x_sample.shape
    n_cache = cache_k.shape[2]
    n_p, n_s = bsz * seq, dbsz * dseq
    assert seq % ROW_TILE == 0 and n_p % OUT_ROW_TILE == 0

    wg = _gate_weight_tiles(w_gate_a[0], w_gate_x[0])
    bg = jnp.stack([b_gate_a[0].reshape(D_LRU), b_gate_x[0].reshape(D_LRU)])
    cw = conv_w[0]
    sinks = attn_sinks[0]

    xs2 = x_sample.reshape(n_s, D_MODEL)
    zs, w_in_bf = _in_proj(xs2, w_in.reshape(D_MODEL, D_IN), b_in)
    zs = zs.reshape(dbsz, dseq, D_IN)

    uap, ulp, kp, vp, cp, hp, w_out_bf = _prompt_mixer(
        x_prompt, sinks, w_in_bf, b_in, cw, conv_b, wg, bg, lru_lambda, w_out.reshape(D_MODEL, D_MODEL))
    yp = _out_proj(uap.reshape(n_p, D_ATTN), ulp.reshape(n_p, D_LRU), x_prompt.reshape(n_p, D_MODEL),
                   w_out_bf, ln_g, ln_b).reshape(bsz, seq, D_MODEL)

    uas, uls, ks, vs, cs, hs = _sample_mixer(
        zs, cache_k[0].reshape(dbsz, n_cache, D_KV), cache_v[0].reshape(dbsz, n_cache, D_KV),
        state_conv[0], state_h[0].reshape(dbsz, 1, D_LRU),
        sinks, cw, conv_b, wg, bg, lru_lambda)
    ys = _out_proj(uas.reshape(n_s, D_ATTN), uls.reshape(n_s, D_LRU), xs2,
                   w_out_bf, ln_g, ln_b).reshape(dbsz, dseq, D_MODEL)

    def from_channel_major(w):
        return w.reshape(bsz, N_KV_HEADS, HEAD_DIM, WINDOW).transpose(0, 3, 1, 2)[None]

    def from_row_major(w):
        return w.reshape(1, dbsz, WINDOW, N_KV_HEADS, HEAD_DIM)

    return (yp, ys,
            from_channel_major(kp), from_channel_major(vp), cp.transpose(1, 0, 2)[None], hp[None],
            from_row_major(ks), from_row_major(vs), cs[None], hs.reshape(1, dbsz, D_LRU))
```

```python
import functools

import jax
import jax.numpy as jnp
from jax import lax
from jax.experimental import pallas as pl
from jax.experimental.pallas import tpu as pltpu

F32 = jnp.float32
BF16 = jnp.bfloat16

D_MODEL = 2048
DEPTH = 1
CHUNK = 64
WINDOW = 128
WINDOW_CHUNKS = WINDOW // CHUNK
HEAD_DIM = 64
D_ATTN = D_MODEL // 2
D_LRU = D_MODEL - D_ATTN
N_Q_HEADS = D_ATTN // HEAD_DIM
N_KV_HEADS = 4
GQA_GROUP = N_Q_HEADS // N_KV_HEADS
D_KV = N_KV_HEADS * HEAD_DIM
N_LRU_BLOCKS = 16
LRU_BLOCK = D_LRU // N_LRU_BLOCKS
CONV_WIDTH = 4
LRU_C = 8.0
D_IN = 2 * D_ATTN + 2 * D_KV + 2 * D_LRU
PAST_LEN = 2048
DEEPNORM_ALPHA = (2.0 * DEPTH) ** 0.25
LN_EPS = 1e-5
NEG_INF = -1e30

OFF_Q = 0
OFF_K = D_ATTN
OFF_V = D_ATTN + D_KV
OFF_GA = D_ATTN + 2 * D_KV
OFF_XL = OFF_GA + D_ATTN
OFF_GL = OFF_XL + D_LRU

SUBLANES = 8
LANES = 128
MXU_DIM = 256
VMEM_LIMIT_BYTES = 56 * 1024 * 1024

GROUP_W = GQA_GROUP * HEAD_DIM
KEY_PAD = 2 * LANES
V_EXT_W = 2 * LANES
GATE_TILE = MXU_DIM
N_GATE_TILES = D_LRU // GATE_TILE
ROW_TILE = 256
OUT_ROW_TILE = 1024
UNITS_PER_LATE_PIECE = 4
SCORE_LOOKAHEAD = 2
SCORE_RING = SCORE_LOOKAHEAD + 2
N_TILE = 512
N_CHUNKS_PER_TILE = ROW_TILE // CHUNK
KBUF_ROWS = WINDOW + ROW_TILE + (KEY_PAD - (WINDOW + CHUNK))

_SLOPES = tuple(2.0 ** (-8.0 * h / N_Q_HEADS) for h in range(1, N_Q_HEADS + 1))


def _resident(shape):
    return pl.BlockSpec(shape, lambda *_: (0,) * len(shape), pipeline_mode=pl.Buffered(1))


class _ColGroups:
    def __init__(self, groups):
        self._groups = groups

    def _locate(self, idx):
        rows, cols = idx
        for first, end, ref in self._groups:
            if first <= cols.start and cols.stop <= end:
                return ref, rows, slice(cols.start - first, cols.stop - first)
        raise ValueError(f"columns {cols} straddle projection groups")

    def __getitem__(self, idx):
        ref, rows, cols = self._locate(idx)
        return ref[rows, cols]

    def __setitem__(self, idx, value):
        ref, rows, cols = self._locate(idx)
        ref[rows, cols] = value


_ALL_ROWS = slice(None)
_PROJ_GROUPS = ((OFF_Q, OFF_K), (OFF_K, OFF_GA), (OFF_GA, OFF_XL), (OFF_XL, OFF_GL), (OFF_GL, D_IN))


def _sigmoid(x):
    return 1.0 / (1.0 + jnp.exp(-x))


def _in_proj_into(xb, w_ref, b_ref, z, first=0, end=D_IN, width=N_TILE):
    for n in range(first, end, width):
        z[_ALL_ROWS, n:n + width] = (
            jnp.dot(xb, w_ref[:, n:n + width], preferred_element_type=F32) + b_ref[:, n:n + width])


def _bias_table(nq, kv, key0_pos, q0_pos, n_keys):
    rows = GQA_GROUP * nq
    r = lax.broadcasted_iota(jnp.int32, (rows, KEY_PAD), 0)
    c = lax.broadcasted_iota(jnp.int32, (rows, KEY_PAD), 1)
    shift = nq.bit_length() - 1
    g = jnp.right_shift(r, shift)
    qpos = q0_pos + (r & (nq - 1))
    kpos = key0_pos + c
    slope = jnp.full((rows, KEY_PAD), _SLOPES[kv * GQA_GROUP], F32)
    for gg in range(1, GQA_GROUP):
        slope = jnp.where(g == gg, _SLOPES[kv * GQA_GROUP + gg], slope)
    dist = jnp.abs(qpos - kpos).astype(F32)
    cshift = CHUNK.bit_length() - 1
    qc = jnp.right_shift(qpos, cshift)
    kc = jnp.right_shift(kpos, cshift)
    valid = (kpos >= 0) & (kc <= qc) & (kc >= qc - WINDOW_CHUNKS) & (c < n_keys)
    return jnp.where(valid, -(slope * dist), NEG_INF)


def _tile_heads(x, kv, copies):
    pair = x[:, (kv // 2) * LANES:(kv // 2 + 1) * LANES]
    rolled = pltpu.roll(pair, HEAD_DIM, axis=1)
    lane = lax.broadcasted_iota(jnp.int32, pair.shape, 1)
    low = lane < HEAD_DIM
    both = jnp.where(low, pair, rolled) if kv % 2 == 0 else jnp.where(low, rolled, pair)
    both = both.astype(BF16)
    return jnp.concatenate([both] * (copies * HEAD_DIM // LANES), axis=1)


def _sink_table(nq, kv, sinks_ref):
    return jnp.concatenate(
        [jnp.full((nq, LANES), sinks_ref[kv * GQA_GROUP + g], F32) for g in range(GQA_GROUP)], axis=0)


def _head_of_lane(nq):
    lane = lax.broadcasted_iota(jnp.int32, (nq, GROUP_W), 1)
    return jnp.right_shift(lane, HEAD_DIM.bit_length() - 1)


def _attn_scores(q, k4):
    nq = q.shape[0]
    qb = (q * (HEAD_DIM ** -0.5)).astype(BF16)
    head_of_lane = _head_of_lane(nq)
    zero = jnp.zeros_like(qb)
    qs = jnp.concatenate([jnp.where(head_of_lane == g, qb, zero) for g in range(GQA_GROUP)], axis=0)
    return lax.dot_general(qs, k4, (((1,), (1,)), ((), ())), preferred_element_type=F32)


def _attn_finish(scores, v4e, bias, sink):
    nq = scores.shape[0] // GQA_GROUP
    s = scores + bias
    m = jnp.maximum(jnp.broadcast_to(jnp.max(s, axis=-1, keepdims=True), sink.shape), sink)
    e = jnp.concatenate(
        [jnp.exp(s[:, j * LANES:(j + 1) * LANES] - m) for j in range(KEY_PAD // LANES)], axis=1)
    o4e = jnp.dot(e.astype(BF16), v4e, preferred_element_type=F32)
    o2 = o4e[:, :LANES] * (1.0 / (o4e[:, LANES:] + jnp.exp(sink - m)))
    low = lax.broadcasted_iota(jnp.int32, (nq, LANES), 1) < HEAD_DIM
    return jnp.concatenate(
        [jnp.where(low, o2[g * nq:(g + 1) * nq], o2[(g + 1) * nq:(g + 2) * nq]) for g in range(0, GQA_GROUP, 2)],
        axis=1)


def _lru(z, u_out, t_rows, xbuf, a_s, b_s, hc, cw_ref, cb_ref, wg_ref, bg_ref, lam_ref, fill=None):
    fill = fill or (lambda: None)
    neg = -lam_ref[...]
    softplus = jnp.maximum(neg, 0.0) + jnp.log1p(jnp.exp(-jnp.abs(neg)))
    coef = -LRU_C * softplus
    tiles = [slice(j * GATE_TILE, (j + 1) * GATE_TILE) for j in range(N_GATE_TILES)]

    def conv_and_gate_matmul(cols, j):
        xl = z[_ALL_ROWS, OFF_XL + cols.start:OFF_XL + cols.stop]
        xbuf[SUBLANES:SUBLANES + t_rows, cols] = xl
        xc = cb_ref[:, cols] + cw_ref[CONV_WIDTH - 1:CONV_WIDTH, cols] * xl
        for k in range(CONV_WIDTH - 1):
            off = SUBLANES - (CONV_WIDTH - 1) + k
            xc = xc + cw_ref[k:k + 1, cols] * xbuf[off:off + t_rows, cols]
        pre = jnp.dot(xc.astype(BF16), wg_ref[j], preferred_element_type=F32)
        a_s[:, cols] = pre[:, :GATE_TILE]
        b_s[:, cols] = pre[:, GATE_TILE:]
        return xc

    def gate_math(cols, xc):
        r = _sigmoid(a_s[:, cols] + bg_ref[0:1, cols])
        i = _sigmoid(b_s[:, cols] + bg_ref[1:2, cols])
        a = jnp.exp(coef[:, cols] * r)
        mult = jnp.sqrt(jnp.maximum(1.0 - a * a, 0.0))
        a_s[:, cols] = a
        b_s[:, cols] = mult * (i * xc)

    fill()
    xc_prev = None
    for j, cols in enumerate(tiles):
        xc = conv_and_gate_matmul(cols, j)
        if xc_prev is not None:
            gate_math(tiles[j - 1], xc_prev)
        xc_prev = xc
        fill()
    gate_math(tiles[-1], xc_prev)

    row = lax.broadcasted_iota(jnp.int32, (SUBLANES, D_LRU), 0)
    h_prev = hc[...]
    for gi in range(t_rows // SUBLANES):
        rows = slice(gi * SUBLANES, (gi + 1) * SUBLANES)
        a8 = a_s[rows, :]
        b8 = b_s[rows, :]
        for d in (1, 2, 4):
            keep = row >= d
            a_sh = pltpu.roll(a8, d, axis=0)
            b_sh = pltpu.roll(b8, d, axis=0)
            b8 = jnp.where(keep, a8 * b_sh + b8, b8)
            a8 = jnp.where(keep, a8 * a_sh, a8)
        h8 = a8 * h_prev + b8
        b_s[rows, :] = h8
        h_prev = jnp.broadcast_to(h8[SUBLANES - 1:SUBLANES, :], (SUBLANES, D_LRU))
    hc[...] = h_prev
    gl = z[_ALL_ROWS, OFF_GL:OFF_GL + D_LRU]
    u_out[...] = (b_s[...] * (gl * _sigmoid(gl))).astype(u_out.dtype)
    xbuf[0:SUBLANES, :] = xbuf[t_rows:t_rows + SUBLANES, :]


def _prompt_kernel(sinks_ref, x_ref, win_ref, bin_ref, cw_ref, cb_ref, wg_ref, bg_ref, lam_ref, wout_ref,
                   ua_ref, ul_ref, kwin_ref, vwin_ref, conv_ref, hout_ref, wout_bf_ref,
                   zq, zkv, zga, zxl, zgl, bias_s, mask_s, sink_s, s_ring, k4_s, v4_s, xbuf, a_s, b_s, hc,
                   *, n_t, n_tiles):
    wout_bf_ref[...] = wout_ref[...].astype(BF16)
    z = _ColGroups([(first, end, ref) for (first, end), ref in zip(_PROJ_GROUPS, (zq, zkv, zga, zxl, zgl))])
    ua = ua_ref.at[0]
    s = pl.program_id(0)
    t_a = lax.rem(jnp.minimum(s, n_tiles - 1), n_t)
    t_l = lax.rem(jnp.maximum(s - 1, 0), n_t)

    @pl.when(s == 0)
    def _init():
        for kv in range(N_KV_HEADS):
            bias_s[kv] = _bias_table(CHUNK, kv, 0, WINDOW, WINDOW + CHUNK)
            sink_s[kv] = _sink_table(CHUNK, kv, sinks_ref)
        col = lax.broadcasted_iota(jnp.int32, (SUBLANES, KEY_PAD), 1)
        for ci in range(WINDOW_CHUNKS):
            mask_s[ci] = jnp.where(col < WINDOW - ci * CHUNK, NEG_INF, 0.0)
        k4_s[...] = jnp.zeros_like(k4_s)
        v4_s[:, :, 0:LANES] = jnp.zeros((N_KV_HEADS, KBUF_ROWS, LANES), BF16)
        v4_s[:, :, LANES:] = jnp.ones((N_KV_HEADS, KBUF_ROWS, LANES), BF16)
        zxl[...] = jnp.zeros_like(zxl)
        zgl[...] = jnp.zeros_like(zgl)

    @pl.when(t_a == 0)
    def _reset_attention_carry():
        k4_s[:, 0:WINDOW, :] = jnp.zeros((N_KV_HEADS, WINDOW, GROUP_W), BF16)
        v4_s[:, 0:WINDOW, 0:LANES] = jnp.zeros((N_KV_HEADS, WINDOW, LANES), BF16)

    @pl.when(t_l == 0)
    def _reset_lru_carry():
        xbuf[0:SUBLANES, :] = jnp.zeros((SUBLANES, D_LRU), F32)
        hc[...] = jnp.zeros_like(hc)

    xb = x_ref[0].astype(BF16)
    early_cols = list(range(OFF_K, OFF_GA, N_TILE)) + list(range(OFF_Q, OFF_K, N_TILE)) + list(
        range(OFF_GA, OFF_XL, N_TILE))
    late_cols = list(range(OFF_XL, D_IN, N_TILE))

    def project(cols):
        if cols:
            c0 = cols.pop(0)
            _in_proj_into(xb, win_ref, bin_ref, z, c0, c0 + N_TILE)

    _lru(z, ul_ref.at[0], ROW_TILE, xbuf, a_s, b_s, hc, cw_ref, cb_ref, wg_ref, bg_ref, lam_ref,
         fill=lambda: project(early_cols))
    while early_cols:
        project(early_cols)

    k_new = z[_ALL_ROWS, OFF_K:OFF_K + D_KV]
    v_new = z[_ALL_ROWS, OFF_V:OFF_V + D_KV]
    for kv in range(N_KV_HEADS):
        k4_s[kv, WINDOW:WINDOW + ROW_TILE, :] = _tile_heads(k_new, kv, GQA_GROUP)
        v4_s[kv, WINDOW:WINDOW + ROW_TILE, 0:LANES] = _tile_heads(v_new, kv, 2)

    units = [(ci, kv) for ci in range(N_CHUNKS_PER_TILE) for kv in range(N_KV_HEADS)]

    def scores_into_ring(idx):
        ci, kv = units[idx]
        s_ring[idx % SCORE_RING] = _attn_scores(
            z[ci * CHUNK:(ci + 1) * CHUNK, OFF_Q + kv * GROUP_W:OFF_Q + (kv + 1) * GROUP_W],
            k4_s[kv, ci * CHUNK:ci * CHUNK + KEY_PAD, :])

    mask_rows = [jnp.where(t_a == 0, mask_s[ci, 0:1, :], 0.0) for ci in range(WINDOW_CHUNKS)]
    for idx in range(SCORE_LOOKAHEAD):
        scores_into_ring(idx)
    for idx, (ci, kv) in enumerate(units):
        if idx + SCORE_LOOKAHEAD < len(units):
            scores_into_ring(idx + SCORE_LOOKAHEAD)
        if idx % UNITS_PER_LATE_PIECE == 1:
            project(late_cols)
        s_cur = s_ring[idx % SCORE_RING]
        rows = slice(ci * CHUNK, (ci + 1) * CHUNK)
        bias = bias_s[kv] + mask_rows[ci] if ci < WINDOW_CHUNKS else bias_s[kv]
        o = _attn_finish(s_cur, v4_s[kv, ci * CHUNK:ci * CHUNK + KEY_PAD, :], bias, sink_s[kv])
        ga = z[rows, OFF_GA + kv * GROUP_W:OFF_GA + (kv + 1) * GROUP_W]
        ua[rows, kv * GROUP_W:(kv + 1) * GROUP_W] = (o * (ga * _sigmoid(ga))).astype(ua.dtype)
    assert not early_cols and not late_cols

    @pl.when(t_l == n_t - 1)
    def _emit_lru_state():
        seq_l = lax.div(jnp.maximum(s - 1, 0), n_t)
        for j in range(CONV_WIDTH - 1):
            row = SUBLANES - (CONV_WIDTH - 1) + j
            conv_ref[j, pl.ds(seq_l, 1), :] = xbuf[row:row + 1, :]
        hout_ref[pl.ds(seq_l, 1), :] = hc[0:1, :]

    @pl.when(t_a == n_t - 1)
    def _emit_kv_window():
        kwin_ref[0] = z[slice(ROW_TILE - WINDOW, ROW_TILE), OFF_K:OFF_K + D_KV].T
        vwin_ref[0] = z[slice(ROW_TILE - WINDOW, ROW_TILE), OFF_V:OFF_V + D_KV].T

    @pl.when(s < n_tiles - 1)
    def _carry_kv():
        k4_s[:, 0:WINDOW, :] = k4_s[:, ROW_TILE:ROW_TILE + WINDOW, :]
        v4_s[:, 0:WINDOW, 0:LANES] = v4_s[:, ROW_TILE:ROW_TILE + WINDOW, 0:LANES]


def _prompt_mixer(x3, sinks, w_in_bf, b_in_row, cw, cb, wg, bg, lam, w_out):
    bsz, seq, _ = x3.shape
    n_t = seq // ROW_TILE
    n_tiles = bsz * n_t
    smem = pl.BlockSpec(memory_space=pltpu.SMEM)
    assert D_MODEL % n_tiles == 0 and (D_MODEL // n_tiles) % (2 * SUBLANES) == 0
    w_out_slab = pl.BlockSpec((D_MODEL // n_tiles, D_MODEL), lambda s: (jnp.minimum(s, n_tiles - 1), 0))

    def attn_tile(s):
        ta = jnp.minimum(s, n_tiles - 1)
        return ta // n_t, ta % n_t

    def lru_tile(s):
        tl = jnp.maximum(s - 1, 0)
        return tl // n_t, tl % n_t

    return pl.pallas_call(
        functools.partial(_prompt_kernel, n_t=n_t, n_tiles=n_tiles),
        grid=(n_tiles + 1,),
        in_specs=[smem,
                  pl.BlockSpec((1, ROW_TILE, D_MODEL), lambda s: (*attn_tile(s), 0)),
                  _resident((D_MODEL, D_IN)), _resident((1, D_IN)),
                  _resident((CONV_WIDTH, D_LRU)), _resident((1, D_LRU)),
                  _resident((N_GATE_TILES, GATE_TILE, 2 * GATE_TILE)),
                  _resident((2, D_LRU)), _resident((1, D_LRU)), w_out_slab],
        out_specs=[pl.BlockSpec((1, ROW_TILE, D_ATTN), lambda s: (*attn_tile(s), 0)),
                   pl.BlockSpec((1, ROW_TILE, D_LRU), lambda s: (*lru_tile(s), 0)),
                   pl.BlockSpec((1, D_KV, WINDOW), lambda s: (attn_tile(s)[0], 0, 0)),
                   pl.BlockSpec((1, D_KV, WINDOW), lambda s: (attn_tile(s)[0], 0, 0)),
                   pl.BlockSpec((CONV_WIDTH - 1, bsz, D_LRU), lambda s: (0, 0, 0)),
                   pl.BlockSpec((bsz, D_LRU), lambda s: (0, 0)), w_out_slab],
        out_shape=[jax.ShapeDtypeStruct((bsz, seq, D_ATTN), BF16),
                   jax.ShapeDtypeStruct((bsz, seq, D_LRU), BF16),
                   jax.ShapeDtypeStruct((bsz, D_KV, WINDOW), F32),
                   jax.ShapeDtypeStruct((bsz, D_KV, WINDOW), F32),
                   jax.ShapeDtypeStruct((CONV_WIDTH - 1, bsz, D_LRU), F32),
                   jax.ShapeDtypeStruct((bsz, D_LRU), F32),
                   jax.ShapeDtypeStruct((D_MODEL, D_MODEL), BF16)],
        scratch_shapes=[pltpu.VMEM((ROW_TILE, end - first), F32) for first, end in _PROJ_GROUPS] + [
                        pltpu.VMEM((N_KV_HEADS, GQA_GROUP * CHUNK, KEY_PAD), F32),
                        pltpu.VMEM((WINDOW_CHUNKS, SUBLANES, KEY_PAD), F32),
                        pltpu.VMEM((N_KV_HEADS, GQA_GROUP * CHUNK, LANES), F32),
                        pltpu.VMEM((SCORE_RING, GQA_GROUP * CHUNK, KEY_PAD), F32),
                        pltpu.VMEM((N_KV_HEADS, KBUF_ROWS, GROUP_W), BF16),
                        pltpu.VMEM((N_KV_HEADS, KBUF_ROWS, V_EXT_W), BF16),
                        pltpu.VMEM((SUBLANES + ROW_TILE, D_LRU), F32),
                        pltpu.VMEM((ROW_TILE, D_LRU), F32),
                        pltpu.VMEM((ROW_TILE, D_LRU), F32),
                        pltpu.VMEM((SUBLANES, D_LRU), F32)],
        compiler_params=pltpu.CompilerParams(
            dimension_semantics=("arbitrary",), vmem_limit_bytes=VMEM_LIMIT_BYTES),
        name="prompt_mixer",
    )(sinks, x3, w_in_bf, b_in_row, cw, cb, wg, bg, lam, w_out)


def _out_proj_kernel(ua_ref, ul_ref, x_ref, w_ref, g_ref, b_ref, y_ref):
    tm = y_ref.shape[0]
    part = min(MXU_DIM, tm)
    n_parts = tm // part
    n_chunks = D_MODEL // N_TILE
    ln_rows = part // n_chunks

    def matmul_chunk(p, n):
        rows = slice(p * part, (p + 1) * part)
        cols = slice(n * N_TILE, (n + 1) * N_TILE)
        y_ref[rows, cols] = (
            DEEPNORM_ALPHA * x_ref[rows, cols]
            + jnp.dot(ua_ref[rows, :], w_ref[0:D_ATTN, cols], preferred_element_type=F32)
            + jnp.dot(ul_ref[rows, :], w_ref[D_ATTN:, cols], preferred_element_type=F32))

    def layer_norm_rows(p, r):
        rows = slice(p * part + r * ln_rows, p * part + (r + 1) * ln_rows)
        y = y_ref[rows, :]
        mu = jnp.mean(y, axis=-1, keepdims=True)
        yc = y - mu
        var = jnp.mean(yc * yc, axis=-1, keepdims=True)
        y_ref[rows, :] = yc * lax.rsqrt(var + LN_EPS) * g_ref[...] + b_ref[...]

    for n in range(n_chunks):
        matmul_chunk(0, n)
    for p in range(1, n_parts):
        for n in range(n_chunks):
            matmul_chunk(p, n)
            layer_norm_rows(p - 1, n)
    for r in range(n_chunks):
        layer_norm_rows(n_parts - 1, r)


def _out_proj(ua2d, ul2d, x2d, w_bf, g_row, b_row):
    m = x2d.shape[0]
    tm = min(OUT_ROW_TILE, m)
    return pl.pallas_call(
        _out_proj_kernel,
        grid=(m // tm,),
        in_specs=[pl.BlockSpec((tm, D_ATTN), lambda i: (i, 0)),
                  pl.BlockSpec((tm, D_LRU), lambda i: (i, 0)),
                  pl.BlockSpec((tm, D_MODEL), lambda i: (i, 0)),
                  _resident((D_MODEL, D_MODEL)),
                  _resident((1, D_MODEL)),
                  _resident((1, D_MODEL))],
        out_specs=pl.BlockSpec((tm, D_MODEL), lambda i: (i, 0)),
        out_shape=jax.ShapeDtypeStruct((m, D_MODEL), F32),
        compiler_params=pltpu.CompilerParams(
            dimension_semantics=("arbitrary",), vmem_limit_bytes=VMEM_LIMIT_BYTES),
        name="out_proj",
    )(ua2d, ul2d, x2d, w_bf, g_row, b_row)


def _in_proj_kernel(x_ref, w_ref, b_ref, z_ref, w_bf_ref):
    @pl.when(pl.program_id(0) == 0)
    def _start_from_bias():
        z_ref[...] = jnp.broadcast_to(b_ref[...], z_ref.shape)

    xb = x_ref[...].astype(BF16)
    for n in range(0, D_IN, N_TILE):
        wb = w_ref[:, n:n + N_TILE].astype(BF16)
        w_bf_ref[:, n:n + N_TILE] = wb
        z_ref[:, n:n + N_TILE] += jnp.dot(xb, wb, preferred_element_type=F32)


def _in_proj(x2d, w, b_row):
    m = x2d.shape[0]
    return pl.pallas_call(
        _in_proj_kernel,
        grid=(D_MODEL // MXU_DIM,),
        in_specs=[pl.BlockSpec((m, MXU_DIM), lambda k: (0, k)),
                  pl.BlockSpec((MXU_DIM, D_IN), lambda k: (k, 0)),
                  pl.BlockSpec((1, D_IN), lambda k: (0, 0))],
        out_specs=[pl.BlockSpec((m, D_IN), lambda k: (0, 0)),
                   pl.BlockSpec((MXU_DIM, D_IN), lambda k: (k, 0))],
        out_shape=[jax.ShapeDtypeStruct((m, D_IN), F32),
                   jax.ShapeDtypeStruct((D_MODEL, D_IN), BF16)],
        compiler_params=pltpu.CompilerParams(
            dimension_semantics=("arbitrary",), vmem_limit_bytes=VMEM_LIMIT_BYTES),
        name="in_proj",
    )(x2d, w, b_row)


def _sample_mixer_kernel(sinks_ref, z_ref, ck_ref, cv_ref, sc_ref, sh_ref,
                         cw_ref, cb_ref, wg_ref, bg_ref, lam_ref,
                         ua_ref, ul_ref, kwin_ref, vwin_ref, conv_ref, hout_ref,
                         bias_s, sink_s, s_ring, k4_s, v4_s, xbuf, a_s, b_s, hc, *, t_rows, n_cache):
    b = pl.program_id(0)
    n_keys = n_cache + t_rows
    z = _ColGroups([(0, D_IN, z_ref.at[0])])
    ua = ua_ref.at[0]

    @pl.when(b == 0)
    def _init_tables():
        for kv in range(N_KV_HEADS):
            bias_s[kv] = _bias_table(t_rows, kv, PAST_LEN - n_cache, PAST_LEN, n_keys)
            sink_s[kv] = _sink_table(t_rows, kv, sinks_ref)
        k4_s[...] = jnp.zeros_like(k4_s)
        v4_s[:, :, 0:LANES] = jnp.zeros((N_KV_HEADS, KEY_PAD, LANES), BF16)
        v4_s[:, :, LANES:] = jnp.ones((N_KV_HEADS, KEY_PAD, LANES), BF16)

    k_all = jnp.concatenate([ck_ref[0], z[_ALL_ROWS, OFF_K:OFF_K + D_KV]], axis=0)
    v_all = jnp.concatenate([cv_ref[0], z[_ALL_ROWS, OFF_V:OFF_V + D_KV]], axis=0)
    for kv in range(N_KV_HEADS):
        k4_s[kv, 0:n_keys, :] = _tile_heads(k_all, kv, GQA_GROUP)
        v4_s[kv, 0:n_keys, 0:LANES] = _tile_heads(v_all, kv, 2)
    for kv in range(N_KV_HEADS):
        s_ring[kv] = _attn_scores(z[_ALL_ROWS, OFF_Q + kv * GROUP_W:OFF_Q + (kv + 1) * GROUP_W], k4_s[kv])
    for kv in range(N_KV_HEADS):
        o = _attn_finish(s_ring[kv], v4_s[kv], bias_s[kv], sink_s[kv])
        ga = z[_ALL_ROWS, OFF_GA + kv * GROUP_W:OFF_GA + (kv + 1) * GROUP_W]
        ua[:, kv * GROUP_W:(kv + 1) * GROUP_W] = (o * (ga * _sigmoid(ga))).astype(ua.dtype)
    kwin_ref[0] = k_all[n_keys - WINDOW:, :]
    vwin_ref[0] = v_all[n_keys - WINDOW:, :]

    xbuf[0:SUBLANES, :] = jnp.zeros((SUBLANES, D_LRU), F32)
    for j in range(CONV_WIDTH - 1):
        row = SUBLANES - (CONV_WIDTH - 1) + j
        xbuf[row:row + 1, :] = sc_ref[j, pl.ds(b, 1), :]
    hc[...] = jnp.broadcast_to(sh_ref[pl.ds(b, 1), :], (SUBLANES, D_LRU))
    _lru(z, ul_ref.at[0], t_rows, xbuf, a_s, b_s, hc, cw_ref, cb_ref, wg_ref, bg_ref, lam_ref)
    for j in range(CONV_WIDTH - 1):
        row = SUBLANES - (CONV_WIDTH - 1) + j
        conv_ref[j, pl.ds(b, 1), :] = xbuf[row:row + 1, :]
    hout_ref[pl.ds(b, 1), :] = hc[0:1, :]


def _sample_mixer(z3, cache_k, cache_v, state_conv, state_h, sinks, cw, cb, wg, bg, lam):
    bsz, t_rows, _ = z3.shape
    n_cache = cache_k.shape[1]
    assert n_cache + t_rows <= KEY_PAD and n_cache + t_rows >= WINDOW
    assert t_rows % SUBLANES == 0 and t_rows & (t_rows - 1) == 0 and t_rows >= CONV_WIDTH - 1
    smem = pl.BlockSpec(memory_space=pltpu.SMEM)
    per_b = lambda shape: pl.BlockSpec((1,) + shape, lambda b: (b, 0, 0))
    return pl.pallas_call(
        functools.partial(_sample_mixer_kernel, t_rows=t_rows, n_cache=n_cache),
        grid=(bsz,),
        in_specs=[smem, per_b((t_rows, D_IN)), per_b((n_cache, D_KV)), per_b((n_cache, D_KV)),
                  _resident((CONV_WIDTH - 1, bsz, D_LRU)), _resident((bsz, D_LRU)),
                  _resident((CONV_WIDTH, D_LRU)), _resident((1, D_LRU)),
                  _resident((N_GATE_TILES, GATE_TILE, 2 * GATE_TILE)),
                  _resident((2, D_LRU)), _resident((1, D_LRU))],
        out_specs=[per_b((t_rows, D_ATTN)), per_b((t_rows, D_LRU)), per_b((WINDOW, D_KV)), per_b((WINDOW, D_KV)),
                   pl.BlockSpec((CONV_WIDTH - 1, bsz, D_LRU), lambda b: (0, 0, 0)),
                   pl.BlockSpec((bsz, D_LRU), lambda b: (0, 0))],
        out_shape=[jax.ShapeDtypeStruct((bsz, t_rows, D_ATTN), BF16),
                   jax.ShapeDtypeStruct((bsz, t_rows, D_LRU), BF16),
                   jax.ShapeDtypeStruct((bsz, WINDOW, D_KV), F32),
                   jax.ShapeDtypeStruct((bsz, WINDOW, D_KV), F32),
                   jax.ShapeDtypeStruct((CONV_WIDTH - 1, bsz, D_LRU), F32),
                   jax.ShapeDtypeStruct((bsz, D_LRU), F32)],
        scratch_shapes=[pltpu.VMEM((N_KV_HEADS, GQA_GROUP * t_rows, KEY_PAD), F32),
                        pltpu.VMEM((N_KV_HEADS, GQA_GROUP * t_rows, LANES), F32),
                        pltpu.VMEM((N_KV_HEADS, GQA_GROUP * t_rows, KEY_PAD), F32),
                        pltpu.VMEM((N_KV_HEADS, KEY_PAD, GROUP_W), BF16),
                        pltpu.VMEM((N_KV_HEADS, KEY_PAD, V_EXT_W), BF16),
                        pltpu.VMEM((SUBLANES + t_rows, D_LRU), F32),
                        pltpu.VMEM((t_rows, D_LRU), F32),
                        pltpu.VMEM((t_rows, D_LRU), F32),
                        pltpu.VMEM((SUBLANES, D_LRU), F32)],
        compiler_params=pltpu.CompilerParams(
            dimension_semantics=("arbitrary",), vmem_limit_bytes=VMEM_LIMIT_BYTES),
        name="sample_mixer",
    )(sinks, z3, cache_k, cache_v, state_conv, state_h, cw, cb, wg, bg, lam)


def _gate_weight_tiles(w_a, w_x):
    per = GATE_TILE // LRU_BLOCK
    row_block = lax.broadcasted_iota(jnp.int32, (GATE_TILE, GATE_TILE), 0) // LRU_BLOCK
    col_block = lax.broadcasted_iota(jnp.int32, (GATE_TILE, GATE_TILE), 1) // LRU_BLOCK

    def block_diagonal(w):
        rows = w.reshape(N_GATE_TILES, GATE_TILE, LRU_BLOCK)
        return jnp.where(row_block == col_block, jnp.concatenate([rows] * per, axis=-1), 0.0)

    return jnp.concatenate([block_diagonal(w_a), block_diagonal(w_x)], axis=-1).astype(BF16)


def kernel(x_prompt, x_sample, cache_k, cache_v, state_conv, state_h, w_in, b_in, conv_w, conv_b,
           w_gate_a, b_gate_a, w_gate_x, b_gate_x, lru_lambda, attn_sinks, w_out, ln_g, ln_b):
    assert w_in.shape[0] == DEPTH == 1
    bsz, seq, _ = x_prompt.shape
    dbsz, dseq, _ = x_sample.shape
    n_cache = cache_k.shape[2]
    n_p, n_s = bsz * seq, dbsz * dseq
    assert seq % ROW_TILE == 0 and n_p % OUT_ROW_TILE == 0

    wg = _gate_weight_tiles(w_gate_a[0], w_gate_x[0])
    bg = jnp.concatenate([b_gate_a[0], b_gate_x[0]], axis=0).reshape(2, D_LRU)
    cw = conv_w[0]
    sinks = attn_sinks[0]

    xs2 = x_sample.reshape(n_s, D_MODEL)
    zs, w_in_bf = _in_proj(xs2, w_in.reshape(D_MODEL, D_IN), b_in)
    zs = zs.reshape(dbsz, dseq, D_IN)

    uap, ulp, kp, vp, cp, hp, w_out_bf = _prompt_mixer(
        x_prompt, sinks, w_in_bf, b_in, cw, conv_b, wg, bg, lru_lambda, w_out.reshape(D_MODEL, D_MODEL))
    yp = _out_proj(uap.reshape(n_p, D_ATTN), ulp.reshape(n_p, D_LRU), x_prompt.reshape(n_p, D_MODEL),
                   w_out_bf, ln_g, ln_b).reshape(bsz, seq, D_MODEL)

    uas, uls, ks, vs, cs, hs = _sample_mixer(
        zs, cache_k[0].reshape(dbsz, n_cache, D_KV), cache_v[0].reshape(dbsz, n_cache, D_KV),
        state_conv[0].transpose(1, 0, 2), state_h[0],
        sinks, cw, conv_b, wg, bg, lru_lambda)
    ys = _out_proj(uas.reshape(n_s, D_ATTN), uls.reshape(n_s, D_LRU), xs2,
                   w_out_bf, ln_g, ln_b).reshape(dbsz, dseq, D_MODEL)

    def from_channel_major(w):
        return w.reshape(bsz, N_KV_HEADS, HEAD_DIM, WINDOW).transpose(0, 3, 1, 2)[None]

    def from_row_major(w):
        return w.reshape(1, dbsz, WINDOW, N_KV_HEADS, HEAD_DIM)

    return (yp, ys,
            from_channel_major(kp), from_channel_major(vp), cp.transpose(1, 0, 2)[None], hp[None],
            from_row_major(ks), from_row_major(vs), cs.transpose(1, 0, 2)[None], hs[None])
```

```python
import functools

import jax
import jax.numpy as jnp
from jax import lax
from jax.experimental import pallas as pl
from jax.experimental.pallas import tpu as pltpu

F32 = jnp.float32
BF16 = jnp.bfloat16

D_MODEL = 2048
DEPTH = 1
CHUNK = 64
WINDOW = 128
WINDOW_CHUNKS = WINDOW // CHUNK
HEAD_DIM = 64
D_ATTN = D_MODEL // 2
D_LRU = D_MODEL - D_ATTN
N_Q_HEADS = D_ATTN // HEAD_DIM
N_KV_HEADS = 4
GQA_GROUP = N_Q_HEADS // N_KV_HEADS
D_KV = N_KV_HEADS * HEAD_DIM
N_LRU_BLOCKS = 16
LRU_BLOCK = D_LRU // N_LRU_BLOCKS
CONV_WIDTH = 4
LRU_C = 8.0
D_IN = 2 * D_ATTN + 2 * D_KV + 2 * D_LRU
PAST_LEN = 2048
DEEPNORM_ALPHA = (2.0 * DEPTH) ** 0.25
LN_EPS = 1e-5
NEG_INF = -1e30

OFF_Q = 0
OFF_K = D_ATTN
OFF_V = D_ATTN + D_KV
OFF_GA = D_ATTN + 2 * D_KV
OFF_XL = OFF_GA + D_ATTN
OFF_GL = OFF_XL + D_LRU

SUBLANES = 8
LANES = 128
MXU_DIM = 256
VMEM_LIMIT_BYTES = 56 * 1024 * 1024

GROUP_W = GQA_GROUP * HEAD_DIM
KEY_PAD = 2 * LANES
V_EXT_W = 2 * LANES
GATE_TILE = MXU_DIM
N_GATE_TILES = D_LRU // GATE_TILE
ROW_TILE = 256
OUT_ROW_TILE = 1024
UNITS_PER_LATE_PIECE = 4
SCORE_LOOKAHEAD = 2
SCORE_RING = SCORE_LOOKAHEAD + 2
N_TILE = 512
W_OUT_SLAB_STEPS = 4
N_CHUNKS_PER_TILE = ROW_TILE // CHUNK
KBUF_ROWS = WINDOW + ROW_TILE + (KEY_PAD - (WINDOW + CHUNK))

_SLOPES = tuple(2.0 ** (-8.0 * h / N_Q_HEADS) for h in range(1, N_Q_HEADS + 1))


def _resident(shape):
    return pl.BlockSpec(shape, lambda *_: (0,) * len(shape), pipeline_mode=pl.Buffered(1))


class _ColGroups:
    def __init__(self, groups):
        self._groups = groups

    def _locate(self, idx):
        rows, cols = idx
        for first, end, ref in self._groups:
            if first <= cols.start and cols.stop <= end:
                return ref, rows, slice(cols.start - first, cols.stop - first)
        raise ValueError(f"columns {cols} straddle projection groups")

    def __getitem__(self, idx):
        ref, rows, cols = self._locate(idx)
        return ref[rows, cols]

    def __setitem__(self, idx, value):
        ref, rows, cols = self._locate(idx)
        ref[rows, cols] = value


_ALL_ROWS = slice(None)
_PROJ_GROUPS = ((OFF_Q, OFF_K), (OFF_K, OFF_GA), (OFF_GA, OFF_XL), (OFF_XL, OFF_GL), (OFF_GL, D_IN))


def _sigmoid(x):
    return 1.0 / (1.0 + jnp.exp(-x))


def _in_proj_into(xb, w_ref, b_ref, z, first=0, end=D_IN, width=N_TILE):
    for n in range(first, end, width):
        z[_ALL_ROWS, n:n + width] = (
            jnp.dot(xb, w_ref[:, n:n + width], preferred_element_type=F32) + b_ref[:, n:n + width])


def _bias_table(nq, kv, key0_pos, q0_pos, n_keys):
    rows = GQA_GROUP * nq
    r = lax.broadcasted_iota(jnp.int32, (rows, KEY_PAD), 0)
    c = lax.broadcasted_iota(jnp.int32, (rows, KEY_PAD), 1)
    shift = nq.bit_length() - 1
    g = jnp.right_shift(r, shift)
    qpos = q0_pos + (r & (nq - 1))
    kpos = key0_pos + c
    slope = jnp.full((rows, KEY_PAD), _SLOPES[kv * GQA_GROUP], F32)
    for gg in range(1, GQA_GROUP):
        slope = jnp.where(g == gg, _SLOPES[kv * GQA_GROUP + gg], slope)
    dist = jnp.abs(qpos - kpos).astype(F32)
    cshift = CHUNK.bit_length() - 1
    qc = jnp.right_shift(qpos, cshift)
    kc = jnp.right_shift(kpos, cshift)
    valid = (kpos >= 0) & (kc <= qc) & (kc >= qc - WINDOW_CHUNKS) & (c < n_keys)
    return jnp.where(valid, -(slope * dist), NEG_INF)


def _tile_heads(x, kv, copies):
    pair = x[:, (kv // 2) * LANES:(kv // 2 + 1) * LANES]
    rolled = pltpu.roll(pair, HEAD_DIM, axis=1)
    lane = lax.broadcasted_iota(jnp.int32, pair.shape, 1)
    low = lane < HEAD_DIM
    both = jnp.where(low, pair, rolled) if kv % 2 == 0 else jnp.where(low, rolled, pair)
    both = both.astype(BF16)
    return jnp.concatenate([both] * (copies * HEAD_DIM // LANES), axis=1)


def _sink_table(nq, kv, sinks_ref):
    return jnp.concatenate(
        [jnp.full((nq, LANES), sinks_ref[kv * GQA_GROUP + g], F32) for g in range(GQA_GROUP)], axis=0)


def _head_of_lane(nq):
    lane = lax.broadcasted_iota(jnp.int32, (nq, GROUP_W), 1)
    return jnp.right_shift(lane, HEAD_DIM.bit_length() - 1)


def _attn_scores(q, k4):
    nq = q.shape[0]
    qb = (q * (HEAD_DIM ** -0.5)).astype(BF16)
    head_of_lane = _head_of_lane(nq)
    zero = jnp.zeros_like(qb)
    qs = jnp.concatenate([jnp.where(head_of_lane == g, qb, zero) for g in range(GQA_GROUP)], axis=0)
    return lax.dot_general(qs, k4, (((1,), (1,)), ((), ())), preferred_element_type=F32)


def _attn_finish(scores, v4e, bias, sink):
    nq = scores.shape[0] // GQA_GROUP
    s = scores + bias
    m = jnp.maximum(jnp.broadcast_to(jnp.max(s, axis=-1, keepdims=True), sink.shape), sink)
    e = jnp.concatenate(
        [jnp.exp(s[:, j * LANES:(j + 1) * LANES] - m) for j in range(KEY_PAD // LANES)], axis=1)
    o4e = jnp.dot(e.astype(BF16), v4e, preferred_element_type=F32)
    o2 = o4e[:, :LANES] * (1.0 / (o4e[:, LANES:] + jnp.exp(sink - m)))
    low = lax.broadcasted_iota(jnp.int32, (nq, LANES), 1) < HEAD_DIM
    return jnp.concatenate(
        [jnp.where(low, o2[g * nq:(g + 1) * nq], o2[(g + 1) * nq:(g + 2) * nq]) for g in range(0, GQA_GROUP, 2)],
        axis=1)


def _lru(z, u_out, t_rows, xbuf, a_s, b_s, hc, cw_ref, cb_ref, wg_ref, bg_ref, lam_ref, fill=None):
    fill = fill or (lambda: None)
    neg = -lam_ref[...]
    softplus = jnp.maximum(neg, 0.0) + jnp.log1p(jnp.exp(-jnp.abs(neg)))
    coef = -LRU_C * softplus
    tiles = [slice(j * GATE_TILE, (j + 1) * GATE_TILE) for j in range(N_GATE_TILES)]

    def conv_and_gate_matmul(cols, j):
        xl = z[_ALL_ROWS, OFF_XL + cols.start:OFF_XL + cols.stop]
        xbuf[SUBLANES:SUBLANES + t_rows, cols] = xl
        xc = cb_ref[:, cols] + cw_ref[CONV_WIDTH - 1:CONV_WIDTH, cols] * xl
        for k in range(CONV_WIDTH - 1):
            off = SUBLANES - (CONV_WIDTH - 1) + k
            xc = xc + cw_ref[k:k + 1, cols] * xbuf[off:off + t_rows, cols]
        pre = jnp.dot(xc.astype(BF16), wg_ref[j], preferred_element_type=F32)
        a_s[:, cols] = pre[:, :GATE_TILE]
        b_s[:, cols] = pre[:, GATE_TILE:]
        return xc

    def gate_math(cols, xc):
        r = _sigmoid(a_s[:, cols] + bg_ref[0:1, cols])
        i = _sigmoid(b_s[:, cols] + bg_ref[1:2, cols])
        a = jnp.exp(coef[:, cols] * r)
        mult = jnp.sqrt(jnp.maximum(1.0 - a * a, 0.0))
        a_s[:, cols] = a
        b_s[:, cols] = mult * (i * xc)

    fill()
    xc_prev = None
    for j, cols in enumerate(tiles):
        xc = conv_and_gate_matmul(cols, j)
        if xc_prev is not None:
            gate_math(tiles[j - 1], xc_prev)
        xc_prev = xc
        fill()
    gate_math(tiles[-1], xc_prev)

    row = lax.broadcasted_iota(jnp.int32, (SUBLANES, D_LRU), 0)
    h_prev = hc[...]
    for gi in range(t_rows // SUBLANES):
        rows = slice(gi * SUBLANES, (gi + 1) * SUBLANES)
        a8 = a_s[rows, :]
        b8 = b_s[rows, :]
        for d in (1, 2, 4):
            keep = row >= d
            a_sh = pltpu.roll(a8, d, axis=0)
            b_sh = pltpu.roll(b8, d, axis=0)
            b8 = jnp.where(keep, a8 * b_sh + b8, b8)
            a8 = jnp.where(keep, a8 * a_sh, a8)
        h8 = a8 * h_prev + b8
        b_s[rows, :] = h8
        h_prev = jnp.broadcast_to(h8[SUBLANES - 1:SUBLANES, :], (SUBLANES, D_LRU))
    hc[...] = h_prev
    gl = z[_ALL_ROWS, OFF_GL:OFF_GL + D_LRU]
    u_out[...] = (b_s[...] * (gl * _sigmoid(gl))).astype(u_out.dtype)
    xbuf[0:SUBLANES, :] = xbuf[t_rows:t_rows + SUBLANES, :]


def _prompt_kernel(sinks_ref, x_ref, win_ref, bin_ref, cw_ref, cb_ref, wg_ref, bg_ref, lam_ref, wout_ref,
                   ua_ref, ul_ref, kwin_ref, vwin_ref, conv_ref, hout_ref, wout_bf_ref,
                   zq, zkv, zga, zxl, zgl, bias_s, mask_s, sink_s, s_ring, k4_s, v4_s, xbuf, a_s, b_s, hc,
                   *, n_t, n_tiles):
    z = _ColGroups([(first, end, ref) for (first, end), ref in zip(_PROJ_GROUPS, (zq, zkv, zga, zxl, zgl))])
    ua = ua_ref.at[0]
    s = pl.program_id(0)
    t_a = lax.rem(jnp.minimum(s, n_tiles - 1), n_t)
    t_l = lax.rem(jnp.maximum(s - 1, 0), n_t)

    @pl.when(s == 0)
    def _init():
        for kv in range(N_KV_HEADS):
            bias_s[kv] = _bias_table(CHUNK, kv, 0, WINDOW, WINDOW + CHUNK)
            sink_s[kv] = _sink_table(CHUNK, kv, sinks_ref)
        col = lax.broadcasted_iota(jnp.int32, (SUBLANES, KEY_PAD), 1)
        for ci in range(WINDOW_CHUNKS):
            mask_s[ci] = jnp.where(col < WINDOW - ci * CHUNK, NEG_INF, 0.0)
        k4_s[...] = jnp.zeros_like(k4_s)
        v4_s[:, :, 0:LANES] = jnp.zeros((N_KV_HEADS, KBUF_ROWS, LANES), BF16)
        v4_s[:, :, LANES:] = jnp.ones((N_KV_HEADS, KBUF_ROWS, LANES), BF16)
        zxl[...] = jnp.zeros_like(zxl)
        zgl[...] = jnp.zeros_like(zgl)

    @pl.when(t_a == 0)
    def _reset_attention_carry():
        k4_s[:, 0:WINDOW, :] = jnp.zeros((N_KV_HEADS, WINDOW, GROUP_W), BF16)
        v4_s[:, 0:WINDOW, 0:LANES] = jnp.zeros((N_KV_HEADS, WINDOW, LANES), BF16)

    @pl.when(t_l == 0)
    def _reset_lru_carry():
        xbuf[0:SUBLANES, :] = jnp.zeros((SUBLANES, D_LRU), F32)
        hc[...] = jnp.zeros_like(hc)

    xb = x_ref[0].astype(BF16)
    early_cols = list(range(OFF_K, OFF_GA, N_TILE)) + list(range(OFF_Q, OFF_K, N_TILE)) + list(
        range(OFF_GA, OFF_XL, N_TILE))
    late_cols = list(range(OFF_XL, D_IN, N_TILE))

    def project(cols):
        if cols:
            c0 = cols.pop(0)
            _in_proj_into(xb, win_ref, bin_ref, z, c0, c0 + N_TILE)

    _lru(z, ul_ref.at[0], ROW_TILE, xbuf, a_s, b_s, hc, cw_ref, cb_ref, wg_ref, bg_ref, lam_ref,
         fill=lambda: project(early_cols))
    while early_cols:
        project(early_cols)

    k_new = z[_ALL_ROWS, OFF_K:OFF_K + D_KV]
    v_new = z[_ALL_ROWS, OFF_V:OFF_V + D_KV]
    for kv in range(N_KV_HEADS):
        k4_s[kv, WINDOW:WINDOW + ROW_TILE, :] = _tile_heads(k_new, kv, GQA_GROUP)
        v4_s[kv, WINDOW:WINDOW + ROW_TILE, 0:LANES] = _tile_heads(v_new, kv, 2)

    units = [(ci, kv) for ci in range(N_CHUNKS_PER_TILE) for kv in range(N_KV_HEADS)]

    def scores_into_ring(idx):
        ci, kv = units[idx]
        s_ring[idx % SCORE_RING] = _attn_scores(
            z[ci * CHUNK:(ci + 1) * CHUNK, OFF_Q + kv * GROUP_W:OFF_Q + (kv + 1) * GROUP_W],
            k4_s[kv, ci * CHUNK:ci * CHUNK + KEY_PAD, :])

    mask_rows = [jnp.where(t_a == 0, mask_s[ci, 0:1, :], 0.0) for ci in range(WINDOW_CHUNKS)]
    for idx in range(SCORE_LOOKAHEAD):
        scores_into_ring(idx)
    for idx, (ci, kv) in enumerate(units):
        if idx + SCORE_LOOKAHEAD < len(units):
            scores_into_ring(idx + SCORE_LOOKAHEAD)
        if idx % UNITS_PER_LATE_PIECE == 1:
            project(late_cols)
        s_cur = s_ring[idx % SCORE_RING]
        rows = slice(ci * CHUNK, (ci + 1) * CHUNK)
        bias = bias_s[kv] + mask_rows[ci] if ci < WINDOW_CHUNKS else bias_s[kv]
        o = _attn_finish(s_cur, v4_s[kv, ci * CHUNK:ci * CHUNK + KEY_PAD, :], bias, sink_s[kv])
        ga = z[rows, OFF_GA + kv * GROUP_W:OFF_GA + (kv + 1) * GROUP_W]
        ua[rows, kv * GROUP_W:(kv + 1) * GROUP_W] = (o * (ga * _sigmoid(ga))).astype(ua.dtype)
    assert not early_cols and not late_cols

    @pl.when(t_l == n_t - 1)
    def _emit_lru_state():
        seq_l = lax.div(jnp.maximum(s - 1, 0), n_t)
        for j in range(CONV_WIDTH - 1):
            row = SUBLANES - (CONV_WIDTH - 1) + j
            conv_ref[j, pl.ds(seq_l, 1), :] = xbuf[row:row + 1, :]
        hout_ref[pl.ds(seq_l, 1), :] = hc[0:1, :]

    @pl.when(t_a == n_t - 1)
    def _emit_kv_window():
        kwin_ref[0] = z[slice(ROW_TILE - WINDOW, ROW_TILE), OFF_K:OFF_K + D_KV].T
        vwin_ref[0] = z[slice(ROW_TILE - WINDOW, ROW_TILE), OFF_V:OFF_V + D_KV].T

    @pl.when(s < n_tiles - 1)
    def _carry_kv():
        k4_s[:, 0:WINDOW, :] = k4_s[:, ROW_TILE:ROW_TILE + WINDOW, :]
        v4_s[:, 0:WINDOW, 0:LANES] = v4_s[:, ROW_TILE:ROW_TILE + WINDOW, 0:LANES]

    @pl.when(lax.rem(s, W_OUT_SLAB_STEPS) == 0)
    def _round_w_out_slab():
        wout_bf_ref[...] = wout_ref[...].astype(BF16)


def _prompt_mixer(x3, sinks, w_in_bf, b_in_row, cw, cb, wg, bg, lam, w_out):
    bsz, seq, _ = x3.shape
    n_t = seq // ROW_TILE
    n_tiles = bsz * n_t
    smem = pl.BlockSpec(memory_space=pltpu.SMEM)
    assert n_tiles % W_OUT_SLAB_STEPS == 0 and D_MODEL % (n_tiles // W_OUT_SLAB_STEPS) == 0
    slab_rows = D_MODEL // (n_tiles // W_OUT_SLAB_STEPS)
    w_out_slab = pl.BlockSpec((slab_rows, D_MODEL), lambda s: (jnp.minimum(s, n_tiles - 1) // W_OUT_SLAB_STEPS, 0))

    def attn_tile(s):
        ta = jnp.minimum(s, n_tiles - 1)
        return ta // n_t, ta % n_t

    def lru_tile(s):
        tl = jnp.maximum(s - 1, 0)
        return tl // n_t, tl % n_t

    return pl.pallas_call(
        functools.partial(_prompt_kernel, n_t=n_t, n_tiles=n_tiles),
        grid=(n_tiles + 1,),
        in_specs=[smem,
                  pl.BlockSpec((1, ROW_TILE, D_MODEL), lambda s: (*attn_tile(s), 0)),
                  _resident((D_MODEL, D_IN)), _resident((1, D_IN)),
                  _resident((CONV_WIDTH, D_LRU)), _resident((1, D_LRU)),
                  _resident((N_GATE_TILES, GATE_TILE, 2 * GATE_TILE)),
                  _resident((2, D_LRU)), _resident((1, D_LRU)), w_out_slab],
        out_specs=[pl.BlockSpec((1, ROW_TILE, D_ATTN), lambda s: (*attn_tile(s), 0)),
                   pl.BlockSpec((1, ROW_TILE, D_LRU), lambda s: (*lru_tile(s), 0)),
                   pl.BlockSpec((1, D_KV, WINDOW), lambda s: (attn_tile(s)[0], 0, 0)),
                   pl.BlockSpec((1, D_KV, WINDOW), lambda s: (attn_tile(s)[0], 0, 0)),
                   pl.BlockSpec((CONV_WIDTH - 1, bsz, D_LRU), lambda s: (0, 0, 0)),
                   pl.BlockSpec((bsz, D_LRU), lambda s: (0, 0)), w_out_slab],
        out_shape=[jax.ShapeDtypeStruct((bsz, seq, D_ATTN), BF16),
                   jax.ShapeDtypeStruct((bsz, seq, D_LRU), BF16),
                   jax.ShapeDtypeStruct((bsz, D_KV, WINDOW), F32),
                   jax.ShapeDtypeStruct((bsz, D_KV, WINDOW), F32),
                   jax.ShapeDtypeStruct((CONV_WIDTH - 1, bsz, D_LRU), F32),
                   jax.ShapeDtypeStruct((bsz, D_LRU), F32),
                   jax.ShapeDtypeStruct((D_MODEL, D_MODEL), BF16)],
        scratch_shapes=[pltpu.VMEM((ROW_TILE, end - first), F32) for first, end in _PROJ_GROUPS] + [
                        pltpu.VMEM((N_KV_HEADS, GQA_GROUP * CHUNK, KEY_PAD), F32),
                        pltpu.VMEM((WINDOW_CHUNKS, SUBLANES, KEY_PAD), F32),
                        pltpu.VMEM((N_KV_HEADS, GQA_GROUP * CHUNK, LANES), F32),
                        pltpu.VMEM((SCORE_RING, GQA_GROUP * CHUNK, KEY_PAD), F32),
                        pltpu.VMEM((N_KV_HEADS, KBUF_ROWS, GROUP_W), BF16),
                        pltpu.VMEM((N_KV_HEADS, KBUF_ROWS, V_EXT_W), BF16),
                        pltpu.VMEM((SUBLANES + ROW_TILE, D_LRU), F32),
                        pltpu.VMEM((ROW_TILE, D_LRU), F32),
                        pltpu.VMEM((ROW_TILE, D_LRU), F32),
                        pltpu.VMEM((SUBLANES, D_LRU), F32)],
        compiler_params=pltpu.CompilerParams(
            dimension_semantics=("arbitrary",), vmem_limit_bytes=VMEM_LIMIT_BYTES),
        name="prompt_mixer",
    )(sinks, x3, w_in_bf, b_in_row, cw, cb, wg, bg, lam, w_out)


def _out_proj_kernel(ua_ref, ul_ref, x_ref, w_ref, g_ref, b_ref, y_ref):
    tm = y_ref.shape[0]
    part = min(MXU_DIM, tm)
    n_parts = tm // part
    n_chunks = D_MODEL // N_TILE
    ln_rows = part // n_chunks

    def matmul_chunk(p, n):
        rows = slice(p * part, (p + 1) * part)
        cols = slice(n * N_TILE, (n + 1) * N_TILE)
        y_ref[rows, cols] = (
            DEEPNORM_ALPHA * x_ref[rows, cols]
            + jnp.dot(ua_ref[rows, :], w_ref[0:D_ATTN, cols], preferred_element_type=F32)
            + jnp.dot(ul_ref[rows, :], w_ref[D_ATTN:, cols], preferred_element_type=F32))

    def layer_norm_rows(p, r):
        rows = slice(p * part + r * ln_rows, p * part + (r + 1) * ln_rows)
        y = y_ref[rows, :]
        mu = jnp.mean(y, axis=-1, keepdims=True)
        yc = y - mu
        var = jnp.mean(yc * yc, axis=-1, keepdims=True)
        y_ref[rows, :] = yc * lax.rsqrt(var + LN_EPS) * g_ref[...] + b_ref[...]

    for n in range(n_chunks):
        matmul_chunk(0, n)
    for p in range(1, n_parts):
        for n in range(n_chunks):
            matmul_chunk(p, n)
            layer_norm_rows(p - 1, n)
    for r in range(n_chunks):
        layer_norm_rows(n_parts - 1, r)


def _out_proj(ua2d, ul2d, x2d, w_bf, g_row, b_row):
    m = x2d.shape[0]
    tm = min(OUT_ROW_TILE, m // 2)
    return pl.pallas_call(
        _out_proj_kernel,
        grid=(m // tm,),
        in_specs=[pl.BlockSpec((tm, D_ATTN), lambda i: (i, 0)),
                  pl.BlockSpec((tm, D_LRU), lambda i: (i, 0)),
                  pl.BlockSpec((tm, D_MODEL), lambda i: (i, 0)),
                  _resident((D_MODEL, D_MODEL)),
                  _resident((1, D_MODEL)),
                  _resident((1, D_MODEL))],
        out_specs=pl.BlockSpec((tm, D_MODEL), lambda i: (i, 0)),
        out_shape=jax.ShapeDtypeStruct((m, D_MODEL), F32),
        compiler_params=pltpu.CompilerParams(
            dimension_semantics=("arbitrary",), vmem_limit_bytes=VMEM_LIMIT_BYTES),
        name="out_proj",
    )(ua2d, ul2d, x2d, w_bf, g_row, b_row)


def _in_proj_kernel(x_ref, w_ref, b_ref, z_ref, w_bf_ref):
    @pl.when(pl.program_id(0) == 0)
    def _start_from_bias():
        z_ref[...] = jnp.broadcast_to(b_ref[...], z_ref.shape)

    xb = x_ref[...].astype(BF16)
    for n in range(0, D_IN, N_TILE):
        wb = w_ref[:, n:n + N_TILE].astype(BF16)
        w_bf_ref[:, n:n + N_TILE] = wb
        z_ref[:, n:n + N_TILE] += jnp.dot(xb, wb, preferred_element_type=F32)


def _in_proj(x2d, w, b_row):
    m = x2d.shape[0]
    return pl.pallas_call(
        _in_proj_kernel,
        grid=(D_MODEL // MXU_DIM,),
        in_specs=[pl.BlockSpec((m, MXU_DIM), lambda k: (0, k)),
                  pl.BlockSpec((MXU_DIM, D_IN), lambda k: (k, 0)),
                  pl.BlockSpec((1, D_IN), lambda k: (0, 0))],
        out_specs=[pl.BlockSpec((m, D_IN), lambda k: (0, 0)),
                   pl.BlockSpec((MXU_DIM, D_IN), lambda k: (k, 0))],
        out_shape=[jax.ShapeDtypeStruct((m, D_IN), F32),
                   jax.ShapeDtypeStruct((D_MODEL, D_IN), BF16)],
        compiler_params=pltpu.CompilerParams(
            dimension_semantics=("arbitrary",), vmem_limit_bytes=VMEM_LIMIT_BYTES),
        name="in_proj",
    )(x2d, w, b_row)


def _sample_mixer_kernel(sinks_ref, z_ref, ck_ref, cv_ref, sc_ref, sh_ref,
                         cw_ref, cb_ref, wg_ref, bg_ref, lam_ref,
                         ua_ref, ul_ref, kwin_ref, vwin_ref, conv_ref, hout_ref,
                         bias_s, sink_s, s_ring, k4_s, v4_s, xbuf, a_s, b_s, hc, *, t_rows, n_cache):
    b = pl.program_id(0)
    n_keys = n_cache + t_rows
    z = _ColGroups([(0, D_IN, z_ref.at[0])])
    ua = ua_ref.at[0]

    @pl.when(b == 0)
    def _init_tables():
        for kv in range(N_KV_HEADS):
            bias_s[kv] = _bias_table(t_rows, kv, PAST_LEN - n_cache, PAST_LEN, n_keys)
            sink_s[kv] = _sink_table(t_rows, kv, sinks_ref)
        k4_s[...] = jnp.zeros_like(k4_s)
        v4_s[:, :, 0:LANES] = jnp.zeros((N_KV_HEADS, KEY_PAD, LANES), BF16)
        v4_s[:, :, LANES:] = jnp.ones((N_KV_HEADS, KEY_PAD, LANES), BF16)

    k_all = jnp.concatenate([ck_ref[0], z[_ALL_ROWS, OFF_K:OFF_K + D_KV]], axis=0)
    v_all = jnp.concatenate([cv_ref[0], z[_ALL_ROWS, OFF_V:OFF_V + D_KV]], axis=0)
    for kv in range(N_KV_HEADS):
        k4_s[kv, 0:n_keys, :] = _tile_heads(k_all, kv, GQA_GROUP)
        v4_s[kv, 0:n_keys, 0:LANES] = _tile_heads(v_all, kv, 2)
    for kv in range(N_KV_HEADS):
        s_ring[kv] = _attn_scores(z[_ALL_ROWS, OFF_Q + kv * GROUP_W:OFF_Q + (kv + 1) * GROUP_W], k4_s[kv])
    for kv in range(N_KV_HEADS):
        o = _attn_finish(s_ring[kv], v4_s[kv], bias_s[kv], sink_s[kv])
        ga = z[_ALL_ROWS, OFF_GA + kv * GROUP_W:OFF_GA + (kv + 1) * GROUP_W]
        ua[:, kv * GROUP_W:(kv + 1) * GROUP_W] = (o * (ga * _sigmoid(ga))).astype(ua.dtype)
    kwin_ref[0] = k_all[n_keys - WINDOW:, :]
    vwin_ref[0] = v_all[n_keys - WINDOW:, :]

    xbuf[0:SUBLANES, :] = jnp.zeros((SUBLANES, D_LRU), F32)
    for j in range(CONV_WIDTH - 1):
        row = SUBLANES - (CONV_WIDTH - 1) + j
        xbuf[row:row + 1, :] = sc_ref[j, pl.ds(b, 1), :]
    hc[...] = jnp.broadcast_to(sh_ref[pl.ds(b, 1), :], (SUBLANES, D_LRU))
    _lru(z, ul_ref.at[0], t_rows, xbuf, a_s, b_s, hc, cw_ref, cb_ref, wg_ref, bg_ref, lam_ref)
    for j in range(CONV_WIDTH - 1):
        row = SUBLANES - (CONV_WIDTH - 1) + j
        conv_ref[j, pl.ds(b, 1), :] = xbuf[row:row + 1, :]
    hout_ref[pl.ds(b, 1), :] = hc[0:1, :]


def _sample_mixer(z3, cache_k, cache_v, state_conv, state_h, sinks, cw, cb, wg, bg, lam):
    bsz, t_rows, _ = z3.shape
    n_cache = cache_k.shape[1]
    assert n_cache + t_rows <= KEY_PAD and n_cache + t_rows >= WINDOW
    assert t_rows % SUBLANES == 0 and t_rows & (t_rows - 1) == 0 and t_rows >= CONV_WIDTH - 1
    smem = pl.BlockSpec(memory_space=pltpu.SMEM)
    per_b = lambda shape: pl.BlockSpec((1,) + shape, lambda b: (b, 0, 0))
    return pl.pallas_call(
        functools.partial(_sample_mixer_kernel, t_rows=t_rows, n_cache=n_cache),
        grid=(bsz,),
        in_specs=[smem, per_b((t_rows, D_IN)), per_b((n_cache, D_KV)), per_b((n_cache, D_KV)),
                  _resident((CONV_WIDTH - 1, bsz, D_LRU)), _resident((bsz, D_LRU)),
                  _resident((CONV_WIDTH, D_LRU)), _resident((1, D_LRU)),
                  _resident((N_GATE_TILES, GATE_TILE, 2 * GATE_TILE)),
                  _resident((2, D_LRU)), _resident((1, D_LRU))],
        out_specs=[per_b((t_rows, D_ATTN)), per_b((t_rows, D_LRU)), per_b((WINDOW, D_KV)), per_b((WINDOW, D_KV)),
                   pl.BlockSpec((CONV_WIDTH - 1, bsz, D_LRU), lambda b: (0, 0, 0)),
                   pl.BlockSpec((bsz, D_LRU), lambda b: (0, 0))],
        out_shape=[jax.ShapeDtypeStruct((bsz, t_rows, D_ATTN), BF16),
                   jax.ShapeDtypeStruct((bsz, t_rows, D_LRU), BF16),
                   jax.ShapeDtypeStruct((bsz, WINDOW, D_KV), F32),
                   jax.ShapeDtypeStruct((bsz, WINDOW, D_KV), F32),
                   jax.ShapeDtypeStruct((CONV_WIDTH - 1, bsz, D_LRU), F32),
                   jax.ShapeDtypeStruct((bsz, D_LRU), F32)],
        scratch_shapes=[pltpu.VMEM((N_KV_HEADS, GQA_GROUP * t_rows, KEY_PAD), F32),
                        pltpu.VMEM((N_KV_HEADS, GQA_GROUP * t_rows, LANES), F32),
                        pltpu.VMEM((N_KV_HEADS, GQA_GROUP * t_rows, KEY_PAD), F32),
                        pltpu.VMEM((N_KV_HEADS, KEY_PAD, GROUP_W), BF16),
                        pltpu.VMEM((N_KV_HEADS, KEY_PAD, V_EXT_W), BF16),
                        pltpu.VMEM((SUBLANES + t_rows, D_LRU), F32),
                        pltpu.VMEM((t_rows, D_LRU), F32),
                        pltpu.VMEM((t_rows, D_LRU), F32),
                        pltpu.VMEM((SUBLANES, D_LRU), F32)],
        compiler_params=pltpu.CompilerParams(
            dimension_semantics=("arbitrary",), vmem_limit_bytes=VMEM_LIMIT_BYTES),
        name="sample_mixer",
    )(sinks, z3, cache_k, cache_v, state_conv, state_h, cw, cb, wg, bg, lam)


def _gate_weight_tiles(w_a, w_x):
    per = GATE_TILE // LRU_BLOCK
    row_block = lax.broadcasted_iota(jnp.int32, (GATE_TILE, GATE_TILE), 0) // LRU_BLOCK
    col_block = lax.broadcasted_iota(jnp.int32, (GATE_TILE, GATE_TILE), 1) // LRU_BLOCK

    def block_diagonal(w):
        rows = w.reshape(N_GATE_TILES, GATE_TILE, LRU_BLOCK)
        return jnp.where(row_block == col_block, jnp.concatenate([rows] * per, axis=-1), 0.0)

    return jnp.concatenate([block_diagonal(w_a), block_diagonal(w_x)], axis=-1).astype(BF16)


def kernel(x_prompt, x_sample, cache_k, cache_v, state_conv, state_h, w_in, b_in, conv_w, conv_b,
           w_gate_a, b_gate_a, w_gate_x, b_gate_x, lru_lambda, attn_sinks, w_out, ln_g, ln_b):
    assert w_in.shape[0] == DEPTH == 1
    bsz, seq, _ = x_prompt.shape
    dbsz, dseq, _ = x_sample.shape
    n_cache = cache_k.shape[2]
    n_p, n_s = bsz * seq, dbsz * dseq
    assert seq % ROW_TILE == 0 and n_p % OUT_ROW_TILE == 0

    wg = _gate_weight_tiles(w_gate_a[0], w_gate_x[0])
    bg = jnp.concatenate([b_gate_a[0], b_gate_x[0]], axis=0).reshape(2, D_LRU)
    cw = conv_w[0]
    sinks = attn_sinks[0]

    xs2 = x_sample.reshape(n_s, D_MODEL)
    zs, w_in_bf = _in_proj(xs2, w_in.reshape(D_MODEL, D_IN), b_in)
    zs = zs.reshape(dbsz, dseq, D_IN)

    uap, ulp, kp, vp, cp, hp, w_out_bf = _prompt_mixer(
        x_prompt, sinks, w_in_bf, b_in, cw, conv_b, wg, bg, lru_lambda, w_out.reshape(D_MODEL, D_MODEL))
    yp = _out_proj(uap.reshape(n_p, D_ATTN), ulp.reshape(n_p, D_LRU), x_prompt.reshape(n_p, D_MODEL),
                   w_out_bf, ln_g, ln_b).reshape(bsz, seq, D_MODEL)

    uas, uls, ks, vs, cs, hs = _sample_mixer(
        zs, cache_k[0].reshape(dbsz, n_cache, D_KV), cache_v[0].reshape(dbsz, n_cache, D_KV),
        state_conv[0].transpose(1, 0, 2), state_h[0],
        sinks, cw, conv_b, wg, bg, lru_lambda)
    ys = _out_proj(uas.reshape(n_s, D_ATTN), uls.reshape(n_s, D_LRU), xs2,
                   w_out_bf, ln_g, ln_b).reshape(dbsz, dseq, D_MODEL)

    def from_channel_major(w):
        return w.reshape(bsz, N_KV_HEADS, HEAD_DIM, WINDOW).transpose(0, 3, 1, 2)[None]

    def from_row_major(w):
        return w.reshape(1, dbsz, WINDOW, N_KV_HEADS, HEAD_DIM)

    return (yp, ys,
            from_channel_major(kp), from_channel_major(vp), cp.transpose(1, 0, 2)[None], hp[None],
            from_row_major(ks), from_row_major(vs), cs.transpose(1, 0, 2)[None], hs[None])
```

```python
import functools

import jax
import jax.numpy as jnp
from jax import lax
from jax.experimental import pallas as pl
from jax.experimental.pallas import tpu as pltpu

F32 = jnp.float32
BF16 = jnp.bfloat16

D_MODEL = 2048
DEPTH = 1
CHUNK = 64
WINDOW = 128
WINDOW_CHUNKS = WINDOW // CHUNK
HEAD_DIM = 64
D_ATTN = D_MODEL // 2
D_LRU = D_MODEL - D_ATTN
N_Q_HEADS = D_ATTN // HEAD_DIM
N_KV_HEADS = 4
GQA_GROUP = N_Q_HEADS // N_KV_HEADS
D_KV = N_KV_HEADS * HEAD_DIM
N_LRU_BLOCKS = 16
LRU_BLOCK = D_LRU // N_LRU_BLOCKS
CONV_WIDTH = 4
LRU_C = 8.0
D_IN = 2 * D_ATTN + 2 * D_KV + 2 * D_LRU
PAST_LEN = 2048
DEEPNORM_ALPHA = (2.0 * DEPTH) ** 0.25
LN_EPS = 1e-5
NEG_INF = -1e30

OFF_Q = 0
OFF_K = D_ATTN
OFF_V = D_ATTN + D_KV
OFF_GA = D_ATTN + 2 * D_KV
OFF_XL = OFF_GA + D_ATTN
OFF_GL = OFF_XL + D_LRU

SUBLANES = 8
LANES = 128
MXU_DIM = 256
VMEM_LIMIT_BYTES = 56 * 1024 * 1024

GROUP_W = GQA_GROUP * HEAD_DIM
KEY_PAD = 2 * LANES
V_EXT_W = 2 * LANES
GATE_TILE = MXU_DIM
N_GATE_TILES = D_LRU // GATE_TILE
ROW_TILE = 256
OUT_ROW_TILE = 1024
SCORE_LOOKAHEAD = 2
SCORE_RING = SCORE_LOOKAHEAD + 2
N_TILE = 512
K_CHUNK = 512
GATE_ROW_PARTS = 4
N_CHUNKS_PER_TILE = ROW_TILE // CHUNK
KBUF_ROWS = WINDOW + ROW_TILE + (KEY_PAD - (WINDOW + CHUNK))

_SLOPES = tuple(2.0 ** (-8.0 * h / N_Q_HEADS) for h in range(1, N_Q_HEADS + 1))


def _resident(shape):
    return pl.BlockSpec(shape, lambda *_: (0,) * len(shape), pipeline_mode=pl.Buffered(1))


class _ColGroups:
    def __init__(self, groups):
        self._groups = groups

    def _locate(self, idx):
        rows, cols = idx
        for first, end, ref in self._groups:
            if first <= cols.start and cols.stop <= end:
                return ref, rows, slice(cols.start - first, cols.stop - first)
        raise ValueError(f"columns {cols} straddle projection groups")

    def __getitem__(self, idx):
        ref, rows, cols = self._locate(idx)
        return ref[rows, cols]

    def __setitem__(self, idx, value):
        ref, rows, cols = self._locate(idx)
        ref[rows, cols] = value


_ALL_ROWS = slice(None)
_PROJ_GROUPS = ((OFF_Q, OFF_K), (OFF_K, OFF_GA), (OFF_GA, OFF_XL), (OFF_XL, OFF_GL), (OFF_GL, D_IN))


def _sigmoid(x):
    return 1.0 / (1.0 + jnp.exp(-x))


def _in_proj_substeps(col_starts):
    return [(c, k) for c in col_starts for k in range(0, D_MODEL, K_CHUNK)]


def _in_proj_substep(xb, w_ref, b_ref, z, col, k0):
    cols = slice(col, col + N_TILE)
    part = jnp.dot(xb[:, k0:k0 + K_CHUNK], w_ref[k0:k0 + K_CHUNK, cols], preferred_element_type=F32)
    z[_ALL_ROWS, cols] = part + (b_ref[:, cols] if k0 == 0 else z[_ALL_ROWS, cols])


def _bias_table(nq, kv, key0_pos, q0_pos, n_keys):
    rows = GQA_GROUP * nq
    r = lax.broadcasted_iota(jnp.int32, (rows, KEY_PAD), 0)
    c = lax.broadcasted_iota(jnp.int32, (rows, KEY_PAD), 1)
    shift = nq.bit_length() - 1
    g = jnp.right_shift(r, shift)
    qpos = q0_pos + (r & (nq - 1))
    kpos = key0_pos + c
    slope = jnp.full((rows, KEY_PAD), _SLOPES[kv * GQA_GROUP], F32)
    for gg in range(1, GQA_GROUP):
        slope = jnp.where(g == gg, _SLOPES[kv * GQA_GROUP + gg], slope)
    dist = jnp.abs(qpos - kpos).astype(F32)
    cshift = CHUNK.bit_length() - 1
    qc = jnp.right_shift(qpos, cshift)
    kc = jnp.right_shift(kpos, cshift)
    valid = (kpos >= 0) & (kc <= qc) & (kc >= qc - WINDOW_CHUNKS) & (c < n_keys)
    return jnp.where(valid, -(slope * dist), NEG_INF)


def _tile_heads(x, kv, copies):
    pair = x[:, (kv // 2) * LANES:(kv // 2 + 1) * LANES]
    rolled = pltpu.roll(pair, HEAD_DIM, axis=1)
    lane = lax.broadcasted_iota(jnp.int32, pair.shape, 1)
    low = lane < HEAD_DIM
    both = jnp.where(low, pair, rolled) if kv % 2 == 0 else jnp.where(low, rolled, pair)
    both = both.astype(BF16)
    return jnp.concatenate([both] * (copies * HEAD_DIM // LANES), axis=1)


def _sink_table(nq, kv, sinks_ref):
    return jnp.concatenate(
        [jnp.full((nq, LANES), sinks_ref[kv * GQA_GROUP + g], F32) for g in range(GQA_GROUP)], axis=0)


def _head_of_lane(nq):
    lane = lax.broadcasted_iota(jnp.int32, (nq, GROUP_W), 1)
    return jnp.right_shift(lane, HEAD_DIM.bit_length() - 1)


def _attn_scores(q, k4):
    nq = q.shape[0]
    qb = (q * (HEAD_DIM ** -0.5)).astype(BF16)
    head_of_lane = _head_of_lane(nq)
    zero = jnp.zeros_like(qb)
    qs = jnp.concatenate([jnp.where(head_of_lane == g, qb, zero) for g in range(GQA_GROUP)], axis=0)
    return lax.dot_general(qs, k4, (((1,), (1,)), ((), ())), preferred_element_type=F32)


def _attn_finish(scores, v4e, bias, sink):
    nq = scores.shape[0] // GQA_GROUP
    s = scores + bias
    m = jnp.maximum(jnp.broadcast_to(jnp.max(s, axis=-1, keepdims=True), sink.shape), sink)
    e = jnp.concatenate(
        [jnp.exp(s[:, j * LANES:(j + 1) * LANES] - m) for j in range(KEY_PAD // LANES)], axis=1)
    o4e = jnp.dot(e.astype(BF16), v4e, preferred_element_type=F32)
    o2 = o4e[:, :LANES] * (1.0 / (o4e[:, LANES:] + jnp.exp(sink - m)))
    low = lax.broadcasted_iota(jnp.int32, (nq, LANES), 1) < HEAD_DIM
    return jnp.concatenate(
        [jnp.where(low, o2[g * nq:(g + 1) * nq], o2[(g + 1) * nq:(g + 2) * nq]) for g in range(0, GQA_GROUP, 2)],
        axis=1)


def _lru(z, u_out, t_rows, xbuf, a_s, b_s, hc, cw_ref, cb_ref, wg_ref, bg_ref, lam_ref, fill=None):
    interleaved = fill is not None
    fill = fill or (lambda: None)
    neg = -lam_ref[...]
    softplus = jnp.maximum(neg, 0.0) + jnp.log1p(jnp.exp(-jnp.abs(neg)))
    coef = -LRU_C * softplus
    tiles = [slice(j * GATE_TILE, (j + 1) * GATE_TILE) for j in range(N_GATE_TILES)]

    def conv_and_gate_matmul(cols, j):
        xl = z[_ALL_ROWS, OFF_XL + cols.start:OFF_XL + cols.stop]
        xbuf[SUBLANES:SUBLANES + t_rows, cols] = xl
        xc = cb_ref[:, cols] + cw_ref[CONV_WIDTH - 1:CONV_WIDTH, cols] * xl
        for k in range(CONV_WIDTH - 1):
            off = SUBLANES - (CONV_WIDTH - 1) + k
            xc = xc + cw_ref[k:k + 1, cols] * xbuf[off:off + t_rows, cols]
        pre = jnp.dot(xc.astype(BF16), wg_ref[j], preferred_element_type=F32)
        a_s[:, cols] = pre[:, :GATE_TILE]
        b_s[:, cols] = pre[:, GATE_TILE:]
        return xc

    part_rows = t_rows // GATE_ROW_PARTS if interleaved else t_rows

    def gate_math(cols, xc):
        for r0 in range(0, t_rows, part_rows):
            rows = slice(r0, r0 + part_rows)
            r = _sigmoid(a_s[rows, cols] + bg_ref[0:1, cols])
            i = _sigmoid(b_s[rows, cols] + bg_ref[1:2, cols])
            a = jnp.exp(coef[:, cols] * r)
            mult = jnp.sqrt(jnp.maximum(1.0 - a * a, 0.0))
            a_s[rows, cols] = a
            b_s[rows, cols] = mult * (i * xc[rows])
            fill()

    fill()
    xc_prev = None
    for j, cols in enumerate(tiles):
        xc = conv_and_gate_matmul(cols, j)
        fill()
        if xc_prev is not None:
            gate_math(tiles[j - 1], xc_prev)
        xc_prev = xc
    gate_math(tiles[-1], xc_prev)

    row = lax.broadcasted_iota(jnp.int32, (SUBLANES, D_LRU), 0)
    h_prev = hc[...]
    for gi in range(t_rows // SUBLANES):
        rows = slice(gi * SUBLANES, (gi + 1) * SUBLANES)
        a8 = a_s[rows, :]
        b8 = b_s[rows, :]
        for d in (1, 2, 4):
            keep = row >= d
            a_sh = pltpu.roll(a8, d, axis=0)
            b_sh = pltpu.roll(b8, d, axis=0)
            b8 = jnp.where(keep, a8 * b_sh + b8, b8)
            a8 = jnp.where(keep, a8 * a_sh, a8)
        h8 = a8 * h_prev + b8
        b_s[rows, :] = h8
        h_prev = jnp.broadcast_to(h8[SUBLANES - 1:SUBLANES, :], (SUBLANES, D_LRU))
    hc[...] = h_prev
    gl = z[_ALL_ROWS, OFF_GL:OFF_GL + D_LRU]
    u_out[...] = (b_s[...] * (gl * _sigmoid(gl))).astype(u_out.dtype)
    xbuf[0:SUBLANES, :] = xbuf[t_rows:t_rows + SUBLANES, :]


def _prompt_kernel(sinks_ref, x_ref, win_ref, bin_ref, cw_ref, cb_ref, wg_ref, bg_ref, lam_ref, wout_ref,
                   ua_ref, ul_ref, kwin_ref, vwin_ref, conv_ref, hout_ref, wout_bf_ref,
                   zq, zkv, zga, zxl, zgl, bias_s, mask_s, sink_s, s_ring, k4_s, v4_s, xbuf, a_s, b_s, hc,
                   *, n_t, n_tiles):
    wout_bf_ref[...] = wout_ref[...].astype(BF16)
    z = _ColGroups([(first, end, ref) for (first, end), ref in zip(_PROJ_GROUPS, (zq, zkv, zga, zxl, zgl))])
    ua = ua_ref.at[0]
    s = pl.program_id(0)
    t_a = lax.rem(jnp.minimum(s, n_tiles - 1), n_t)
    t_l = lax.rem(jnp.maximum(s - 1, 0), n_t)

    @pl.when(s == 0)
    def _init():
        for kv in range(N_KV_HEADS):
            bias_s[kv] = _bias_table(CHUNK, kv, 0, WINDOW, WINDOW + CHUNK)
            sink_s[kv] = _sink_table(CHUNK, kv, sinks_ref)
        col = lax.broadcasted_iota(jnp.int32, (SUBLANES, KEY_PAD), 1)
        for ci in range(WINDOW_CHUNKS):
            mask_s[ci] = jnp.where(col < WINDOW - ci * CHUNK, NEG_INF, 0.0)
        k4_s[...] = jnp.zeros_like(k4_s)
        v4_s[:, :, 0:LANES] = jnp.zeros((N_KV_HEADS, KBUF_ROWS, LANES), BF16)
        v4_s[:, :, LANES:] = jnp.ones((N_KV_HEADS, KBUF_ROWS, LANES), BF16)
        zxl[...] = jnp.zeros_like(zxl)
        zgl[...] = jnp.zeros_like(zgl)

    @pl.when(t_a == 0)
    def _reset_attention_carry():
        k4_s[:, 0:WINDOW, :] = jnp.zeros((N_KV_HEADS, WINDOW, GROUP_W), BF16)
        v4_s[:, 0:WINDOW, 0:LANES] = jnp.zeros((N_KV_HEADS, WINDOW, LANES), BF16)

    @pl.when(t_l == 0)
    def _reset_lru_carry():
        xbuf[0:SUBLANES, :] = jnp.zeros((SUBLANES, D_LRU), F32)
        hc[...] = jnp.zeros_like(hc)

    xb = x_ref[0].astype(BF16)
    early_cols = _in_proj_substeps(list(range(OFF_K, OFF_GA, N_TILE)) + list(range(OFF_Q, OFF_K, N_TILE)) + list(
        range(OFF_GA, OFF_XL, N_TILE)))
    late_cols = _in_proj_substeps(range(OFF_XL, D_IN, N_TILE))

    def project(steps):
        if steps:
            _in_proj_substep(xb, win_ref, bin_ref, z, *steps.pop(0))

    _lru(z, ul_ref.at[0], ROW_TILE, xbuf, a_s, b_s, hc, cw_ref, cb_ref, wg_ref, bg_ref, lam_ref,
         fill=lambda: project(early_cols))
    while early_cols:
        project(early_cols)

    k_new = z[_ALL_ROWS, OFF_K:OFF_K + D_KV]
    v_new = z[_ALL_ROWS, OFF_V:OFF_V + D_KV]
    for kv in range(N_KV_HEADS):
        k4_s[kv, WINDOW:WINDOW + ROW_TILE, :] = _tile_heads(k_new, kv, GQA_GROUP)
        v4_s[kv, WINDOW:WINDOW + ROW_TILE, 0:LANES] = _tile_heads(v_new, kv, 2)

    units = [(ci, kv) for ci in range(N_CHUNKS_PER_TILE) for kv in range(N_KV_HEADS)]

    def scores_into_ring(idx):
        ci, kv = units[idx]
        s_ring[idx % SCORE_RING] = _attn_scores(
            z[ci * CHUNK:(ci + 1) * CHUNK, OFF_Q + kv * GROUP_W:OFF_Q + (kv + 1) * GROUP_W],
            k4_s[kv, ci * CHUNK:ci * CHUNK + KEY_PAD, :])

    mask_rows = [jnp.where(t_a == 0, mask_s[ci, 0:1, :], 0.0) for ci in range(WINDOW_CHUNKS)]
    late_per_unit = -(-len(late_cols) // len(units))
    for idx in range(SCORE_LOOKAHEAD):
        scores_into_ring(idx)
    for idx, (ci, kv) in enumerate(units):
        if idx + SCORE_LOOKAHEAD < len(units):
            scores_into_ring(idx + SCORE_LOOKAHEAD)
        for _ in range(late_per_unit):
            project(late_cols)
        s_cur = s_ring[idx % SCORE_RING]
        rows = slice(ci * CHUNK, (ci + 1) * CHUNK)
        bias = bias_s[kv] + mask_rows[ci] if ci < WINDOW_CHUNKS else bias_s[kv]
        o = _attn_finish(s_cur, v4_s[kv, ci * CHUNK:ci * CHUNK + KEY_PAD, :], bias, sink_s[kv])
        ga = z[rows, OFF_GA + kv * GROUP_W:OFF_GA + (kv + 1) * GROUP_W]
        ua[rows, kv * GROUP_W:(kv + 1) * GROUP_W] = (o * (ga * _sigmoid(ga))).astype(ua.dtype)
    assert not early_cols and not late_cols

    @pl.when(t_l == n_t - 1)
    def _emit_lru_state():
        seq_l = lax.div(jnp.maximum(s - 1, 0), n_t)
        for j in range(CONV_WIDTH - 1):
            row = SUBLANES - (CONV_WIDTH - 1) + j
            conv_ref[j, pl.ds(seq_l, 1), :] = xbuf[row:row + 1, :]
        hout_ref[pl.ds(seq_l, 1), :] = hc[0:1, :]

    @pl.when(t_a == n_t - 1)
    def _emit_kv_window():
        kwin_ref[0] = z[slice(ROW_TILE - WINDOW, ROW_TILE), OFF_K:OFF_K + D_KV].T
        vwin_ref[0] = z[slice(ROW_TILE - WINDOW, ROW_TILE), OFF_V:OFF_V + D_KV].T

    @pl.when(s < n_tiles - 1)
    def _carry_kv():
        k4_s[:, 0:WINDOW, :] = k4_s[:, ROW_TILE:ROW_TILE + WINDOW, :]
        v4_s[:, 0:WINDOW, 0:LANES] = v4_s[:, ROW_TILE:ROW_TILE + WINDOW, 0:LANES]


def _prompt_mixer(x3, sinks, w_in_bf, b_in_row, cw, cb, wg, bg, lam, w_out):
    bsz, seq, _ = x3.shape
    n_t = seq // ROW_TILE
    n_tiles = bsz * n_t
    smem = pl.BlockSpec(memory_space=pltpu.SMEM)
    assert D_MODEL % n_tiles == 0 and (D_MODEL // n_tiles) % (2 * SUBLANES) == 0
    w_out_slab = pl.BlockSpec((D_MODEL // n_tiles, D_MODEL), lambda s: (jnp.minimum(s, n_tiles - 1), 0))

    def attn_tile(s):
        ta = jnp.minimum(s, n_tiles - 1)
        return ta // n_t, ta % n_t

    def lru_tile(s):
        tl = jnp.maximum(s - 1, 0)
        return tl // n_t, tl % n_t

    return pl.pallas_call(
        functools.partial(_prompt_kernel, n_t=n_t, n_tiles=n_tiles),
        grid=(n_tiles + 1,),
        in_specs=[smem,
                  pl.BlockSpec((1, ROW_TILE, D_MODEL), lambda s: (*attn_tile(s), 0)),
                  _resident((D_MODEL, D_IN)), _resident((1, D_IN)),
                  _resident((CONV_WIDTH, D_LRU)), _resident((1, D_LRU)),
                  _resident((N_GATE_TILES, GATE_TILE, 2 * GATE_TILE)),
                  _resident((2, D_LRU)), _resident((1, D_LRU)), w_out_slab],
        out_specs=[pl.BlockSpec((1, ROW_TILE, D_ATTN), lambda s: (*attn_tile(s), 0)),
                   pl.BlockSpec((1, ROW_TILE, D_LRU), lambda s: (*lru_tile(s), 0)),
                   pl.BlockSpec((1, D_KV, WINDOW), lambda s: (attn_tile(s)[0], 0, 0)),
                   pl.BlockSpec((1, D_KV, WINDOW), lambda s: (attn_tile(s)[0], 0, 0)),
                   pl.BlockSpec((CONV_WIDTH - 1, bsz, D_LRU), lambda s: (0, 0, 0)),
                   pl.BlockSpec((bsz, D_LRU), lambda s: (0, 0)), w_out_slab],
        out_shape=[jax.ShapeDtypeStruct((bsz, seq, D_ATTN), BF16),
                   jax.ShapeDtypeStruct((bsz, seq, D_LRU), BF16),
                   jax.ShapeDtypeStruct((bsz, D_KV, WINDOW), F32),
                   jax.ShapeDtypeStruct((bsz, D_KV, WINDOW), F32),
                   jax.ShapeDtypeStruct((CONV_WIDTH - 1, bsz, D_LRU), F32),
                   jax.ShapeDtypeStruct((bsz, D_LRU), F32),
                   jax.ShapeDtypeStruct((D_MODEL, D_MODEL), BF16)],
        scratch_shapes=[pltpu.VMEM((ROW_TILE, end - first), F32) for first, end in _PROJ_GROUPS] + [
                        pltpu.VMEM((N_KV_HEADS, GQA_GROUP * CHUNK, KEY_PAD), F32),
                        pltpu.VMEM((WINDOW_CHUNKS, SUBLANES, KEY_PAD), F32),
                        pltpu.VMEM((N_KV_HEADS, GQA_GROUP * CHUNK, LANES), F32),
                        pltpu.VMEM((SCORE_RING, GQA_GROUP * CHUNK, KEY_PAD), F32),
                        pltpu.VMEM((N_KV_HEADS, KBUF_ROWS, GROUP_W), BF16),
                        pltpu.VMEM((N_KV_HEADS, KBUF_ROWS, V_EXT_W), BF16),
                        pltpu.VMEM((SUBLANES + ROW_TILE, D_LRU), F32),
                        pltpu.VMEM((ROW_TILE, D_LRU), F32),
                        pltpu.VMEM((ROW_TILE, D_LRU), F32),
                        pltpu.VMEM((SUBLANES, D_LRU), F32)],
        compiler_params=pltpu.CompilerParams(
            dimension_semantics=("arbitrary",), vmem_limit_bytes=VMEM_LIMIT_BYTES),
        name="prompt_mixer",
    )(sinks, x3, w_in_bf, b_in_row, cw, cb, wg, bg, lam, w_out)


def _out_proj_kernel(ua_ref, ul_ref, x_ref, w_ref, g_ref, b_ref, y_ref):
    tm = y_ref.shape[0]
    part = min(MXU_DIM, tm)
    n_parts = tm // part
    n_chunks = D_MODEL // N_TILE
    ln_rows = part // n_chunks

    def matmul_chunk(p, n):
        rows = slice(p * part, (p + 1) * part)
        cols = slice(n * N_TILE, (n + 1) * N_TILE)
        y_ref[rows, cols] = (
            DEEPNORM_ALPHA * x_ref[rows, cols]
            + jnp.dot(ua_ref[rows, :], w_ref[0:D_ATTN, cols], preferred_element_type=F32)
            + jnp.dot(ul_ref[rows, :], w_ref[D_ATTN:, cols], preferred_element_type=F32))

    def layer_norm_rows(p, r):
        rows = slice(p * part + r * ln_rows, p * part + (r + 1) * ln_rows)
        y = y_ref[rows, :]
        mu = jnp.mean(y, axis=-1, keepdims=True)
        yc = y - mu
        var = jnp.mean(yc * yc, axis=-1, keepdims=True)
        y_ref[rows, :] = yc * lax.rsqrt(var + LN_EPS) * g_ref[...] + b_ref[...]

    for n in range(n_chunks):
        matmul_chunk(0, n)
    for p in range(1, n_parts):
        for n in range(n_chunks):
            matmul_chunk(p, n)
            layer_norm_rows(p - 1, n)
    for r in range(n_chunks):
        layer_norm_rows(n_parts - 1, r)


def _out_proj(ua2d, ul2d, x2d, w_bf, g_row, b_row):
    m = x2d.shape[0]
    tm = min(OUT_ROW_TILE, m)
    return pl.pallas_call(
        _out_proj_kernel,
        grid=(m // tm,),
        in_specs=[pl.BlockSpec((tm, D_ATTN), lambda i: (i, 0)),
                  pl.BlockSpec((tm, D_LRU), lambda i: (i, 0)),
                  pl.BlockSpec((tm, D_MODEL), lambda i: (i, 0)),
                  _resident((D_MODEL, D_MODEL)),
                  _resident((1, D_MODEL)),
                  _resident((1, D_MODEL))],
        out_specs=pl.BlockSpec((tm, D_MODEL), lambda i: (i, 0)),
        out_shape=jax.ShapeDtypeStruct((m, D_MODEL), F32),
        compiler_params=pltpu.CompilerParams(
            dimension_semantics=("arbitrary",), vmem_limit_bytes=VMEM_LIMIT_BYTES),
        name="out_proj",
    )(ua2d, ul2d, x2d, w_bf, g_row, b_row)


def _in_proj_kernel(x_ref, w_ref, b_ref, z_ref, w_bf_ref):
    @pl.when(pl.program_id(0) == 0)
    def _start_from_bias():
        z_ref[...] = jnp.broadcast_to(b_ref[...], z_ref.shape)

    xb = x_ref[...].astype(BF16)
    for n in range(0, D_IN, N_TILE):
        wb = w_ref[:, n:n + N_TILE].astype(BF16)
        w_bf_ref[:, n:n + N_TILE] = wb
        z_ref[:, n:n + N_TILE] += jnp.dot(xb, wb, preferred_element_type=F32)


def _in_proj(x2d, w, b_row):
    m = x2d.shape[0]
    return pl.pallas_call(
        _in_proj_kernel,
        grid=(D_MODEL // MXU_DIM,),
        in_specs=[pl.BlockSpec((m, MXU_DIM), lambda k: (0, k)),
                  pl.BlockSpec((MXU_DIM, D_IN), lambda k: (k, 0)),
                  pl.BlockSpec((1, D_IN), lambda k: (0, 0))],
        out_specs=[pl.BlockSpec((m, D_IN), lambda k: (0, 0)),
                   pl.BlockSpec((MXU_DIM, D_IN), lambda k: (k, 0))],
        out_shape=[jax.ShapeDtypeStruct((m, D_IN), F32),
                   jax.ShapeDtypeStruct((D_MODEL, D_IN), BF16)],
        compiler_params=pltpu.CompilerParams(
            dimension_semantics=("arbitrary",), vmem_limit_bytes=VMEM_LIMIT_BYTES),
        name="in_proj",
    )(x2d, w, b_row)


def _sample_mixer_kernel(sinks_ref, z_ref, ck_ref, cv_ref, sc_ref, sh_ref,
                         cw_ref, cb_ref, wg_ref, bg_ref, lam_ref,
                         ua_ref, ul_ref, kwin_ref, vwin_ref, conv_ref, hout_ref,
                         bias_s, sink_s, s_ring, k4_s, v4_s, xbuf, a_s, b_s, hc, *, t_rows, n_cache):
    b = pl.program_id(0)
    n_keys = n_cache + t_rows
    z = _ColGroups([(0, D_IN, z_ref.at[0])])
    ua = ua_ref.at[0]

    @pl.when(b == 0)
    def _init_tables():
        for kv in range(N_KV_HEADS):
            bias_s[kv] = _bias_table(t_rows, kv, PAST_LEN - n_cache, PAST_LEN, n_keys)
            sink_s[kv] = _sink_table(t_rows, kv, sinks_ref)
        k4_s[...] = jnp.zeros_like(k4_s)
        v4_s[:, :, 0:LANES] = jnp.zeros((N_KV_HEADS, KEY_PAD, LANES), BF16)
        v4_s[:, :, LANES:] = jnp.ones((N_KV_HEADS, KEY_PAD, LANES), BF16)

    k_all = jnp.concatenate([ck_ref[0], z[_ALL_ROWS, OFF_K:OFF_K + D_KV]], axis=0)
    v_all = jnp.concatenate([cv_ref[0], z[_ALL_ROWS, OFF_V:OFF_V + D_KV]], axis=0)
    for kv in range(N_KV_HEADS):
        k4_s[kv, 0:n_keys, :] = _tile_heads(k_all, kv, GQA_GROUP)
        v4_s[kv, 0:n_keys, 0:LANES] = _tile_heads(v_all, kv, 2)
    for kv in range(N_KV_HEADS):
        s_ring[kv] = _attn_scores(z[_ALL_ROWS, OFF_Q + kv * GROUP_W:OFF_Q + (kv + 1) * GROUP_W], k4_s[kv])
    for kv in range(N_KV_HEADS):
        o = _attn_finish(s_ring[kv], v4_s[kv], bias_s[kv], sink_s[kv])
        ga = z[_ALL_ROWS, OFF_GA + kv * GROUP_W:OFF_GA + (kv + 1) * GROUP_W]
        ua[:, kv * GROUP_W:(kv + 1) * GROUP_W] = (o * (ga * _sigmoid(ga))).astype(ua.dtype)
    kwin_ref[0] = k_all[n_keys - WINDOW:, :]
    vwin_ref[0] = v_all[n_keys - WINDOW:, :]

    xbuf[0:SUBLANES, :] = jnp.zeros((SUBLANES, D_LRU), F32)
    for j in range(CONV_WIDTH - 1):
        row = SUBLANES - (CONV_WIDTH - 1) + j
        xbuf[row:row + 1, :] = sc_ref[j, pl.ds(b, 1), :]
    hc[...] = jnp.broadcast_to(sh_ref[pl.ds(b, 1), :], (SUBLANES, D_LRU))
    _lru(z, ul_ref.at[0], t_rows, xbuf, a_s, b_s, hc, cw_ref, cb_ref, wg_ref, bg_ref, lam_ref)
    for j in range(CONV_WIDTH - 1):
        row = SUBLANES - (CONV_WIDTH - 1) + j
        conv_ref[j, pl.ds(b, 1), :] = xbuf[row:row + 1, :]
    hout_ref[pl.ds(b, 1), :] = hc[0:1, :]


def _sample_mixer(z3, cache_k, cache_v, state_conv, state_h, sinks, cw, cb, wg, bg, lam):
    bsz, t_rows, _ = z3.shape
    n_cache = cache_k.shape[1]
    assert n_cache + t_rows <= KEY_PAD and n_cache + t_rows >= WINDOW
    assert t_rows % SUBLANES == 0 and t_rows & (t_rows - 1) == 0 and t_rows >= CONV_WIDTH - 1
    smem = pl.BlockSpec(memory_space=pltpu.SMEM)
    per_b = lambda shape: pl.BlockSpec((1,) + shape, lambda b: (b, 0, 0))
    return pl.pallas_call(
        functools.partial(_sample_mixer_kernel, t_rows=t_rows, n_cache=n_cache),
        grid=(bsz,),
        in_specs=[smem, per_b((t_rows, D_IN)), per_b((n_cache, D_KV)), per_b((n_cache, D_KV)),
                  _resident((CONV_WIDTH - 1, bsz, D_LRU)), _resident((bsz, D_LRU)),
                  _resident((CONV_WIDTH, D_LRU)), _resident((1, D_LRU)),
                  _resident((N_GATE_TILES, GATE_TILE, 2 * GATE_TILE)),
                  _resident((2, D_LRU)), _resident((1, D_LRU))],
        out_specs=[per_b((t_rows, D_ATTN)), per_b((t_rows, D_LRU)), per_b((WINDOW, D_KV)), per_b((WINDOW, D_KV)),
                   pl.BlockSpec((CONV_WIDTH - 1, bsz, D_LRU), lambda b: (0, 0, 0)),
                   pl.BlockSpec((bsz, D_LRU), lambda b: (0, 0))],
        out_shape=[jax.ShapeDtypeStruct((bsz, t_rows, D_ATTN), BF16),
                   jax.ShapeDtypeStruct((bsz, t_rows, D_LRU), BF16),
                   jax.ShapeDtypeStruct((bsz, WINDOW, D_KV), F32),
                   jax.ShapeDtypeStruct((bsz, WINDOW, D_KV), F32),
                   jax.ShapeDtypeStruct((CONV_WIDTH - 1, bsz, D_LRU), F32),
                   jax.ShapeDtypeStruct((bsz, D_LRU), F32)],
        scratch_shapes=[pltpu.VMEM((N_KV_HEADS, GQA_GROUP * t_rows, KEY_PAD), F32),
                        pltpu.VMEM((N_KV_HEADS, GQA_GROUP * t_rows, LANES), F32),
                        pltpu.VMEM((N_KV_HEADS, GQA_GROUP * t_rows, KEY_PAD), F32),
                        pltpu.VMEM((N_KV_HEADS, KEY_PAD, GROUP_W), BF16),
                        pltpu.VMEM((N_KV_HEADS, KEY_PAD, V_EXT_W), BF16),
                        pltpu.VMEM((SUBLANES + t_rows, D_LRU), F32),
                        pltpu.VMEM((t_rows, D_LRU), F32),
                        pltpu.VMEM((t_rows, D_LRU), F32),
                        pltpu.VMEM((SUBLANES, D_LRU), F32)],
        compiler_params=pltpu.CompilerParams(
            dimension_semantics=("arbitrary",), vmem_limit_bytes=VMEM_LIMIT_BYTES),
        name="sample_mixer",
    )(sinks, z3, cache_k, cache_v, state_conv, state_h, cw, cb, wg, bg, lam)


def _gate_weight_tiles(w_a, w_x):
    per = GATE_TILE // LRU_BLOCK
    row_block = lax.broadcasted_iota(jnp.int32, (GATE_TILE, GATE_TILE), 0) // LRU_BLOCK
    col_block = lax.broadcasted_iota(jnp.int32, (GATE_TILE, GATE_TILE), 1) // LRU_BLOCK

    def block_diagonal(w):
        rows = w.reshape(N_GATE_TILES, GATE_TILE, LRU_BLOCK)
        return jnp.where(row_block == col_block, jnp.concatenate([rows] * per, axis=-1), 0.0)

    return jnp.concatenate([block_diagonal(w_a), block_diagonal(w_x)], axis=-1).astype(BF16)


def kernel(x_prompt, x_sample, cache_k, cache_v, state_conv, state_h, w_in, b_in, conv_w, conv_b,
           w_gate_a, b_gate_a, w_gate_x, b_gate_x, lru_lambda, attn_sinks, w_out, ln_g, ln_b):
    assert w_in.shape[0] == DEPTH == 1
    bsz, seq, _ = x_prompt.shape
    dbsz, dseq, _ = x_sample.shape
    n_cache = cache_k.shape[2]
    n_p, n_s = bsz * seq, dbsz * dseq
    assert seq % ROW_TILE == 0 and n_p % OUT_ROW_TILE == 0

    wg = _gate_weight_tiles(w_gate_a[0], w_gate_x[0])
    bg = jnp.concatenate([b_gate_a[0], b_gate_x[0]], axis=0).reshape(2, D_LRU)
    cw = conv_w[0]
    sinks = attn_sinks[0]

    xs2 = x_sample.reshape(n_s, D_MODEL)
    zs, w_in_bf = _in_proj(xs2, w_in.reshape(D_MODEL, D_IN), b_in)
    zs = zs.reshape(dbsz, dseq, D_IN)

    uap, ulp, kp, vp, cp, hp, w_out_bf = _prompt_mixer(
        x_prompt, sinks, w_in_bf, b_in, cw, conv_b, wg, bg, lru_lambda, w_out.reshape(D_MODEL, D_MODEL))
    yp = _out_proj(uap.reshape(n_p, D_ATTN), ulp.reshape(n_p, D_LRU), x_prompt.reshape(n_p, D_MODEL),
                   w_out_bf, ln_g, ln_b).reshape(bsz, seq, D_MODEL)

    uas, uls, ks, vs, cs, hs = _sample_mixer(
        zs, cache_k[0].reshape(dbsz, n_cache, D_KV), cache_v[0].reshape(dbsz, n_cache, D_KV),
        state_conv[0].transpose(1, 0, 2), state_h[0],
        sinks, cw, conv_b, wg, bg, lru_lambda)
    ys = _out_proj(uas.reshape(n_s, D_ATTN), uls.reshape(n_s, D_LRU), xs2,
                   w_out_bf, ln_g, ln_b).reshape(dbsz, dseq, D_MODEL)

    def from_channel_major(w):
        return w.reshape(bsz, N_KV_HEADS, HEAD_DIM, WINDOW).transpose(0, 3, 1, 2)[None]

    def from_row_major(w):
        return w.reshape(1, dbsz, WINDOW, N_KV_HEADS, HEAD_DIM)

    return (yp, ys,
            from_channel_major(kp), from_channel_major(vp), cp.transpose(1, 0, 2)[None], hp[None],
            from_row_major(ks), from_row_major(vs), cs.transpose(1, 0, 2)[None], hs[None])
```

```python
import functools

import jax
import jax.numpy as jnp
from jax import lax
from jax.experimental import pallas as pl
from jax.experimental.pallas import tpu as pltpu

F32 = jnp.float32
BF16 = jnp.bfloat16

D_MODEL = 2048
DEPTH = 1
CHUNK = 64
WINDOW = 128
WINDOW_CHUNKS = WINDOW // CHUNK
HEAD_DIM = 64
D_ATTN = D_MODEL // 2
D_LRU = D_MODEL - D_ATTN
N_Q_HEADS = D_ATTN // HEAD_DIM
N_KV_HEADS = 4
GQA_GROUP = N_Q_HEADS // N_KV_HEADS
D_KV = N_KV_HEADS * HEAD_DIM
N_LRU_BLOCKS = 16
LRU_BLOCK = D_LRU // N_LRU_BLOCKS
CONV_WIDTH = 4
LRU_C = 8.0
D_IN = 2 * D_ATTN + 2 * D_KV + 2 * D_LRU
PAST_LEN = 2048
DEEPNORM_ALPHA = (2.0 * DEPTH) ** 0.25
LN_EPS = 1e-5
NEG_INF = -1e30
NEG_LOG2_E = -1.4426950408889634

OFF_Q = 0
OFF_K = D_ATTN
OFF_V = D_ATTN + D_KV
OFF_GA = D_ATTN + 2 * D_KV
OFF_XL = OFF_GA + D_ATTN
OFF_GL = OFF_XL + D_LRU

SUBLANES = 8
LANES = 128
MXU_DIM = 256
VMEM_LIMIT_BYTES = 56 * 1024 * 1024

GROUP_W = GQA_GROUP * HEAD_DIM
KEY_PAD = 2 * LANES
V_EXT_W = 2 * LANES
GATE_TILE = MXU_DIM
N_GATE_TILES = D_LRU // GATE_TILE
ROW_TILE = 256
OUT_ROW_TILE = 1024
SCORE_LOOKAHEAD = 2
SCORE_RING = SCORE_LOOKAHEAD + 2
N_TILE = 512
K_CHUNK = 512
GATE_ROW_PARTS = 4
N_CHUNKS_PER_TILE = ROW_TILE // CHUNK
KBUF_ROWS = WINDOW + ROW_TILE + (KEY_PAD - (WINDOW + CHUNK))

_SLOPES = tuple(2.0 ** (-8.0 * h / N_Q_HEADS) for h in range(1, N_Q_HEADS + 1))


def _resident(shape):
    return pl.BlockSpec(shape, lambda *_: (0,) * len(shape), pipeline_mode=pl.Buffered(1))


class _ColGroups:
    def __init__(self, groups):
        self._groups = groups

    def _locate(self, idx):
        rows, cols = idx
        for first, end, ref in self._groups:
            if first <= cols.start and cols.stop <= end:
                return ref, rows, slice(cols.start - first, cols.stop - first)
        raise ValueError(f"columns {cols} straddle projection groups")

    def __getitem__(self, idx):
        ref, rows, cols = self._locate(idx)
        return ref[rows, cols]

    def __setitem__(self, idx, value):
        ref, rows, cols = self._locate(idx)
        ref[rows, cols] = value


_ALL_ROWS = slice(None)
_PROJ_GROUPS = ((OFF_Q, OFF_K), (OFF_K, OFF_GA), (OFF_GA, OFF_XL), (OFF_XL, OFF_GL), (OFF_GL, D_IN))


def _sigmoid(x):
    return 1.0 / (1.0 + jnp.exp2(x * NEG_LOG2_E))


def _in_proj_substeps(col_starts):
    return [(c, k) for c in col_starts for k in range(0, D_MODEL, K_CHUNK)]


def _in_proj_substep(xb, w_ref, b_ref, z, col, k0):
    cols = slice(col, col + N_TILE)
    part = jnp.dot(xb[:, k0:k0 + K_CHUNK], w_ref[k0:k0 + K_CHUNK, cols], preferred_element_type=F32)
    z[_ALL_ROWS, cols] = part + (b_ref[:, cols] if k0 == 0 else z[_ALL_ROWS, cols])


def _bias_table(nq, kv, key0_pos, q0_pos, n_keys):
    rows = GQA_GROUP * nq
    r = lax.broadcasted_iota(jnp.int32, (rows, KEY_PAD), 0)
    c = lax.broadcasted_iota(jnp.int32, (rows, KEY_PAD), 1)
    shift = nq.bit_length() - 1
    g = jnp.right_shift(r, shift)
    qpos = q0_pos + (r & (nq - 1))
    kpos = key0_pos + c
    slope = jnp.full((rows, KEY_PAD), _SLOPES[kv * GQA_GROUP], F32)
    for gg in range(1, GQA_GROUP):
        slope = jnp.where(g == gg, _SLOPES[kv * GQA_GROUP + gg], slope)
    dist = jnp.abs(qpos - kpos).astype(F32)
    cshift = CHUNK.bit_length() - 1
    qc = jnp.right_shift(qpos, cshift)
    kc = jnp.right_shift(kpos, cshift)
    valid = (kpos >= 0) & (kc <= qc) & (kc >= qc - WINDOW_CHUNKS) & (c < n_keys)
    return jnp.where(valid, -(slope * dist), NEG_INF)


def _tile_heads(x, kv, copies):
    pair = x[:, (kv // 2) * LANES:(kv // 2 + 1) * LANES]
    rolled = pltpu.roll(pair, HEAD_DIM, axis=1)
    lane = lax.broadcasted_iota(jnp.int32, pair.shape, 1)
    low = lane < HEAD_DIM
    both = jnp.where(low, pair, rolled) if kv % 2 == 0 else jnp.where(low, rolled, pair)
    both = both.astype(BF16)
    return jnp.concatenate([both] * (copies * HEAD_DIM // LANES), axis=1)


def _sink_table(nq, kv, sinks_ref):
    return jnp.concatenate(
        [jnp.full((nq, LANES), sinks_ref[kv * GQA_GROUP + g], F32) for g in range(GQA_GROUP)], axis=0)


def _head_of_lane(nq):
    lane = lax.broadcasted_iota(jnp.int32, (nq, GROUP_W), 1)
    return jnp.right_shift(lane, HEAD_DIM.bit_length() - 1)


def _attn_scores(q, k4):
    nq = q.shape[0]
    qb = (q * (HEAD_DIM ** -0.5)).astype(BF16)
    head_of_lane = _head_of_lane(nq)
    zero = jnp.zeros_like(qb)
    qs = jnp.concatenate([jnp.where(head_of_lane == g, qb, zero) for g in range(GQA_GROUP)], axis=0)
    return lax.dot_general(qs, k4, (((1,), (1,)), ((), ())), preferred_element_type=F32)


def _attn_finish(scores, v4e, bias, sink):
    nq = scores.shape[0] // GQA_GROUP
    s = scores + bias
    m = jnp.maximum(jnp.broadcast_to(jnp.max(s, axis=-1, keepdims=True), sink.shape), sink)
    e = jnp.concatenate(
        [jnp.exp(s[:, j * LANES:(j + 1) * LANES] - m) for j in range(KEY_PAD // LANES)], axis=1)
    o4e = jnp.dot(e.astype(BF16), v4e, preferred_element_type=F32)
    o2 = o4e[:, :LANES] * (1.0 / (o4e[:, LANES:] + jnp.exp(sink - m)))
    low = lax.broadcasted_iota(jnp.int32, (nq, LANES), 1) < HEAD_DIM
    return jnp.concatenate(
        [jnp.where(low, o2[g * nq:(g + 1) * nq], o2[(g + 1) * nq:(g + 2) * nq]) for g in range(0, GQA_GROUP, 2)],
        axis=1)


def _lru(z, u_out, t_rows, xbuf, a_s, b_s, hc, cw_ref, cb_ref, wg_ref, bg_ref, lam_ref, fill=None):
    interleaved = fill is not None
    fill = fill or (lambda: None)
    neg = -lam_ref[...]
    softplus = jnp.maximum(neg, 0.0) + jnp.log1p(jnp.exp(-jnp.abs(neg)))
    coef = -LRU_C * softplus
    tiles = [slice(j * GATE_TILE, (j + 1) * GATE_TILE) for j in range(N_GATE_TILES)]

    def conv_and_gate_matmul(cols, j):
        xl = z[_ALL_ROWS, OFF_XL + cols.start:OFF_XL + cols.stop]
        xbuf[SUBLANES:SUBLANES + t_rows, cols] = xl
        xc = cb_ref[:, cols] + cw_ref[CONV_WIDTH - 1:CONV_WIDTH, cols] * xl
        for k in range(CONV_WIDTH - 1):
            off = SUBLANES - (CONV_WIDTH - 1) + k
            xc = xc + cw_ref[k:k + 1, cols] * xbuf[off:off + t_rows, cols]
        pre = jnp.dot(xc.astype(BF16), wg_ref[j], preferred_element_type=F32)
        a_s[:, cols] = pre[:, :GATE_TILE]
        b_s[:, cols] = pre[:, GATE_TILE:]
        return xc

    part_rows = t_rows // GATE_ROW_PARTS if interleaved else t_rows

    def gate_math(cols, xc):
        for r0 in range(0, t_rows, part_rows):
            rows = slice(r0, r0 + part_rows)
            r = _sigmoid(a_s[rows, cols] + bg_ref[0:1, cols])
            i = _sigmoid(b_s[rows, cols] + bg_ref[1:2, cols])
            a = jnp.exp(coef[:, cols] * r)
            mult = jnp.sqrt(jnp.maximum(1.0 - a * a, 0.0))
            a_s[rows, cols] = a
            b_s[rows, cols] = mult * (i * xc[rows])
            fill()

    fill()
    xc_prev = None
    for j, cols in enumerate(tiles):
        xc = conv_and_gate_matmul(cols, j)
        fill()
        if xc_prev is not None:
            gate_math(tiles[j - 1], xc_prev)
        xc_prev = xc
    gate_math(tiles[-1], xc_prev)

    row = lax.broadcasted_iota(jnp.int32, (SUBLANES, D_LRU), 0)
    h_prev = hc[...]
    for gi in range(t_rows // SUBLANES):
        rows = slice(gi * SUBLANES, (gi + 1) * SUBLANES)
        a8 = a_s[rows, :]
        b8 = b_s[rows, :]
        for d in (1, 2, 4):
            keep = row >= d
            a_sh = pltpu.roll(a8, d, axis=0)
            b_sh = pltpu.roll(b8, d, axis=0)
            b8 = jnp.where(keep, a8 * b_sh + b8, b8)
            a8 = jnp.where(keep, a8 * a_sh, a8)
        h8 = a8 * h_prev + b8
        b_s[rows, :] = h8
        h_prev = jnp.broadcast_to(h8[SUBLANES - 1:SUBLANES, :], (SUBLANES, D_LRU))
    hc[...] = h_prev
    gl = z[_ALL_ROWS, OFF_GL:OFF_GL + D_LRU]
    u_out[...] = (b_s[...] * (gl * _sigmoid(gl))).astype(u_out.dtype)
    xbuf[0:SUBLANES, :] = xbuf[t_rows:t_rows + SUBLANES, :]


def _prompt_kernel(sinks_ref, x_ref, win_ref, bin_ref, cw_ref, cb_ref, wg_ref, bg_ref, lam_ref, wout_ref,
                   ua_ref, ul_ref, kwin_ref, vwin_ref, conv_ref, hout_ref, wout_bf_ref,
                   zq, zkv, zga, zxl, zgl, bias_s, mask_s, sink_s, s_ring, k4_s, v4_s, xbuf, a_s, b_s, hc,
                   *, n_t, n_tiles):
    wout_bf_ref[...] = wout_ref[...].astype(BF16)
    z = _ColGroups([(first, end, ref) for (first, end), ref in zip(_PROJ_GROUPS, (zq, zkv, zga, zxl, zgl))])
    ua = ua_ref.at[0]
    s = pl.program_id(0)
    t_a = lax.rem(jnp.minimum(s, n_tiles - 1), n_t)
    t_l = lax.rem(jnp.maximum(s - 1, 0), n_t)

    @pl.when(s == 0)
    def _init():
        for kv in range(N_KV_HEADS):
            bias_s[kv] = _bias_table(CHUNK, kv, 0, WINDOW, WINDOW + CHUNK)
            sink_s[kv] = _sink_table(CHUNK, kv, sinks_ref)
        col = lax.broadcasted_iota(jnp.int32, (SUBLANES, KEY_PAD), 1)
        for ci in range(WINDOW_CHUNKS):
            mask_s[ci] = jnp.where(col < WINDOW - ci * CHUNK, NEG_INF, 0.0)
        k4_s[...] = jnp.zeros_like(k4_s)
        v4_s[:, :, 0:LANES] = jnp.zeros((N_KV_HEADS, KBUF_ROWS, LANES), BF16)
        v4_s[:, :, LANES:] = jnp.ones((N_KV_HEADS, KBUF_ROWS, LANES), BF16)
        zxl[...] = jnp.zeros_like(zxl)
        zgl[...] = jnp.zeros_like(zgl)

    @pl.when(t_a == 0)
    def _reset_attention_carry():
        k4_s[:, 0:WINDOW, :] = jnp.zeros((N_KV_HEADS, WINDOW, GROUP_W), BF16)
        v4_s[:, 0:WINDOW, 0:LANES] = jnp.zeros((N_KV_HEADS, WINDOW, LANES), BF16)

    @pl.when(t_l == 0)
    def _reset_lru_carry():
        xbuf[0:SUBLANES, :] = jnp.zeros((SUBLANES, D_LRU), F32)
        hc[...] = jnp.zeros_like(hc)

    xb = x_ref[0].astype(BF16)
    early_cols = _in_proj_substeps(list(range(OFF_K, OFF_GA, N_TILE)) + list(range(OFF_Q, OFF_K, N_TILE)) + list(
        range(OFF_GA, OFF_XL, N_TILE)))
    late_cols = _in_proj_substeps(range(OFF_XL, D_IN, N_TILE))

    def project(steps):
        if steps:
            _in_proj_substep(xb, win_ref, bin_ref, z, *steps.pop(0))

    _lru(z, ul_ref.at[0], ROW_TILE, xbuf, a_s, b_s, hc, cw_ref, cb_ref, wg_ref, bg_ref, lam_ref,
         fill=lambda: project(early_cols))
    while early_cols:
        project(early_cols)

    k_new = z[_ALL_ROWS, OFF_K:OFF_K + D_KV]
    v_new = z[_ALL_ROWS, OFF_V:OFF_V + D_KV]
    for kv in range(N_KV_HEADS):
        k4_s[kv, WINDOW:WINDOW + ROW_TILE, :] = _tile_heads(k_new, kv, GQA_GROUP)
        v4_s[kv, WINDOW:WINDOW + ROW_TILE, 0:LANES] = _tile_heads(v_new, kv, 2)

    units = [(ci, kv) for ci in range(N_CHUNKS_PER_TILE) for kv in range(N_KV_HEADS)]

    def scores_into_ring(idx):
        ci, kv = units[idx]
        s_ring[idx % SCORE_RING] = _attn_scores(
            z[ci * CHUNK:(ci + 1) * CHUNK, OFF_Q + kv * GROUP_W:OFF_Q + (kv + 1) * GROUP_W],
            k4_s[kv, ci * CHUNK:ci * CHUNK + KEY_PAD, :])

    mask_rows = [jnp.where(t_a == 0, mask_s[ci, 0:1, :], 0.0) for ci in range(WINDOW_CHUNKS)]
    late_per_unit = -(-len(late_cols) // len(units))
    for idx in range(SCORE_LOOKAHEAD):
        scores_into_ring(idx)
    for idx, (ci, kv) in enumerate(units):
        if idx + SCORE_LOOKAHEAD < len(units):
            scores_into_ring(idx + SCORE_LOOKAHEAD)
        for _ in range(late_per_unit):
            project(late_cols)
        s_cur = s_ring[idx % SCORE_RING]
        rows = slice(ci * CHUNK, (ci + 1) * CHUNK)
        bias = bias_s[kv] + mask_rows[ci] if ci < WINDOW_CHUNKS else bias_s[kv]
        o = _attn_finish(s_cur, v4_s[kv, ci * CHUNK:ci * CHUNK + KEY_PAD, :], bias, sink_s[kv])
        ga = z[rows, OFF_GA + kv * GROUP_W:OFF_GA + (kv + 1) * GROUP_W]
        ua[rows, kv * GROUP_W:(kv + 1) * GROUP_W] = (o * (ga * _sigmoid(ga))).astype(ua.dtype)
    assert not early_cols and not late_cols

    @pl.when(t_l == n_t - 1)
    def _emit_lru_state():
        seq_l = lax.div(jnp.maximum(s - 1, 0), n_t)
        for j in range(CONV_WIDTH - 1):
            row = SUBLANES - (CONV_WIDTH - 1) + j
            conv_ref[j, pl.ds(seq_l, 1), :] = xbuf[row:row + 1, :]
        hout_ref[pl.ds(seq_l, 1), :] = hc[0:1, :]

    @pl.when(t_a == n_t - 1)
    def _emit_kv_window():
        kwin_ref[0] = z[slice(ROW_TILE - WINDOW, ROW_TILE), OFF_K:OFF_K + D_KV].T
        vwin_ref[0] = z[slice(ROW_TILE - WINDOW, ROW_TILE), OFF_V:OFF_V + D_KV].T

    @pl.when(s < n_tiles - 1)
    def _carry_kv():
        k4_s[:, 0:WINDOW, :] = k4_s[:, ROW_TILE:ROW_TILE + WINDOW, :]
        v4_s[:, 0:WINDOW, 0:LANES] = v4_s[:, ROW_TILE:ROW_TILE + WINDOW, 0:LANES]


def _prompt_mixer(x3, sinks, w_in_bf, b_in_row, cw, cb, wg, bg, lam, w_out):
    bsz, seq, _ = x3.shape
    n_t = seq // ROW_TILE
    n_tiles = bsz * n_t
    smem = pl.BlockSpec(memory_space=pltpu.SMEM)
    assert D_MODEL % n_tiles == 0 and (D_MODEL // n_tiles) % (2 * SUBLANES) == 0
    w_out_slab = pl.BlockSpec((D_MODEL // n_tiles, D_MODEL), lambda s: (jnp.minimum(s, n_tiles - 1), 0))

    def attn_tile(s):
        ta = jnp.minimum(s, n_tiles - 1)
        return ta // n_t, ta % n_t

    def lru_tile(s):
        tl = jnp.maximum(s - 1, 0)
        return tl // n_t, tl % n_t

    return pl.pallas_call(
        functools.partial(_prompt_kernel, n_t=n_t, n_tiles=n_tiles),
        grid=(n_tiles + 1,),
        in_specs=[smem,
                  pl.BlockSpec((1, ROW_TILE, D_MODEL), lambda s: (*attn_tile(s), 0)),
                  _resident((D_MODEL, D_IN)), _resident((1, D_IN)),
                  _resident((CONV_WIDTH, D_LRU)), _resident((1, D_LRU)),
                  _resident((N_GATE_TILES, GATE_TILE, 2 * GATE_TILE)),
                  _resident((2, D_LRU)), _resident((1, D_LRU)), w_out_slab],
        out_specs=[pl.BlockSpec((1, ROW_TILE, D_ATTN), lambda s: (*attn_tile(s), 0)),
                   pl.BlockSpec((1, ROW_TILE, D_LRU), lambda s: (*lru_tile(s), 0)),
                   pl.BlockSpec((1, D_KV, WINDOW), lambda s: (attn_tile(s)[0], 0, 0)),
                   pl.BlockSpec((1, D_KV, WINDOW), lambda s: (attn_tile(s)[0], 0, 0)),
                   pl.BlockSpec((CONV_WIDTH - 1, bsz, D_LRU), lambda s: (0, 0, 0)),
                   pl.BlockSpec((bsz, D_LRU), lambda s: (0, 0)), w_out_slab],
        out_shape=[jax.ShapeDtypeStruct((bsz, seq, D_ATTN), BF16),
                   jax.ShapeDtypeStruct((bsz, seq, D_LRU), BF16),
                   jax.ShapeDtypeStruct((bsz, D_KV, WINDOW), F32),
                   jax.ShapeDtypeStruct((bsz, D_KV, WINDOW), F32),
                   jax.ShapeDtypeStruct((CONV_WIDTH - 1, bsz, D_LRU), F32),
                   jax.ShapeDtypeStruct((bsz, D_LRU), F32),
                   jax.ShapeDtypeStruct((D_MODEL, D_MODEL), BF16)],
        scratch_shapes=[pltpu.VMEM((ROW_TILE, end - first), F32) for first, end in _PROJ_GROUPS] + [
                        pltpu.VMEM((N_KV_HEADS, GQA_GROUP * CHUNK, KEY_PAD), F32),
                        pltpu.VMEM((WINDOW_CHUNKS, SUBLANES, KEY_PAD), F32),
                        pltpu.VMEM((N_KV_HEADS, GQA_GROUP * CHUNK, LANES), F32),
                        pltpu.VMEM((SCORE_RING, GQA_GROUP * CHUNK, KEY_PAD), F32),
                        pltpu.VMEM((N_KV_HEADS, KBUF_ROWS, GROUP_W), BF16),
                        pltpu.VMEM((N_KV_HEADS, KBUF_ROWS, V_EXT_W), BF16),
                        pltpu.VMEM((SUBLANES + ROW_TILE, D_LRU), F32),
                        pltpu.VMEM((ROW_TILE, D_LRU), F32),
                        pltpu.VMEM((ROW_TILE, D_LRU), F32),
                        pltpu.VMEM((SUBLANES, D_LRU), F32)],
        compiler_params=pltpu.CompilerParams(
            dimension_semantics=("arbitrary",), vmem_limit_bytes=VMEM_LIMIT_BYTES),
        name="prompt_mixer",
    )(sinks, x3, w_in_bf, b_in_row, cw, cb, wg, bg, lam, w_out)


def _out_proj_kernel(ua_ref, ul_ref, x_ref, w_ref, g_ref, b_ref, y_ref):
    tm = y_ref.shape[0]
    part = min(MXU_DIM, tm)
    n_parts = tm // part
    n_chunks = D_MODEL // N_TILE
    ln_rows = part // n_chunks

    def matmul_chunk(p, n):
        rows = slice(p * part, (p + 1) * part)
        cols = slice(n * N_TILE, (n + 1) * N_TILE)
        y_ref[rows, cols] = (
            DEEPNORM_ALPHA * x_ref[rows, cols]
            + jnp.dot(ua_ref[rows, :], w_ref[0:D_ATTN, cols], preferred_element_type=F32)
            + jnp.dot(ul_ref[rows, :], w_ref[D_ATTN:, cols], preferred_element_type=F32))

    def layer_norm_rows(p, r):
        rows = slice(p * part + r * ln_rows, p * part + (r + 1) * ln_rows)
        y = y_ref[rows, :]
        mu = jnp.mean(y, axis=-1, keepdims=True)
        yc = y - mu
        var = jnp.mean(yc * yc, axis=-1, keepdims=True)
        y_ref[rows, :] = yc * lax.rsqrt(var + LN_EPS) * g_ref[...] + b_ref[...]

    for n in range(n_chunks):
        matmul_chunk(0, n)
    for p in range(1, n_parts):
        for n in range(n_chunks):
            matmul_chunk(p, n)
            layer_norm_rows(p - 1, n)
    for r in range(n_chunks):
        layer_norm_rows(n_parts - 1, r)


def _out_proj(ua2d, ul2d, x2d, w_bf, g_row, b_row):
    m = x2d.shape[0]
    tm = min(OUT_ROW_TILE, m)
    return pl.pallas_call(
        _out_proj_kernel,
        grid=(m // tm,),
        in_specs=[pl.BlockSpec((tm, D_ATTN), lambda i: (i, 0)),
                  pl.BlockSpec((tm, D_LRU), lambda i: (i, 0)),
                  pl.BlockSpec((tm, D_MODEL), lambda i: (i, 0)),
                  _resident((D_MODEL, D_MODEL)),
                  _resident((1, D_MODEL)),
                  _resident((1, D_MODEL))],
        out_specs=pl.BlockSpec((tm, D_MODEL), lambda i: (i, 0)),
        out_shape=jax.ShapeDtypeStruct((m, D_MODEL), F32),
        compiler_params=pltpu.CompilerParams(
            dimension_semantics=("arbitrary",), vmem_limit_bytes=VMEM_LIMIT_BYTES),
        name="out_proj",
    )(ua2d, ul2d, x2d, w_bf, g_row, b_row)


def _in_proj_kernel(x_ref, w_ref, b_ref, z_ref, w_bf_ref):
    @pl.when(pl.program_id(0) == 0)
    def _start_from_bias():
        z_ref[...] = jnp.broadcast_to(b_ref[...], z_ref.shape)

    xb = x_ref[...].astype(BF16)
    for n in range(0, D_IN, N_TILE):
        wb = w_ref[:, n:n + N_TILE].astype(BF16)
        w_bf_ref[:, n:n + N_TILE] = wb
        z_ref[:, n:n + N_TILE] += jnp.dot(xb, wb, preferred_element_type=F32)


def _in_proj(x2d, w, b_row):
    m = x2d.shape[0]
    return pl.pallas_call(
        _in_proj_kernel,
        grid=(D_MODEL // MXU_DIM,),
        in_specs=[pl.BlockSpec((m, MXU_DIM), lambda k: (0, k)),
                  pl.BlockSpec((MXU_DIM, D_IN), lambda k: (k, 0)),
                  pl.BlockSpec((1, D_IN), lambda k: (0, 0))],
        out_specs=[pl.BlockSpec((m, D_IN), lambda k: (0, 0)),
                   pl.BlockSpec((MXU_DIM, D_IN), lambda k: (k, 0))],
        out_shape=[jax.ShapeDtypeStruct((m, D_IN), F32),
                   jax.ShapeDtypeStruct((D_MODEL, D_IN), BF16)],
        compiler_params=pltpu.CompilerParams(
            dimension_semantics=("arbitrary",), vmem_limit_bytes=VMEM_LIMIT_BYTES),
        name="in_proj",
    )(x2d, w, b_row)


def _sample_mixer_kernel(sinks_ref, z_ref, ck_ref, cv_ref, sc_ref, sh_ref,
                         cw_ref, cb_ref, wg_ref, bg_ref, lam_ref,
                         ua_ref, ul_ref, kwin_ref, vwin_ref, conv_ref, hout_ref,
                         bias_s, sink_s, s_ring, k4_s, v4_s, xbuf, a_s, b_s, hc, *, t_rows, n_cache):
    b = pl.program_id(0)
    n_keys = n_cache + t_rows
    z = _ColGroups([(0, D_IN, z_ref.at[0])])
    ua = ua_ref.at[0]

    @pl.when(b == 0)
    def _init_tables():
        for kv in range(N_KV_HEADS):
            bias_s[kv] = _bias_table(t_rows, kv, PAST_LEN - n_cache, PAST_LEN, n_keys)
            sink_s[kv] = _sink_table(t_rows, kv, sinks_ref)
        k4_s[...] = jnp.zeros_like(k4_s)
        v4_s[:, :, 0:LANES] = jnp.zeros((N_KV_HEADS, KEY_PAD, LANES), BF16)
        v4_s[:, :, LANES:] = jnp.ones((N_KV_HEADS, KEY_PAD, LANES), BF16)

    k_all = jnp.concatenate([ck_ref[0], z[_ALL_ROWS, OFF_K:OFF_K + D_KV]], axis=0)
    v_all = jnp.concatenate([cv_ref[0], z[_ALL_ROWS, OFF_V:OFF_V + D_KV]], axis=0)
    for kv in range(N_KV_HEADS):
        k4_s[kv, 0:n_keys, :] = _tile_heads(k_all, kv, GQA_GROUP)
        v4_s[kv, 0:n_keys, 0:LANES] = _tile_heads(v_all, kv, 2)
    for kv in range(N_KV_HEADS):
        s_ring[kv] = _attn_scores(z[_ALL_ROWS, OFF_Q + kv * GROUP_W:OFF_Q + (kv + 1) * GROUP_W], k4_s[kv])
    for kv in range(N_KV_HEADS):
        o = _attn_finish(s_ring[kv], v4_s[kv], bias_s[kv], sink_s[kv])
        ga = z[_ALL_ROWS, OFF_GA + kv * GROUP_W:OFF_GA + (kv + 1) * GROUP_W]
        ua[:, kv * GROUP_W:(kv + 1) * GROUP_W] = (o * (ga * _sigmoid(ga))).astype(ua.dtype)
    kwin_ref[0] = k_all[n_keys - WINDOW:, :]
    vwin_ref[0] = v_all[n_keys - WINDOW:, :]

    xbuf[0:SUBLANES, :] = jnp.zeros((SUBLANES, D_LRU), F32)
    for j in range(CONV_WIDTH - 1):
        row = SUBLANES - (CONV_WIDTH - 1) + j
        xbuf[row:row + 1, :] = sc_ref[j, pl.ds(b, 1), :]
    hc[...] = jnp.broadcast_to(sh_ref[pl.ds(b, 1), :], (SUBLANES, D_LRU))
    _lru(z, ul_ref.at[0], t_rows, xbuf, a_s, b_s, hc, cw_ref, cb_ref, wg_ref, bg_ref, lam_ref)
    for j in range(CONV_WIDTH - 1):
        row = SUBLANES - (CONV_WIDTH - 1) + j
        conv_ref[j, pl.ds(b, 1), :] = xbuf[row:row + 1, :]
    hout_ref[pl.ds(b, 1), :] = hc[0:1, :]


def _sample_mixer(z3, cache_k, cache_v, state_conv, state_h, sinks, cw, cb, wg, bg, lam):
    bsz, t_rows, _ = z3.shape
    n_cache = cache_k.shape[1]
    assert n_cache + t_rows <= KEY_PAD and n_cache + t_rows >= WINDOW
    assert t_rows % SUBLANES == 0 and t_rows & (t_rows - 1) == 0 and t_rows >= CONV_WIDTH - 1
    smem = pl.BlockSpec(memory_space=pltpu.SMEM)
    per_b = lambda shape: pl.BlockSpec((1,) + shape, lambda b: (b, 0, 0))
    return pl.pallas_call(
        functools.partial(_sample_mixer_kernel, t_rows=t_rows, n_cache=n_cache),
        grid=(bsz,),
        in_specs=[smem, per_b((t_rows, D_IN)), per_b((n_cache, D_KV)), per_b((n_cache, D_KV)),
                  _resident((CONV_WIDTH - 1, bsz, D_LRU)), _resident((bsz, D_LRU)),
                  _resident((CONV_WIDTH, D_LRU)), _resident((1, D_LRU)),
                  _resident((N_GATE_TILES, GATE_TILE, 2 * GATE_TILE)),
                  _resident((2, D_LRU)), _resident((1, D_LRU))],
        out_specs=[per_b((t_rows, D_ATTN)), per_b((t_rows, D_LRU)), per_b((WINDOW, D_KV)), per_b((WINDOW, D_KV)),
                   pl.BlockSpec((CONV_WIDTH - 1, bsz, D_LRU), lambda b: (0, 0, 0)),
                   pl.BlockSpec((bsz, D_LRU), lambda b: (0, 0))],
        out_shape=[jax.ShapeDtypeStruct((bsz, t_rows, D_ATTN), BF16),
                   jax.ShapeDtypeStruct((bsz, t_rows, D_LRU), BF16),
                   jax.ShapeDtypeStruct((bsz, WINDOW, D_KV), F32),
                   jax.ShapeDtypeStruct((bsz, WINDOW, D_KV), F32),
                   jax.ShapeDtypeStruct((CONV_WIDTH - 1, bsz, D_LRU), F32),
                   jax.ShapeDtypeStruct((bsz, D_LRU), F32)],
        scratch_shapes=[pltpu.VMEM((N_KV_HEADS, GQA_GROUP * t_rows, KEY_PAD), F32),
                        pltpu.VMEM((N_KV_HEADS, GQA_GROUP * t_rows, LANES), F32),
                        pltpu.VMEM((N_KV_HEADS, GQA_GROUP * t_rows, KEY_PAD), F32),
                        pltpu.VMEM((N_KV_HEADS, KEY_PAD, GROUP_W), BF16),
                        pltpu.VMEM((N_KV_HEADS, KEY_PAD, V_EXT_W), BF16),
                        pltpu.VMEM((SUBLANES + t_rows, D_LRU), F32),
                        pltpu.VMEM((t_rows, D_LRU), F32),
                        pltpu.VMEM((t_rows, D_LRU), F32),
                        pltpu.VMEM((SUBLANES, D_LRU), F32)],
        compiler_params=pltpu.CompilerParams(
            dimension_semantics=("arbitrary",), vmem_limit_bytes=VMEM_LIMIT_BYTES),
        name="sample_mixer",
    )(sinks, z3, cache_k, cache_v, state_conv, state_h, cw, cb, wg, bg, lam)


def _gate_weight_tiles(w_a, w_x):
    per = GATE_TILE // LRU_BLOCK
    row_block = lax.broadcasted_iota(jnp.int32, (GATE_TILE, GATE_TILE), 0) // LRU_BLOCK
    col_block = lax.broadcasted_iota(jnp.int32, (GATE_TILE, GATE_TILE), 1) // LRU_BLOCK

    def block_diagonal(w):
        rows = w.reshape(N_GATE_TILES, GATE_TILE, LRU_BLOCK)
        return jnp.where(row_block == col_block, jnp.concatenate([rows] * per, axis=-1), 0.0)

    return jnp.concatenate([block_diagonal(w_a), block_diagonal(w_x)], axis=-1).astype(BF16)


def kernel(x_prompt, x_sample, cache_k, cache_v, state_conv, state_h, w_in, b_in, conv_w, conv_b,
           w_gate_a, b_gate_a, w_gate_x, b_gate_x, lru_lambda, attn_sinks, w_out, ln_g, ln_b):
    assert w_in.shape[0] == DEPTH == 1
    bsz, seq, _ = x_prompt.shape
    dbsz, dseq, _ = x_sample.shape
    n_cache = cache_k.shape[2]
    n_p, n_s = bsz * seq, dbsz * dseq
    assert seq % ROW_TILE == 0 and n_p % OUT_ROW_TILE == 0

    wg = _gate_weight_tiles(w_gate_a[0], w_gate_x[0])
    bg = jnp.concatenate([b_gate_a[0], b_gate_x[0]], axis=0).reshape(2, D_LRU)
    cw = conv_w[0]
    sinks = attn_sinks[0]

    xs2 = x_sample.reshape(n_s, D_MODEL)
    zs, w_in_bf = _in_proj(xs2, w_in.reshape(D_MODEL, D_IN), b_in)
    zs = zs.reshape(dbsz, dseq, D_IN)

    uap, ulp, kp, vp, cp, hp, w_out_bf = _prompt_mixer(
        x_prompt, sinks, w_in_bf, b_in, cw, conv_b, wg, bg, lru_lambda, w_out.reshape(D_MODEL, D_MODEL))
    yp = _out_proj(uap.reshape(n_p, D_ATTN), ulp.reshape(n_p, D_LRU), x_prompt.reshape(n_p, D_MODEL),
                   w_out_bf, ln_g, ln_b).reshape(bsz, seq, D_MODEL)

    uas, uls, ks, vs, cs, hs = _sample_mixer(
        zs, cache_k[0].reshape(dbsz, n_cache, D_KV), cache_v[0].reshape(dbsz, n_cache, D_KV),
        state_conv[0].transpose(1, 0, 2), state_h[0],
        sinks, cw, conv_b, wg, bg, lru_lambda)
    ys = _out_proj(uas.reshape(n_s, D_ATTN), uls.reshape(n_s, D_LRU), xs2,
                   w_out_bf, ln_g, ln_b).reshape(dbsz, dseq, D_MODEL)

    def from_channel_major(w):
        return w.reshape(bsz, N_KV_HEADS, HEAD_DIM, WINDOW).transpose(0, 3, 1, 2)[None]

    def from_row_major(w):
        return w.reshape(1, dbsz, WINDOW, N_KV_HEADS, HEAD_DIM)

    return (yp, ys,
            from_channel_major(kp), from_channel_major(vp), cp.transpose(1, 0, 2)[None], hp[None],
            from_row_major(ks), from_row_major(vs), cs.transpose(1, 0, 2)[None], hs[None])
```

```python
import functools

import jax
import jax.numpy as jnp
from jax import lax
from jax.experimental import pallas as pl
from jax.experimental.pallas import tpu as pltpu

F32 = jnp.float32
BF16 = jnp.bfloat16

D_MODEL = 2048
DEPTH = 1
CHUNK = 64
WINDOW = 128
WINDOW_CHUNKS = WINDOW // CHUNK
HEAD_DIM = 64
D_ATTN = D_MODEL // 2
D_LRU = D_MODEL - D_ATTN
N_Q_HEADS = D_ATTN // HEAD_DIM
N_KV_HEADS = 4
GQA_GROUP = N_Q_HEADS // N_KV_HEADS
D_KV = N_KV_HEADS * HEAD_DIM
N_LRU_BLOCKS = 16
LRU_BLOCK = D_LRU // N_LRU_BLOCKS
CONV_WIDTH = 4
LRU_C = 8.0
D_IN = 2 * D_ATTN + 2 * D_KV + 2 * D_LRU
PAST_LEN = 2048
DEEPNORM_ALPHA = (2.0 * DEPTH) ** 0.25
LN_EPS = 1e-5
NEG_INF = -1e30
NEG_LOG2_E = -1.4426950408889634
TINY = 1e-30

OFF_Q = 0
OFF_K = D_ATTN
OFF_V = D_ATTN + D_KV
OFF_GA = D_ATTN + 2 * D_KV
OFF_XL = OFF_GA + D_ATTN
OFF_GL = OFF_XL + D_LRU

SUBLANES = 8
LANES = 128
MXU_DIM = 256
VMEM_LIMIT_BYTES = 56 * 1024 * 1024

GROUP_W = GQA_GROUP * HEAD_DIM
KEY_PAD = 2 * LANES
V_EXT_W = 2 * LANES
GATE_TILE = MXU_DIM
N_GATE_TILES = D_LRU // GATE_TILE
ROW_TILE = 256
OUT_ROW_TILE = 1024
SCORE_LOOKAHEAD = 2
SCORE_RING = SCORE_LOOKAHEAD + 2
N_TILE = 512
K_CHUNK = 512
GATE_ROW_PARTS = 4
N_CHUNKS_PER_TILE = ROW_TILE // CHUNK
KBUF_ROWS = WINDOW + ROW_TILE + (KEY_PAD - (WINDOW + CHUNK))

_SLOPES = tuple(2.0 ** (-8.0 * h / N_Q_HEADS) for h in range(1, N_Q_HEADS + 1))


def _resident(shape):
    return pl.BlockSpec(shape, lambda *_: (0,) * len(shape), pipeline_mode=pl.Buffered(1))


class _ColGroups:
    def __init__(self, groups):
        self._groups = groups

    def _locate(self, idx):
        rows, cols = idx
        for first, end, ref in self._groups:
            if first <= cols.start and cols.stop <= end:
                return ref, rows, slice(cols.start - first, cols.stop - first)
        raise ValueError(f"columns {cols} straddle projection groups")

    def __getitem__(self, idx):
        ref, rows, cols = self._locate(idx)
        return ref[rows, cols]

    def __setitem__(self, idx, value):
        ref, rows, cols = self._locate(idx)
        ref[rows, cols] = value


_ALL_ROWS = slice(None)
_PROJ_GROUPS = ((OFF_Q, OFF_K), (OFF_K, OFF_GA), (OFF_GA, OFF_XL), (OFF_XL, OFF_GL), (OFF_GL, D_IN))


def _sigmoid(x):
    return 1.0 / (1.0 + jnp.exp2(x * NEG_LOG2_E))


def _in_proj_substeps(col_starts):
    return [(c, k) for c in col_starts for k in range(0, D_MODEL, K_CHUNK)]


def _in_proj_substep(xb, w_ref, b_ref, z, col, k0):
    cols = slice(col, col + N_TILE)
    part = jnp.dot(xb[:, k0:k0 + K_CHUNK], w_ref[k0:k0 + K_CHUNK, cols], preferred_element_type=F32)
    z[_ALL_ROWS, cols] = part + (b_ref[:, cols] if k0 == 0 else z[_ALL_ROWS, cols])


def _bias_table(nq, kv, key0_pos, q0_pos, n_keys):
    rows = GQA_GROUP * nq
    r = lax.broadcasted_iota(jnp.int32, (rows, KEY_PAD), 0)
    c = lax.broadcasted_iota(jnp.int32, (rows, KEY_PAD), 1)
    shift = nq.bit_length() - 1
    g = jnp.right_shift(r, shift)
    qpos = q0_pos + (r & (nq - 1))
    kpos = key0_pos + c
    slope = jnp.full((rows, KEY_PAD), _SLOPES[kv * GQA_GROUP], F32)
    for gg in range(1, GQA_GROUP):
        slope = jnp.where(g == gg, _SLOPES[kv * GQA_GROUP + gg], slope)
    dist = jnp.abs(qpos - kpos).astype(F32)
    cshift = CHUNK.bit_length() - 1
    qc = jnp.right_shift(qpos, cshift)
    kc = jnp.right_shift(kpos, cshift)
    valid = (kpos >= 0) & (kc <= qc) & (kc >= qc - WINDOW_CHUNKS) & (c < n_keys)
    return jnp.where(valid, -(slope * dist), NEG_INF)


def _tile_heads(x, kv, copies):
    pair = x[:, (kv // 2) * LANES:(kv // 2 + 1) * LANES]
    rolled = pltpu.roll(pair, HEAD_DIM, axis=1)
    lane = lax.broadcasted_iota(jnp.int32, pair.shape, 1)
    low = lane < HEAD_DIM
    both = jnp.where(low, pair, rolled) if kv % 2 == 0 else jnp.where(low, rolled, pair)
    both = both.astype(BF16)
    return jnp.concatenate([both] * (copies * HEAD_DIM // LANES), axis=1)


def _sink_table(nq, kv, sinks_ref):
    return jnp.concatenate(
        [jnp.full((nq, LANES), sinks_ref[kv * GQA_GROUP + g], F32) for g in range(GQA_GROUP)], axis=0)


def _head_of_lane(nq):
    lane = lax.broadcasted_iota(jnp.int32, (nq, GROUP_W), 1)
    return jnp.right_shift(lane, HEAD_DIM.bit_length() - 1)


def _attn_scores(q, k4):
    nq = q.shape[0]
    qb = (q * (HEAD_DIM ** -0.5)).astype(BF16)
    head_of_lane = _head_of_lane(nq)
    zero = jnp.zeros_like(qb)
    qs = jnp.concatenate([jnp.where(head_of_lane == g, qb, zero) for g in range(GQA_GROUP)], axis=0)
    return lax.dot_general(qs, k4, (((1,), (1,)), ((), ())), preferred_element_type=F32)


def _attn_finish(scores, v4e, bias, sink):
    nq = scores.shape[0] // GQA_GROUP
    s = scores + bias
    m = jnp.maximum(jnp.broadcast_to(jnp.max(s, axis=-1, keepdims=True), sink.shape), sink)
    e = jnp.concatenate(
        [jnp.exp(s[:, j * LANES:(j + 1) * LANES] - m) for j in range(KEY_PAD // LANES)], axis=1)
    o4e = jnp.dot(e.astype(BF16), v4e, preferred_element_type=F32)
    o2 = o4e[:, :LANES] * (1.0 / (o4e[:, LANES:] + jnp.exp(sink - m)))
    low = lax.broadcasted_iota(jnp.int32, (nq, LANES), 1) < HEAD_DIM
    return jnp.concatenate(
        [jnp.where(low, o2[g * nq:(g + 1) * nq], o2[(g + 1) * nq:(g + 2) * nq]) for g in range(0, GQA_GROUP, 2)],
        axis=1)


def _lru(z, u_out, t_rows, xbuf, a_s, b_s, hc, cw_ref, cb_ref, wg_ref, bg_ref, lam_ref, fill=None):
    interleaved = fill is not None
    fill = fill or (lambda: None)
    neg = -lam_ref[...]
    softplus = jnp.maximum(neg, 0.0) + jnp.log1p(jnp.exp(-jnp.abs(neg)))
    coef = (LRU_C * NEG_LOG2_E) * softplus
    tiles = [slice(j * GATE_TILE, (j + 1) * GATE_TILE) for j in range(N_GATE_TILES)]

    def conv_and_gate_matmul(cols, j):
        xl = z[_ALL_ROWS, OFF_XL + cols.start:OFF_XL + cols.stop]
        xbuf[SUBLANES:SUBLANES + t_rows, cols] = xl
        xc = cb_ref[:, cols] + cw_ref[CONV_WIDTH - 1:CONV_WIDTH, cols] * xl
        for k in range(CONV_WIDTH - 1):
            off = SUBLANES - (CONV_WIDTH - 1) + k
            xc = xc + cw_ref[k:k + 1, cols] * xbuf[off:off + t_rows, cols]
        pre = jnp.dot(xc.astype(BF16), wg_ref[j], preferred_element_type=F32)
        a_s[:, cols] = pre[:, :GATE_TILE]
        b_s[:, cols] = pre[:, GATE_TILE:]
        return xc

    part_rows = t_rows // GATE_ROW_PARTS if interleaved else t_rows

    def gate_math(cols, xc):
        for r0 in range(0, t_rows, part_rows):
            rows = slice(r0, r0 + part_rows)
            r = _sigmoid(a_s[rows, cols] + bg_ref[0:1, cols])
            i = _sigmoid(b_s[rows, cols] + bg_ref[1:2, cols])
            a = jnp.exp2(coef[:, cols] * r)
            gap = 1.0 - a * a
            mult = gap * lax.rsqrt(jnp.maximum(gap, TINY))
            a_s[rows, cols] = a
            b_s[rows, cols] = mult * (i * xc[rows])
            fill()

    fill()
    xc_prev = None
    for j, cols in enumerate(tiles):
        xc = conv_and_gate_matmul(cols, j)
        fill()
        if xc_prev is not None:
            gate_math(tiles[j - 1], xc_prev)
        xc_prev = xc
    gate_math(tiles[-1], xc_prev)

    row = lax.broadcasted_iota(jnp.int32, (SUBLANES, D_LRU), 0)
    h_prev = hc[...]
    for gi in range(t_rows // SUBLANES):
        rows = slice(gi * SUBLANES, (gi + 1) * SUBLANES)
        a8 = a_s[rows, :]
        b8 = b_s[rows, :]
        for d in (1, 2, 4):
            keep = row >= d
            a_sh = pltpu.roll(a8, d, axis=0)
            b_sh = pltpu.roll(b8, d, axis=0)
            b8 = jnp.where(keep, a8 * b_sh + b8, b8)
            a8 = jnp.where(keep, a8 * a_sh, a8)
        h8 = a8 * h_prev + b8
        b_s[rows, :] = h8
        h_prev = jnp.broadcast_to(h8[SUBLANES - 1:SUBLANES, :], (SUBLANES, D_LRU))
    hc[...] = h_prev
    gl = z[_ALL_ROWS, OFF_GL:OFF_GL + D_LRU]
    u_out[...] = (b_s[...] * (gl * _sigmoid(gl))).astype(u_out.dtype)
    xbuf[0:SUBLANES, :] = xbuf[t_rows:t_rows + SUBLANES, :]


def _prompt_kernel(sinks_ref, x_ref, win_ref, bin_ref, cw_ref, cb_ref, wg_ref, bg_ref, lam_ref, wout_ref,
                   ua_ref, ul_ref, kwin_ref, vwin_ref, conv_ref, hout_ref, wout_bf_ref,
                   zq, zkv, zga, zxl, zgl, bias_s, mask_s, sink_s, s_ring, k4_s, v4_s, xbuf, a_s, b_s, hc,
                   *, n_t, n_tiles):
    wout_bf_ref[...] = wout_ref[...].astype(BF16)
    z = _ColGroups([(first, end, ref) for (first, end), ref in zip(_PROJ_GROUPS, (zq, zkv, zga, zxl, zgl))])
    ua = ua_ref.at[0]
    s = pl.program_id(0)
    t_a = lax.rem(jnp.minimum(s, n_tiles - 1), n_t)
    t_l = lax.rem(jnp.maximum(s - 1, 0), n_t)

    @pl.when(s == 0)
    def _init():
        for kv in range(N_KV_HEADS):
            bias_s[kv] = _bias_table(CHUNK, kv, 0, WINDOW, WINDOW + CHUNK)
            sink_s[kv] = _sink_table(CHUNK, kv, sinks_ref)
        col = lax.broadcasted_iota(jnp.int32, (SUBLANES, KEY_PAD), 1)
        for ci in range(WINDOW_CHUNKS):
            mask_s[ci] = jnp.where(col < WINDOW - ci * CHUNK, NEG_INF, 0.0)
        k4_s[...] = jnp.zeros_like(k4_s)
        v4_s[:, :, 0:LANES] = jnp.zeros((N_KV_HEADS, KBUF_ROWS, LANES), BF16)
        v4_s[:, :, LANES:] = jnp.ones((N_KV_HEADS, KBUF_ROWS, LANES), BF16)
        zxl[...] = jnp.zeros_like(zxl)
        zgl[...] = jnp.zeros_like(zgl)

    @pl.when(t_a == 0)
    def _reset_attention_carry():
        k4_s[:, 0:WINDOW, :] = jnp.zeros((N_KV_HEADS, WINDOW, GROUP_W), BF16)
        v4_s[:, 0:WINDOW, 0:LANES] = jnp.zeros((N_KV_HEADS, WINDOW, LANES), BF16)

    @pl.when(t_l == 0)
    def _reset_lru_carry():
        xbuf[0:SUBLANES, :] = jnp.zeros((SUBLANES, D_LRU), F32)
        hc[...] = jnp.zeros_like(hc)

    xb = x_ref[0].astype(BF16)
    early_cols = _in_proj_substeps(list(range(OFF_K, OFF_GA, N_TILE)) + list(range(OFF_Q, OFF_K, N_TILE)) + list(
        range(OFF_GA, OFF_XL, N_TILE)))
    late_cols = _in_proj_substeps(range(OFF_XL, D_IN, N_TILE))

    def project(steps):
        if steps:
            _in_proj_substep(xb, win_ref, bin_ref, z, *steps.pop(0))

    _lru(z, ul_ref.at[0], ROW_TILE, xbuf, a_s, b_s, hc, cw_ref, cb_ref, wg_ref, bg_ref, lam_ref,
         fill=lambda: project(early_cols))
    while early_cols:
        project(early_cols)

    k_new = z[_ALL_ROWS, OFF_K:OFF_K + D_KV]
    v_new = z[_ALL_ROWS, OFF_V:OFF_V + D_KV]
    for kv in range(N_KV_HEADS):
        k4_s[kv, WINDOW:WINDOW + ROW_TILE, :] = _tile_heads(k_new, kv, GQA_GROUP)
        v4_s[kv, WINDOW:WINDOW + ROW_TILE, 0:LANES] = _tile_heads(v_new, kv, 2)

    units = [(ci, kv) for ci in range(N_CHUNKS_PER_TILE) for kv in range(N_KV_HEADS)]

    def scores_into_ring(idx):
        ci, kv = units[idx]
        s_ring[idx % SCORE_RING] = _attn_scores(
            z[ci * CHUNK:(ci + 1) * CHUNK, OFF_Q + kv * GROUP_W:OFF_Q + (kv + 1) * GROUP_W],
            k4_s[kv, ci * CHUNK:ci * CHUNK + KEY_PAD, :])

    mask_rows = [jnp.where(t_a == 0, mask_s[ci, 0:1, :], 0.0) for ci in range(WINDOW_CHUNKS)]
    late_per_unit = -(-len(late_cols) // len(units))
    for idx in range(SCORE_LOOKAHEAD):
        scores_into_ring(idx)
    for idx, (ci, kv) in enumerate(units):
        if idx + SCORE_LOOKAHEAD < len(units):
            scores_into_ring(idx + SCORE_LOOKAHEAD)
        for _ in range(late_per_unit):
            project(late_cols)
        s_cur = s_ring[idx % SCORE_RING]
        rows = slice(ci * CHUNK, (ci + 1) * CHUNK)
        bias = bias_s[kv] + mask_rows[ci] if ci < WINDOW_CHUNKS else bias_s[kv]
        o = _attn_finish(s_cur, v4_s[kv, ci * CHUNK:ci * CHUNK + KEY_PAD, :], bias, sink_s[kv])
        ga = z[rows, OFF_GA + kv * GROUP_W:OFF_GA + (kv + 1) * GROUP_W]
        ua[rows, kv * GROUP_W:(kv + 1) * GROUP_W] = (o * (ga * _sigmoid(ga))).astype(ua.dtype)
    assert not early_cols and not late_cols

    @pl.when(t_l == n_t - 1)
    def _emit_lru_state():
        seq_l = lax.div(jnp.maximum(s - 1, 0), n_t)
        for j in range(CONV_WIDTH - 1):
            row = SUBLANES - (CONV_WIDTH - 1) + j
            conv_ref[j, pl.ds(seq_l, 1), :] = xbuf[row:row + 1, :]
        hout_ref[pl.ds(seq_l, 1), :] = hc[0:1, :]

    @pl.when(t_a == n_t - 1)
    def _emit_kv_window():
        kwin_ref[0] = z[slice(ROW_TILE - WINDOW, ROW_TILE), OFF_K:OFF_K + D_KV].T
        vwin_ref[0] = z[slice(ROW_TILE - WINDOW, ROW_TILE), OFF_V:OFF_V + D_KV].T

    @pl.when(s < n_tiles - 1)
    def _carry_kv():
        k4_s[:, 0:WINDOW, :] = k4_s[:, ROW_TILE:ROW_TILE + WINDOW, :]
        v4_s[:, 0:WINDOW, 0:LANES] = v4_s[:, ROW_TILE:ROW_TILE + WINDOW, 0:LANES]


def _prompt_mixer(x3, sinks, w_in_bf, b_in_row, cw, cb, wg, bg, lam, w_out):
    bsz, seq, _ = x3.shape
    n_t = seq // ROW_TILE
    n_tiles = bsz * n_t
    smem = pl.BlockSpec(memory_space=pltpu.SMEM)
    assert D_MODEL % n_tiles == 0 and (D_MODEL // n_tiles) % (2 * SUBLANES) == 0
    w_out_slab = pl.BlockSpec((D_MODEL // n_tiles, D_MODEL), lambda s: (jnp.minimum(s, n_tiles - 1), 0))

    def attn_tile(s):
        ta = jnp.minimum(s, n_tiles - 1)
        return ta // n_t, ta % n_t

    def lru_tile(s):
        tl = jnp.maximum(s - 1, 0)
        return tl // n_t, tl % n_t

    return pl.pallas_call(
        functools.partial(_prompt_kernel, n_t=n_t, n_tiles=n_tiles),
        grid=(n_tiles + 1,),
        in_specs=[smem,
                  pl.BlockSpec((1, ROW_TILE, D_MODEL), lambda s: (*attn_tile(s), 0)),
                  _resident((D_MODEL, D_IN)), _resident((1, D_IN)),
                  _resident((CONV_WIDTH, D_LRU)), _resident((1, D_LRU)),
                  _resident((N_GATE_TILES, GATE_TILE, 2 * GATE_TILE)),
                  _resident((2, D_LRU)), _resident((1, D_LRU)), w_out_slab],
        out_specs=[pl.BlockSpec((1, ROW_TILE, D_ATTN), lambda s: (*attn_tile(s), 0)),
                   pl.BlockSpec((1, ROW_TILE, D_LRU), lambda s: (*lru_tile(s), 0)),
                   pl.BlockSpec((1, D_KV, WINDOW), lambda s: (attn_tile(s)[0], 0, 0)),
                   pl.BlockSpec((1, D_KV, WINDOW), lambda s: (attn_tile(s)[0], 0, 0)),
                   pl.BlockSpec((CONV_WIDTH - 1, bsz, D_LRU), lambda s: (0, 0, 0)),
                   pl.BlockSpec((bsz, D_LRU), lambda s: (0, 0)), w_out_slab],
        out_shape=[jax.ShapeDtypeStruct((bsz, seq, D_ATTN), BF16),
                   jax.ShapeDtypeStruct((bsz, seq, D_LRU), BF16),
                   jax.ShapeDtypeStruct((bsz, D_KV, WINDOW), F32),
                   jax.ShapeDtypeStruct((bsz, D_KV, WINDOW), F32),
                   jax.ShapeDtypeStruct((CONV_WIDTH - 1, bsz, D_LRU), F32),
                   jax.ShapeDtypeStruct((bsz, D_LRU), F32),
                   jax.ShapeDtypeStruct((D_MODEL, D_MODEL), BF16)],
        scratch_shapes=[pltpu.VMEM((ROW_TILE, end - first), F32) for first, end in _PROJ_GROUPS] + [
                        pltpu.VMEM((N_KV_HEADS, GQA_GROUP * CHUNK, KEY_PAD), F32),
                        pltpu.VMEM((WINDOW_CHUNKS, SUBLANES, KEY_PAD), F32),
                        pltpu.VMEM((N_KV_HEADS, GQA_GROUP * CHUNK, LANES), F32),
                        pltpu.VMEM((SCORE_RING, GQA_GROUP * CHUNK, KEY_PAD), F32),
                        pltpu.VMEM((N_KV_HEADS, KBUF_ROWS, GROUP_W), BF16),
                        pltpu.VMEM((N_KV_HEADS, KBUF_ROWS, V_EXT_W), BF16),
                        pltpu.VMEM((SUBLANES + ROW_TILE, D_LRU), F32),
                        pltpu.VMEM((ROW_TILE, D_LRU), F32),
                        pltpu.VMEM((ROW_TILE, D_LRU), F32),
                        pltpu.VMEM((SUBLANES, D_LRU), F32)],
        compiler_params=pltpu.CompilerParams(
            dimension_semantics=("arbitrary",), vmem_limit_bytes=VMEM_LIMIT_BYTES),
        name="prompt_mixer",
    )(sinks, x3, w_in_bf, b_in_row, cw, cb, wg, bg, lam, w_out)


def _out_proj_kernel(ua_ref, ul_ref, x_ref, w_ref, g_ref, b_ref, y_ref):
    tm = y_ref.shape[0]
    part = min(MXU_DIM, tm)
    n_parts = tm // part
    n_chunks = D_MODEL // N_TILE
    ln_rows = part // n_chunks

    def matmul_chunk(p, n):
        rows = slice(p * part, (p + 1) * part)
        cols = slice(n * N_TILE, (n + 1) * N_TILE)
        y_ref[rows, cols] = (
            DEEPNORM_ALPHA * x_ref[rows, cols]
            + jnp.dot(ua_ref[rows, :], w_ref[0:D_ATTN, cols], preferred_element_type=F32)
            + jnp.dot(ul_ref[rows, :], w_ref[D_ATTN:, cols], preferred_element_type=F32))

    def layer_norm_rows(p, r):
        rows = slice(p * part + r * ln_rows, p * part + (r + 1) * ln_rows)
        y = y_ref[rows, :]
        mu = jnp.mean(y, axis=-1, keepdims=True)
        yc = y - mu
        var = jnp.mean(yc * yc, axis=-1, keepdims=True)
        y_ref[rows, :] = yc * lax.rsqrt(var + LN_EPS) * g_ref[...] + b_ref[...]

    for n in range(n_chunks):
        matmul_chunk(0, n)
    for p in range(1, n_parts):
        for n in range(n_chunks):
            matmul_chunk(p, n)
            layer_norm_rows(p - 1, n)
    for r in range(n_chunks):
        layer_norm_rows(n_parts - 1, r)


def _out_proj(ua2d, ul2d, x2d, w_bf, g_row, b_row):
    m = x2d.shape[0]
    tm = min(OUT_ROW_TILE, m)
    return pl.pallas_call(
        _out_proj_kernel,
        grid=(m // tm,),
        in_specs=[pl.BlockSpec((tm, D_ATTN), lambda i: (i, 0)),
                  pl.BlockSpec((tm, D_LRU), lambda i: (i, 0)),
                  pl.BlockSpec((tm, D_MODEL), lambda i: (i, 0)),
                  _resident((D_MODEL, D_MODEL)),
                  _resident((1, D_MODEL)),
                  _resident((1, D_MODEL))],
        out_specs=pl.BlockSpec((tm, D_MODEL), lambda i: (i, 0)),
        out_shape=jax.ShapeDtypeStruct((m, D_MODEL), F32),
        compiler_params=pltpu.CompilerParams(
            dimension_semantics=("arbitrary",), vmem_limit_bytes=VMEM_LIMIT_BYTES),
        name="out_proj",
    )(ua2d, ul2d, x2d, w_bf, g_row, b_row)


def _in_proj_kernel(x_ref, w_ref, b_ref, z_ref, w_bf_ref):
    @pl.when(pl.program_id(0) == 0)
    def _start_from_bias():
        z_ref[...] = jnp.broadcast_to(b_ref[...], z_ref.shape)

    xb = x_ref[...].astype(BF16)
    for n in range(0, D_IN, N_TILE):
        wb = w_ref[:, n:n + N_TILE].astype(BF16)
        w_bf_ref[:, n:n + N_TILE] = wb
        z_ref[:, n:n + N_TILE] += jnp.dot(xb, wb, preferred_element_type=F32)


def _in_proj(x2d, w, b_row):
    m = x2d.shape[0]
    return pl.pallas_call(
        _in_proj_kernel,
        grid=(D_MODEL // MXU_DIM,),
        in_specs=[pl.BlockSpec((m, MXU_DIM), lambda k: (0, k)),
                  pl.BlockSpec((MXU_DIM, D_IN), lambda k: (k, 0)),
                  pl.BlockSpec((1, D_IN), lambda k: (0, 0))],
        out_specs=[pl.BlockSpec((m, D_IN), lambda k: (0, 0)),
                   pl.BlockSpec((MXU_DIM, D_IN), lambda k: (k, 0))],
        out_shape=[jax.ShapeDtypeStruct((m, D_IN), F32),
                   jax.ShapeDtypeStruct((D_MODEL, D_IN), BF16)],
        compiler_params=pltpu.CompilerParams(
            dimension_semantics=("arbitrary",), vmem_limit_bytes=VMEM_LIMIT_BYTES),
        name="in_proj",
    )(x2d, w, b_row)


def _sample_mixer_kernel(sinks_ref, z_ref, ck_ref, cv_ref, sc_ref, sh_ref,
                         cw_ref, cb_ref, wg_ref, bg_ref, lam_ref,
                         ua_ref, ul_ref, kwin_ref, vwin_ref, conv_ref, hout_ref,
                         bias_s, sink_s, s_ring, k4_s, v4_s, xbuf, a_s, b_s, hc, *, t_rows, n_cache):
    b = pl.program_id(0)
    n_keys = n_cache + t_rows
    z = _ColGroups([(0, D_IN, z_ref.at[0])])
    ua = ua_ref.at[0]

    @pl.when(b == 0)
    def _init_tables():
        for kv in range(N_KV_HEADS):
            bias_s[kv] = _bias_table(t_rows, kv, PAST_LEN - n_cache, PAST_LEN, n_keys)
            sink_s[kv] = _sink_table(t_rows, kv, sinks_ref)
        k4_s[...] = jnp.zeros_like(k4_s)
        v4_s[:, :, 0:LANES] = jnp.zeros((N_KV_HEADS, KEY_PAD, LANES), BF16)
        v4_s[:, :, LANES:] = jnp.ones((N_KV_HEADS, KEY_PAD, LANES), BF16)

    k_all = jnp.concatenate([ck_ref[0], z[_ALL_ROWS, OFF_K:OFF_K + D_KV]], axis=0)
    v_all = jnp.concatenate([cv_ref[0], z[_ALL_ROWS, OFF_V:OFF_V + D_KV]], axis=0)
    for kv in range(N_KV_HEADS):
        k4_s[kv, 0:n_keys, :] = _tile_heads(k_all, kv, GQA_GROUP)
        v4_s[kv, 0:n_keys, 0:LANES] = _tile_heads(v_all, kv, 2)
    for kv in range(N_KV_HEADS):
        s_ring[kv] = _attn_scores(z[_ALL_ROWS, OFF_Q + kv * GROUP_W:OFF_Q + (kv + 1) * GROUP_W], k4_s[kv])
    for kv in range(N_KV_HEADS):
        o = _attn_finish(s_ring[kv], v4_s[kv], bias_s[kv], sink_s[kv])
        ga = z[_ALL_ROWS, OFF_GA + kv * GROUP_W:OFF_GA + (kv + 1) * GROUP_W]
        ua[:, kv * GROUP_W:(kv + 1) * GROUP_W] = (o * (ga * _sigmoid(ga))).astype(ua.dtype)
    kwin_ref[0] = k_all[n_keys - WINDOW:, :]
    vwin_ref[0] = v_all[n_keys - WINDOW:, :]

    xbuf[0:SUBLANES, :] = jnp.zeros((SUBLANES, D_LRU), F32)
    for j in range(CONV_WIDTH - 1):
        row = SUBLANES - (CONV_WIDTH - 1) + j
        xbuf[row:row + 1, :] = sc_ref[j, pl.ds(b, 1), :]
    hc[...] = jnp.broadcast_to(sh_ref[pl.ds(b, 1), :], (SUBLANES, D_LRU))
    _lru(z, ul_ref.at[0], t_rows, xbuf, a_s, b_s, hc, cw_ref, cb_ref, wg_ref, bg_ref, lam_ref)
    for j in range(CONV_WIDTH - 1):
        row = SUBLANES - (CONV_WIDTH - 1) + j
        conv_ref[j, pl.ds(b, 1), :] = xbuf[row:row + 1, :]
    hout_ref[pl.ds(b, 1), :] = hc[0:1, :]


def _sample_mixer(z3, cache_k, cache_v, state_conv, state_h, sinks, cw, cb, wg, bg, lam):
    bsz, t_rows, _ = z3.shape
    n_cache = cache_k.shape[1]
    assert n_cache + t_rows <= KEY_PAD and n_cache + t_rows >= WINDOW
    assert t_rows % SUBLANES == 0 and t_rows & (t_rows - 1) == 0 and t_rows >= CONV_WIDTH - 1
    smem = pl.BlockSpec(memory_space=pltpu.SMEM)
    per_b = lambda shape: pl.BlockSpec((1,) + shape, lambda b: (b, 0, 0))
    return pl.pallas_call(
        functools.partial(_sample_mixer_kernel, t_rows=t_rows, n_cache=n_cache),
        grid=(bsz,),
        in_specs=[smem, per_b((t_rows, D_IN)), per_b((n_cache, D_KV)), per_b((n_cache, D_KV)),
                  _resident((CONV_WIDTH - 1, bsz, D_LRU)), _resident((bsz, D_LRU)),
                  _resident((CONV_WIDTH, D_LRU)), _resident((1, D_LRU)),
                  _resident((N_GATE_TILES, GATE_TILE, 2 * GATE_TILE)),
                  _resident((2, D_LRU)), _resident((1, D_LRU))],
        out_specs=[per_b((t_rows, D_ATTN)), per_b((t_rows, D_LRU)), per_b((WINDOW, D_KV)), per_b((WINDOW, D_KV)),
                   pl.BlockSpec((CONV_WIDTH - 1, bsz, D_LRU), lambda b: (0, 0, 0)),
                   pl.BlockSpec((bsz, D_LRU), lambda b: (0, 0))],
        out_shape=[jax.ShapeDtypeStruct((bsz, t_rows, D_ATTN), BF16),
                   jax.ShapeDtypeStruct((bsz, t_rows, D_LRU), BF16),
                   jax.ShapeDtypeStruct((bsz, WINDOW, D_KV), F32),
                   jax.ShapeDtypeStruct((bsz, WINDOW, D_KV), F32),
                   jax.ShapeDtypeStruct((CONV_WIDTH - 1, bsz, D_LRU), F32),
                   jax.ShapeDtypeStruct((bsz, D_LRU), F32)],
        scratch_shapes=[pltpu.VMEM((N_KV_HEADS, GQA_GROUP * t_rows, KEY_PAD), F32),
                        pltpu.VMEM((N_KV_HEADS, GQA_GROUP * t_rows, LANES), F32),
                        pltpu.VMEM((N_KV_HEADS, GQA_GROUP * t_rows, KEY_PAD), F32),
                        pltpu.VMEM((N_KV_HEADS, KEY_PAD, GROUP_W), BF16),
                        pltpu.VMEM((N_KV_HEADS, KEY_PAD, V_EXT_W), BF16),
                        pltpu.VMEM((SUBLANES + t_rows, D_LRU), F32),
                        pltpu.VMEM((t_rows, D_LRU), F32),
                        pltpu.VMEM((t_rows, D_LRU), F32),
                        pltpu.VMEM((SUBLANES, D_LRU), F32)],
        compiler_params=pltpu.CompilerParams(
            dimension_semantics=("arbitrary",), vmem_limit_bytes=VMEM_LIMIT_BYTES),
        name="sample_mixer",
    )(sinks, z3, cache_k, cache_v, state_conv, state_h, cw, cb, wg, bg, lam)


def _gate_weight_tiles(w_a, w_x):
    per = GATE_TILE // LRU_BLOCK
    row_block = lax.broadcasted_iota(jnp.int32, (GATE_TILE, GATE_TILE), 0) // LRU_BLOCK
    col_block = lax.broadcasted_iota(jnp.int32, (GATE_TILE, GATE_TILE), 1) // LRU_BLOCK

    def block_diagonal(w):
        rows = w.reshape(N_GATE_TILES, GATE_TILE, LRU_BLOCK)
        return jnp.where(row_block == col_block, jnp.concatenate([rows] * per, axis=-1), 0.0)

    return jnp.concatenate([block_diagonal(w_a), block_diagonal(w_x)], axis=-1).astype(BF16)


def kernel(x_prompt, x_sample, cache_k, cache_v, state_conv, state_h, w_in, b_in, conv_w, conv_b,
           w_gate_a, b_gate_a, w_gate_x, b_gate_x, lru_lambda, attn_sinks, w_out, ln_g, ln_b):
    assert w_in.shape[0] == DEPTH == 1
    bsz, seq, _ = x_prompt.shape
    dbsz, dseq, _ = x_sample.shape
    n_cache = cache_k.shape[2]
    n_p, n_s = bsz * seq, dbsz * dseq
    assert seq % ROW_TILE == 0 and n_p % OUT_ROW_TILE == 0

    wg = _gate_weight_tiles(w_gate_a[0], w_gate_x[0])
    bg = jnp.concatenate([b_gate_a[0], b_gate_x[0]], axis=0).reshape(2, D_LRU)
    cw = conv_w[0]
    sinks = attn_sinks[0]

    xs2 = x_sample.reshape(n_s, D_MODEL)
    zs, w_in_bf = _in_proj(xs2, w_in.reshape(D_MODEL, D_IN), b_in)
    zs = zs.reshape(dbsz, dseq, D_IN)

    uap, ulp, kp, vp, cp, hp, w_out_bf = _prompt_mixer(
        x_prompt, sinks, w_in_bf, b_in, cw, conv_b, wg, bg, lru_lambda, w_out.reshape(D_MODEL, D_MODEL))
    yp = _out_proj(uap.reshape(n_p, D_ATTN), ulp.reshape(n_p, D_LRU), x_prompt.reshape(n_p, D_MODEL),
                   w_out_bf, ln_g, ln_b).reshape(bsz, seq, D_MODEL)

    uas, uls, ks, vs, cs, hs = _sample_mixer(
        zs, cache_k[0].reshape(dbsz, n_cache, D_KV), cache_v[0].reshape(dbsz, n_cache, D_KV),
        state_conv[0].transpose(1, 0, 2), state_h[0],
        sinks, cw, conv_b, wg, bg, lru_lambda)
    ys = _out_proj(uas.reshape(n_s, D_ATTN), uls.reshape(n_s, D_LRU), xs2,
                   w_out_bf, ln_g, ln_b).reshape(dbsz, dseq, D_MODEL)

    def from_channel_major(w):
        return w.reshape(bsz, N_KV_HEADS, HEAD_DIM, WINDOW).transpose(0, 3, 1, 2)[None]

    def from_row_major(w):
        return w.reshape(1, dbsz, WINDOW, N_KV_HEADS, HEAD_DIM)

    return (yp, ys,
            from_channel_major(kp), from_channel_major(vp), cp.transpose(1, 0, 2)[None], hp[None],
            from_row_major(ks), from_row_major(vs), cs.transpose(1, 0, 2)[None], hs[None])
```

```python
import functools

import jax
import jax.numpy as jnp
from jax import lax
from jax.experimental import pallas as pl
from jax.experimental.pallas import tpu as pltpu

F32 = jnp.float32
BF16 = jnp.bfloat16

D_MODEL = 2048
DEPTH = 1
CHUNK = 64
WINDOW = 128
WINDOW_CHUNKS = WINDOW // CHUNK
HEAD_DIM = 64
D_ATTN = D_MODEL // 2
D_LRU = D_MODEL - D_ATTN
N_Q_HEADS = D_ATTN // HEAD_DIM
N_KV_HEADS = 4
GQA_GROUP = N_Q_HEADS // N_KV_HEADS
D_KV = N_KV_HEADS * HEAD_DIM
N_LRU_BLOCKS = 16
LRU_BLOCK = D_LRU // N_LRU_BLOCKS
CONV_WIDTH = 4
LRU_C = 8.0
D_IN = 2 * D_ATTN + 2 * D_KV + 2 * D_LRU
PAST_LEN = 2048
DEEPNORM_ALPHA = (2.0 * DEPTH) ** 0.25
LN_EPS = 1e-5
NEG_INF = -1e30
NEG_LOG2_E = -1.4426950408889634
TINY = 1e-30

OFF_Q = 0
OFF_K = D_ATTN
OFF_V = D_ATTN + D_KV
OFF_GA = D_ATTN + 2 * D_KV
OFF_XL = OFF_GA + D_ATTN
OFF_GL = OFF_XL + D_LRU

SUBLANES = 8
LANES = 128
MXU_DIM = 256
VMEM_LIMIT_BYTES = 56 * 1024 * 1024

GROUP_W = GQA_GROUP * HEAD_DIM
KEY_PAD = 2 * LANES
V_EXT_W = 2 * LANES
GATE_TILE = MXU_DIM
N_GATE_TILES = D_LRU // GATE_TILE
ROW_TILE = 256
OUT_ROW_TILE = 1024
SCORE_LOOKAHEAD = 2
SCORE_RING = SCORE_LOOKAHEAD + 2
N_TILE = 512
K_CHUNK = 512
GATE_ROW_PARTS = 4
N_CHUNKS_PER_TILE = ROW_TILE // CHUNK
KBUF_ROWS = WINDOW + ROW_TILE + (KEY_PAD - (WINDOW + CHUNK))

_SLOPES = tuple(2.0 ** (-8.0 * h / N_Q_HEADS) for h in range(1, N_Q_HEADS + 1))


def _resident(shape):
    return pl.BlockSpec(shape, lambda *_: (0,) * len(shape), pipeline_mode=pl.Buffered(1))


class _ColGroups:
    def __init__(self, groups):
        self._groups = groups

    def _locate(self, idx):
        rows, cols = idx
        for first, end, ref in self._groups:
            if first <= cols.start and cols.stop <= end:
                return ref, rows, slice(cols.start - first, cols.stop - first)
        raise ValueError(f"columns {cols} straddle projection groups")

    def __getitem__(self, idx):
        ref, rows, cols = self._locate(idx)
        return ref[rows, cols]

    def __setitem__(self, idx, value):
        ref, rows, cols = self._locate(idx)
        ref[rows, cols] = value


_ALL_ROWS = slice(None)
_PROJ_GROUPS = ((OFF_Q, OFF_K), (OFF_K, OFF_GA), (OFF_GA, OFF_XL), (OFF_XL, OFF_GL), (OFF_GL, D_IN))


def _sigmoid(x):
    return 1.0 / (1.0 + jnp.exp2(x * NEG_LOG2_E))


def _in_proj_substeps(col_starts):
    return [(c, k) for c in col_starts for k in range(0, D_MODEL, K_CHUNK)]


def _in_proj_substep(xb, w_ref, b_ref, z, col, k0):
    cols = slice(col, col + N_TILE)
    part = jnp.dot(xb[:, k0:k0 + K_CHUNK], w_ref[k0:k0 + K_CHUNK, cols], preferred_element_type=F32)
    z[_ALL_ROWS, cols] = part + (b_ref[:, cols] if k0 == 0 else z[_ALL_ROWS, cols])


def _bias_table(nq, kv, key0_pos, q0_pos, n_keys):
    rows = GQA_GROUP * nq
    r = lax.broadcasted_iota(jnp.int32, (rows, KEY_PAD), 0)
    c = lax.broadcasted_iota(jnp.int32, (rows, KEY_PAD), 1)
    shift = nq.bit_length() - 1
    g = jnp.right_shift(r, shift)
    qpos = q0_pos + (r & (nq - 1))
    kpos = key0_pos + c
    slope = jnp.full((rows, KEY_PAD), _SLOPES[kv * GQA_GROUP], F32)
    for gg in range(1, GQA_GROUP):
        slope = jnp.where(g == gg, _SLOPES[kv * GQA_GROUP + gg], slope)
    dist = jnp.abs(qpos - kpos).astype(F32)
    cshift = CHUNK.bit_length() - 1
    qc = jnp.right_shift(qpos, cshift)
    kc = jnp.right_shift(kpos, cshift)
    valid = (kpos >= 0) & (kc <= qc) & (kc >= qc - WINDOW_CHUNKS) & (c < n_keys)
    return jnp.where(valid, -(slope * dist), NEG_INF)


def _tile_heads(x, kv, copies):
    pair = x[:, (kv // 2) * LANES:(kv // 2 + 1) * LANES]
    rolled = pltpu.roll(pair, HEAD_DIM, axis=1)
    lane = lax.broadcasted_iota(jnp.int32, pair.shape, 1)
    low = lane < HEAD_DIM
    both = jnp.where(low, pair, rolled) if kv % 2 == 0 else jnp.where(low, rolled, pair)
    both = both.astype(BF16)
    return jnp.concatenate([both] * (copies * HEAD_DIM // LANES), axis=1)


def _sink_table(nq, kv, sinks_ref):
    return jnp.concatenate(
        [jnp.full((nq, LANES), sinks_ref[kv * GQA_GROUP + g], F32) for g in range(GQA_GROUP)], axis=0)


def _head_of_lane(nq):
    lane = lax.broadcasted_iota(jnp.int32, (nq, GROUP_W), 1)
    return jnp.right_shift(lane, HEAD_DIM.bit_length() - 1)


def _attn_scores(q, k4):
    nq = q.shape[0]
    qb = (q * (HEAD_DIM ** -0.5)).astype(BF16)
    head_of_lane = _head_of_lane(nq)
    zero = jnp.zeros_like(qb)
    qs = jnp.concatenate([jnp.where(head_of_lane == g, qb, zero) for g in range(GQA_GROUP)], axis=0)
    return lax.dot_general(qs, k4, (((1,), (1,)), ((), ())), preferred_element_type=F32)


def _attn_finish(scores, v4e, bias, sink):
    nq = scores.shape[0] // GQA_GROUP
    s = scores + bias
    m = jnp.maximum(jnp.broadcast_to(jnp.max(s, axis=-1, keepdims=True), sink.shape), sink)
    e = jnp.concatenate(
        [jnp.exp(s[:, j * LANES:(j + 1) * LANES] - m) for j in range(KEY_PAD // LANES)], axis=1)
    o4e = jnp.dot(e.astype(BF16), v4e, preferred_element_type=F32)
    o2 = o4e[:, :LANES] * (1.0 / (o4e[:, LANES:] + jnp.exp(sink - m)))
    low = lax.broadcasted_iota(jnp.int32, (nq, LANES), 1) < HEAD_DIM
    return jnp.concatenate(
        [jnp.where(low, o2[g * nq:(g + 1) * nq], o2[(g + 1) * nq:(g + 2) * nq]) for g in range(0, GQA_GROUP, 2)],
        axis=1)


def _lru(z, u_out, t_rows, xbuf, a_s, b_s, hc, cw_ref, cb_ref, wg_ref, bg_ref, lam_ref, fill=None):
    interleaved = fill is not None
    fill = fill or (lambda: None)
    neg = -lam_ref[...]
    softplus = jnp.maximum(neg, 0.0) + jnp.log1p(jnp.exp(-jnp.abs(neg)))
    coef = (LRU_C * NEG_LOG2_E) * softplus
    tiles = [slice(j * GATE_TILE, (j + 1) * GATE_TILE) for j in range(N_GATE_TILES)]

    def conv_and_gate_matmul(cols, j):
        xl = z[_ALL_ROWS, OFF_XL + cols.start:OFF_XL + cols.stop]
        xbuf[SUBLANES:SUBLANES + t_rows, cols] = xl
        xc = cb_ref[:, cols] + cw_ref[CONV_WIDTH - 1:CONV_WIDTH, cols] * xl
        for k in range(CONV_WIDTH - 1):
            off = SUBLANES - (CONV_WIDTH - 1) + k
            xc = xc + cw_ref[k:k + 1, cols] * xbuf[off:off + t_rows, cols]
        pre = jnp.dot(xc.astype(BF16), wg_ref[j], preferred_element_type=F32)
        a_s[:, cols] = pre[:, :GATE_TILE]
        b_s[:, cols] = pre[:, GATE_TILE:]
        return xc

    part_rows = t_rows // GATE_ROW_PARTS if interleaved else t_rows

    def gate_math(cols, xc):
        for r0 in range(0, t_rows, part_rows):
            rows = slice(r0, r0 + part_rows)
            r = _sigmoid(a_s[rows, cols] + bg_ref[0:1, cols])
            i = _sigmoid(b_s[rows, cols] + bg_ref[1:2, cols])
            a = jnp.exp2(coef[:, cols] * r)
            gap = 1.0 - a * a
            mult = gap * lax.rsqrt(jnp.maximum(gap, TINY))
            a_s[rows, cols] = a
            b_s[rows, cols] = mult * (i * xc[rows])
            fill()

    fill()
    xc_prev = None
    for j, cols in enumerate(tiles):
        xc = conv_and_gate_matmul(cols, j)
        fill()
        if xc_prev is not None:
            gate_math(tiles[j - 1], xc_prev)
        xc_prev = xc
    gate_math(tiles[-1], xc_prev)

    row = lax.broadcasted_iota(jnp.int32, (SUBLANES, D_LRU), 0)
    h_prev = hc[...]
    for gi in range(t_rows // SUBLANES):
        rows = slice(gi * SUBLANES, (gi + 1) * SUBLANES)
        a8 = a_s[rows, :]
        b8 = b_s[rows, :]
        for d in (1, 2, 4):
            keep = row >= d
            a_sh = pltpu.roll(a8, d, axis=0)
            b_sh = pltpu.roll(b8, d, axis=0)
            b8 = jnp.where(keep, a8 * b_sh + b8, b8)
            a8 = jnp.where(keep, a8 * a_sh, a8)
        h8 = a8 * h_prev + b8
        b_s[rows, :] = h8
        h_prev = jnp.broadcast_to(h8[SUBLANES - 1:SUBLANES, :], (SUBLANES, D_LRU))
    hc[...] = h_prev
    gl = z[_ALL_ROWS, OFF_GL:OFF_GL + D_LRU]
    u_out[...] = (b_s[...] * (gl * _sigmoid(gl))).astype(u_out.dtype)
    xbuf[0:SUBLANES, :] = xbuf[t_rows:t_rows + SUBLANES, :]


def _prompt_kernel(sinks_ref, x_ref, win_ref, bin_ref, cw_ref, cb_ref, wg_ref, bg_ref, lam_ref, wout_ref,
                   ua_ref, ul_ref, kwin_ref, vwin_ref, conv_ref, hout_ref, wout_bf_ref,
                   zq, zkv, zga, zxl, zgl, bias_s, mask_s, sink_s, s_ring, k4_s, v4_s, xbuf, a_s, b_s, hc,
                   *, n_t, n_tiles):
    wout_bf_ref[...] = wout_ref[...].astype(BF16)
    z = _ColGroups([(first, end, ref) for (first, end), ref in zip(_PROJ_GROUPS, (zq, zkv, zga, zxl, zgl))])
    ua = ua_ref.at[0]
    s = pl.program_id(0)
    t_a = lax.rem(jnp.minimum(s, n_tiles - 1), n_t)
    t_l = lax.rem(jnp.maximum(s - 1, 0), n_t)

    @pl.when(s == 0)
    def _init():
        for kv in range(N_KV_HEADS):
            bias_s[kv] = _bias_table(CHUNK, kv, 0, WINDOW, WINDOW + CHUNK)
            sink_s[kv] = _sink_table(CHUNK, kv, sinks_ref)
        col = lax.broadcasted_iota(jnp.int32, (SUBLANES, KEY_PAD), 1)
        for ci in range(WINDOW_CHUNKS):
            mask_s[ci] = jnp.where(col < WINDOW - ci * CHUNK, NEG_INF, 0.0)
        k4_s[...] = jnp.zeros_like(k4_s)
        v4_s[:, :, 0:LANES] = jnp.zeros((N_KV_HEADS, KBUF_ROWS, LANES), BF16)
        v4_s[:, :, LANES:] = jnp.ones((N_KV_HEADS, KBUF_ROWS, LANES), BF16)
        zxl[...] = jnp.zeros_like(zxl)
        zgl[...] = jnp.zeros_like(zgl)

    @pl.when(t_a == 0)
    def _reset_attention_carry():
        k4_s[:, 0:WINDOW, :] = jnp.zeros((N_KV_HEADS, WINDOW, GROUP_W), BF16)
        v4_s[:, 0:WINDOW, 0:LANES] = jnp.zeros((N_KV_HEADS, WINDOW, LANES), BF16)

    @pl.when(t_l == 0)
    def _reset_lru_carry():
        xbuf[0:SUBLANES, :] = jnp.zeros((SUBLANES, D_LRU), F32)
        hc[...] = jnp.zeros_like(hc)

    xb = x_ref[0].astype(BF16)
    early_cols = _in_proj_substeps(list(range(OFF_K, OFF_GA, N_TILE)) + list(range(OFF_Q, OFF_K, N_TILE)) + list(
        range(OFF_GA, OFF_XL, N_TILE)))
    late_cols = _in_proj_substeps(range(OFF_XL, D_IN, N_TILE))

    def project(steps):
        if steps:
            _in_proj_substep(xb, win_ref, bin_ref, z, *steps.pop(0))

    _lru(z, ul_ref.at[0], ROW_TILE, xbuf, a_s, b_s, hc, cw_ref, cb_ref, wg_ref, bg_ref, lam_ref,
         fill=lambda: project(early_cols))
    while early_cols:
        project(early_cols)

    k_new = z[_ALL_ROWS, OFF_K:OFF_K + D_KV]
    v_new = z[_ALL_ROWS, OFF_V:OFF_V + D_KV]
    for kv in range(N_KV_HEADS):
        k4_s[kv, WINDOW:WINDOW + ROW_TILE, :] = _tile_heads(k_new, kv, GQA_GROUP)
        v4_s[kv, WINDOW:WINDOW + ROW_TILE, 0:LANES] = _tile_heads(v_new, kv, 2)

    units = [(ci, kv) for ci in range(N_CHUNKS_PER_TILE) for kv in range(N_KV_HEADS)]

    def scores_into_ring(idx):
        ci, kv = units[idx]
        s_ring[idx % SCORE_RING] = _attn_scores(
            z[ci * CHUNK:(ci + 1) * CHUNK, OFF_Q + kv * GROUP_W:OFF_Q + (kv + 1) * GROUP_W],
            k4_s[kv, ci * CHUNK:ci * CHUNK + KEY_PAD, :])

    mask_rows = [jnp.where(t_a == 0, mask_s[ci, 0:1, :], 0.0) for ci in range(WINDOW_CHUNKS)]
    late_per_unit = -(-len(late_cols) // len(units))
    for idx in range(SCORE_LOOKAHEAD):
        scores_into_ring(idx)
    for idx, (ci, kv) in enumerate(units):
        if idx + SCORE_LOOKAHEAD < len(units):
            scores_into_ring(idx + SCORE_LOOKAHEAD)
        for _ in range(late_per_unit):
            project(late_cols)
        s_cur = s_ring[idx % SCORE_RING]
        rows = slice(ci * CHUNK, (ci + 1) * CHUNK)
        bias = bias_s[kv] + mask_rows[ci] if ci < WINDOW_CHUNKS else bias_s[kv]
        o = _attn_finish(s_cur, v4_s[kv, ci * CHUNK:ci * CHUNK + KEY_PAD, :], bias, sink_s[kv])
        ga = z[rows, OFF_GA + kv * GROUP_W:OFF_GA + (kv + 1) * GROUP_W]
        ua[rows, kv * GROUP_W:(kv + 1) * GROUP_W] = (o * (ga * _sigmoid(ga))).astype(ua.dtype)
    assert not early_cols and not late_cols

    @pl.when(t_l == n_t - 1)
    def _emit_lru_state():
        seq_l = lax.div(jnp.maximum(s - 1, 0), n_t)
        for j in range(CONV_WIDTH - 1):
            row = SUBLANES - (CONV_WIDTH - 1) + j
            conv_ref[j, pl.ds(seq_l, 1), :] = xbuf[row:row + 1, :]
        hout_ref[pl.ds(seq_l, 1), :] = hc[0:1, :]

    @pl.when(t_a == n_t - 1)
    def _emit_kv_window():
        kwin_ref[0] = z[slice(ROW_TILE - WINDOW, ROW_TILE), OFF_K:OFF_K + D_KV].T
        vwin_ref[0] = z[slice(ROW_TILE - WINDOW, ROW_TILE), OFF_V:OFF_V + D_KV].T

    @pl.when(s < n_tiles - 1)
    def _carry_kv():
        k4_s[:, 0:WINDOW, :] = k4_s[:, ROW_TILE:ROW_TILE + WINDOW, :]
        v4_s[:, 0:WINDOW, 0:LANES] = v4_s[:, ROW_TILE:ROW_TILE + WINDOW, 0:LANES]


def _prompt_mixer(x3, sinks, w_in_bf, b_in_row, cw, cb, wg, bg, lam, w_out):
    bsz, seq, _ = x3.shape
    n_t = seq // ROW_TILE
    n_tiles = bsz * n_t
    smem = pl.BlockSpec(memory_space=pltpu.SMEM)
    assert D_MODEL % n_tiles == 0 and (D_MODEL // n_tiles) % (2 * SUBLANES) == 0
    w_out_slab = pl.BlockSpec((D_MODEL // n_tiles, D_MODEL), lambda s: (jnp.minimum(s, n_tiles - 1), 0))

    def attn_tile(s):
        ta = jnp.minimum(s, n_tiles - 1)
        return ta // n_t, ta % n_t

    def lru_tile(s):
        tl = jnp.maximum(s - 1, 0)
        return tl // n_t, tl % n_t

    return pl.pallas_call(
        functools.partial(_prompt_kernel, n_t=n_t, n_tiles=n_tiles),
        grid=(n_tiles + 1,),
        in_specs=[smem,
                  pl.BlockSpec((1, ROW_TILE, D_MODEL), lambda s: (*attn_tile(s), 0)),
                  _resident((D_MODEL, D_IN)), _resident((1, D_IN)),
                  _resident((CONV_WIDTH, D_LRU)), _resident((1, D_LRU)),
                  _resident((N_GATE_TILES, GATE_TILE, 2 * GATE_TILE)),
                  _resident((2, D_LRU)), _resident((1, D_LRU)), w_out_slab],
        out_specs=[pl.BlockSpec((1, ROW_TILE, D_ATTN), lambda s: (*attn_tile(s), 0)),
                   pl.BlockSpec((1, ROW_TILE, D_LRU), lambda s: (*lru_tile(s), 0)),
                   pl.BlockSpec((1, D_KV, WINDOW), lambda s: (attn_tile(s)[0], 0, 0)),
                   pl.BlockSpec((1, D_KV, WINDOW), lambda s: (attn_tile(s)[0], 0, 0)),
                   pl.BlockSpec((CONV_WIDTH - 1, bsz, D_LRU), lambda s: (0, 0, 0)),
                   pl.BlockSpec((bsz, D_LRU), lambda s: (0, 0)), w_out_slab],
        out_shape=[jax.ShapeDtypeStruct((bsz, seq, D_ATTN), BF16),
                   jax.ShapeDtypeStruct((bsz, seq, D_LRU), BF16),
                   jax.ShapeDtypeStruct((bsz, D_KV, WINDOW), F32),
                   jax.ShapeDtypeStruct((bsz, D_KV, WINDOW), F32),
                   jax.ShapeDtypeStruct((CONV_WIDTH - 1, bsz, D_LRU), F32),
                   jax.ShapeDtypeStruct((bsz, D_LRU), F32),
                   jax.ShapeDtypeStruct((D_MODEL, D_MODEL), BF16)],
        scratch_shapes=[pltpu.VMEM((ROW_TILE, end - first), F32) for first, end in _PROJ_GROUPS] + [
                        pltpu.VMEM((N_KV_HEADS, GQA_GROUP * CHUNK, KEY_PAD), F32),
                        pltpu.VMEM((WINDOW_CHUNKS, SUBLANES, KEY_PAD), F32),
                        pltpu.VMEM((N_KV_HEADS, GQA_GROUP * CHUNK, LANES), F32),
                        pltpu.VMEM((SCORE_RING, GQA_GROUP * CHUNK, KEY_PAD), F32),
                        pltpu.VMEM((N_KV_HEADS, KBUF_ROWS, GROUP_W), BF16),
                        pltpu.VMEM((N_KV_HEADS, KBUF_ROWS, V_EXT_W), BF16),
                        pltpu.VMEM((SUBLANES + ROW_TILE, D_LRU), F32),
                        pltpu.VMEM((ROW_TILE, D_LRU), F32),
                        pltpu.VMEM((ROW_TILE, D_LRU), F32),
                        pltpu.VMEM((SUBLANES, D_LRU), F32)],
        compiler_params=pltpu.CompilerParams(
            dimension_semantics=("arbitrary",), vmem_limit_bytes=VMEM_LIMIT_BYTES),
        name="prompt_mixer",
    )(sinks, x3, w_in_bf, b_in_row, cw, cb, wg, bg, lam, w_out)


def _out_proj_kernel(ua_ref, ul_ref, x_ref, w_ref, g_ref, b_ref, y_ref):
    tm = y_ref.shape[0]
    part = min(MXU_DIM, tm)
    n_parts = tm // part
    n_chunks = D_MODEL // N_TILE
    ln_rows = part // n_chunks

    def matmul_chunk(p, n):
        rows = slice(p * part, (p + 1) * part)
        cols = slice(n * N_TILE, (n + 1) * N_TILE)
        pre = (
            DEEPNORM_ALPHA * x_ref[rows, cols]
            + jnp.dot(ua_ref[rows, :], w_ref[0:D_ATTN, cols], preferred_element_type=F32)
            + jnp.dot(ul_ref[rows, :], w_ref[D_ATTN:, cols], preferred_element_type=F32))
        y_ref[rows, cols] = pre
        words = lax.bitcast_convert_type(pre[0:SUBLANES, 0:1], jnp.uint32)
        zero = lax.shift_right_logical(lax.shift_right_logical(words, jnp.uint32(16)), jnp.uint32(16))
        return lax.bitcast_convert_type(zero, F32)

    def layer_norm_rows(p, r, zero_col=None):
        rows = slice(p * part + r * ln_rows, p * part + (r + 1) * ln_rows)
        y = y_ref[rows, :]
        mu = jnp.mean(y, axis=-1, keepdims=True)
        if zero_col is not None:
            mu = mu + jnp.concatenate([zero_col] * (ln_rows // SUBLANES), axis=0)
        yc = y - mu
        var = jnp.mean(yc * yc, axis=-1, keepdims=True)
        y_ref[rows, :] = yc * lax.rsqrt(var + LN_EPS) * g_ref[...] + b_ref[...]

    for n in range(n_chunks):
        matmul_chunk(0, n)
    for p in range(1, n_parts):
        for n in range(n_chunks):
            layer_norm_rows(p - 1, n, matmul_chunk(p, n))
    for r in range(n_chunks):
        layer_norm_rows(n_parts - 1, r)


def _out_proj(ua2d, ul2d, x2d, w_bf, g_row, b_row):
    m = x2d.shape[0]
    tm = min(OUT_ROW_TILE, m)
    return pl.pallas_call(
        _out_proj_kernel,
        grid=(m // tm,),
        in_specs=[pl.BlockSpec((tm, D_ATTN), lambda i: (i, 0)),
                  pl.BlockSpec((tm, D_LRU), lambda i: (i, 0)),
                  pl.BlockSpec((tm, D_MODEL), lambda i: (i, 0)),
                  _resident((D_MODEL, D_MODEL)),
                  _resident((1, D_MODEL)),
                  _resident((1, D_MODEL))],
        out_specs=pl.BlockSpec((tm, D_MODEL), lambda i: (i, 0)),
        out_shape=jax.ShapeDtypeStruct((m, D_MODEL), F32),
        compiler_params=pltpu.CompilerParams(
            dimension_semantics=("arbitrary",), vmem_limit_bytes=VMEM_LIMIT_BYTES),
        name="out_proj",
    )(ua2d, ul2d, x2d, w_bf, g_row, b_row)


def _in_proj_kernel(x_ref, w_ref, b_ref, z_ref, w_bf_ref):
    @pl.when(pl.program_id(0) == 0)
    def _start_from_bias():
        z_ref[...] = jnp.broadcast_to(b_ref[...], z_ref.shape)

    xb = x_ref[...].astype(BF16)
    for n in range(0, D_IN, N_TILE):
        wb = w_ref[:, n:n + N_TILE].astype(BF16)
        w_bf_ref[:, n:n + N_TILE] = wb
        z_ref[:, n:n + N_TILE] += jnp.dot(xb, wb, preferred_element_type=F32)


def _in_proj(x2d, w, b_row):
    m = x2d.shape[0]
    return pl.pallas_call(
        _in_proj_kernel,
        grid=(D_MODEL // MXU_DIM,),
        in_specs=[pl.BlockSpec((m, MXU_DIM), lambda k: (0, k)),
                  pl.BlockSpec((MXU_DIM, D_IN), lambda k: (k, 0)),
                  pl.BlockSpec((1, D_IN), lambda k: (0, 0))],
        out_specs=[pl.BlockSpec((m, D_IN), lambda k: (0, 0)),
                   pl.BlockSpec((MXU_DIM, D_IN), lambda k: (k, 0))],
        out_shape=[jax.ShapeDtypeStruct((m, D_IN), F32),
                   jax.ShapeDtypeStruct((D_MODEL, D_IN), BF16)],
        compiler_params=pltpu.CompilerParams(
            dimension_semantics=("arbitrary",), vmem_limit_bytes=VMEM_LIMIT_BYTES),
        name="in_proj",
    )(x2d, w, b_row)


def _sample_mixer_kernel(sinks_ref, z_ref, ck_ref, cv_ref, sc_ref, sh_ref,
                         cw_ref, cb_ref, wg_ref, bg_ref, lam_ref,
                         ua_ref, ul_ref, kwin_ref, vwin_ref, conv_ref, hout_ref,
                         bias_s, sink_s, s_ring, k4_s, v4_s, xbuf, a_s, b_s, hc, *, t_rows, n_cache):
    b = pl.program_id(0)
    n_keys = n_cache + t_rows
    z = _ColGroups([(0, D_IN, z_ref.at[0])])
    ua = ua_ref.at[0]

    @pl.when(b == 0)
    def _init_tables():
        for kv in range(N_KV_HEADS):
            bias_s[kv] = _bias_table(t_rows, kv, PAST_LEN - n_cache, PAST_LEN, n_keys)
            sink_s[kv] = _sink_table(t_rows, kv, sinks_ref)
        k4_s[...] = jnp.zeros_like(k4_s)
        v4_s[:, :, 0:LANES] = jnp.zeros((N_KV_HEADS, KEY_PAD, LANES), BF16)
        v4_s[:, :, LANES:] = jnp.ones((N_KV_HEADS, KEY_PAD, LANES), BF16)

    k_all = jnp.concatenate([ck_ref[0], z[_ALL_ROWS, OFF_K:OFF_K + D_KV]], axis=0)
    v_all = jnp.concatenate([cv_ref[0], z[_ALL_ROWS, OFF_V:OFF_V + D_KV]], axis=0)
    for kv in range(N_KV_HEADS):
        k4_s[kv, 0:n_keys, :] = _tile_heads(k_all, kv, GQA_GROUP)
        v4_s[kv, 0:n_keys, 0:LANES] = _tile_heads(v_all, kv, 2)
    for kv in range(N_KV_HEADS):
        s_ring[kv] = _attn_scores(z[_ALL_ROWS, OFF_Q + kv * GROUP_W:OFF_Q + (kv + 1) * GROUP_W], k4_s[kv])
    for kv in range(N_KV_HEADS):
        o = _attn_finish(s_ring[kv], v4_s[kv], bias_s[kv], sink_s[kv])
        ga = z[_ALL_ROWS, OFF_GA + kv * GROUP_W:OFF_GA + (kv + 1) * GROUP_W]
        ua[:, kv * GROUP_W:(kv + 1) * GROUP_W] = (o * (ga * _sigmoid(ga))).astype(ua.dtype)
    kwin_ref[0] = k_all[n_keys - WINDOW:, :]
    vwin_ref[0] = v_all[n_keys - WINDOW:, :]

    xbuf[0:SUBLANES, :] = jnp.zeros((SUBLANES, D_LRU), F32)
    for j in range(CONV_WIDTH - 1):
        row = SUBLANES - (CONV_WIDTH - 1) + j
        xbuf[row:row + 1, :] = sc_ref[j, pl.ds(b, 1), :]
    hc[...] = jnp.broadcast_to(sh_ref[pl.ds(b, 1), :], (SUBLANES, D_LRU))
    _lru(z, ul_ref.at[0], t_rows, xbuf, a_s, b_s, hc, cw_ref, cb_ref, wg_ref, bg_ref, lam_ref)
    for j in range(CONV_WIDTH - 1):
        row = SUBLANES - (CONV_WIDTH - 1) + j
        conv_ref[j, pl.ds(b, 1), :] = xbuf[row:row + 1, :]
    hout_ref[pl.ds(b, 1), :] = hc[0:1, :]


def _sample_mixer(z3, cache_k, cache_v, state_conv, state_h, sinks, cw, cb, wg, bg, lam):
    bsz, t_rows, _ = z3.shape
    n_cache = cache_k.shape[1]
    assert n_cache + t_rows <= KEY_PAD and n_cache + t_rows >= WINDOW
    assert t_rows % SUBLANES == 0 and t_rows & (t_rows - 1) == 0 and t_rows >= CONV_WIDTH - 1
    smem = pl.BlockSpec(memory_space=pltpu.SMEM)
    per_b = lambda shape: pl.BlockSpec((1,) + shape, lambda b: (b, 0, 0))
    return pl.pallas_call(
        functools.partial(_sample_mixer_kernel, t_rows=t_rows, n_cache=n_cache),
        grid=(bsz,),
        in_specs=[smem, per_b((t_rows, D_IN)), per_b((n_cache, D_KV)), per_b((n_cache, D_KV)),
                  _resident((CONV_WIDTH - 1, bsz, D_LRU)), _resident((bsz, D_LRU)),
                  _resident((CONV_WIDTH, D_LRU)), _resident((1, D_LRU)),
                  _resident((N_GATE_TILES, GATE_TILE, 2 * GATE_TILE)),
                  _resident((2, D_LRU)), _resident((1, D_LRU))],
        out_specs=[per_b((t_rows, D_ATTN)), per_b((t_rows, D_LRU)), per_b((WINDOW, D_KV)), per_b((WINDOW, D_KV)),
                   pl.BlockSpec((CONV_WIDTH - 1, bsz, D_LRU), lambda b: (0, 0, 0)),
                   pl.BlockSpec((bsz, D_LRU), lambda b: (0, 0))],
        out_shape=[jax.ShapeDtypeStruct((bsz, t_rows, D_ATTN), BF16),
                   jax.ShapeDtypeStruct((bsz, t_rows, D_LRU), BF16),
                   jax.ShapeDtypeStruct((bsz, WINDOW, D_KV), F32),
                   jax.ShapeDtypeStruct((bsz, WINDOW, D_KV), F32),
                   jax.ShapeDtypeStruct((CONV_WIDTH - 1, bsz, D_LRU), F32),
                   jax.ShapeDtypeStruct((bsz, D_LRU), F32)],
        scratch_shapes=[pltpu.VMEM((N_KV_HEADS, GQA_GROUP * t_rows, KEY_PAD), F32),
                        pltpu.VMEM((N_KV_HEADS, GQA_GROUP * t_rows, LANES), F32),
                        pltpu.VMEM((N_KV_HEADS, GQA_GROUP * t_rows, KEY_PAD), F32),
                        pltpu.VMEM((N_KV_HEADS, KEY_PAD, GROUP_W), BF16),
                        pltpu.VMEM((N_KV_HEADS, KEY_PAD, V_EXT_W), BF16),
                        pltpu.VMEM((SUBLANES + t_rows, D_LRU), F32),
                        pltpu.VMEM((t_rows, D_LRU), F32),
                        pltpu.VMEM((t_rows, D_LRU), F32),
                        pltpu.VMEM((SUBLANES, D_LRU), F32)],
        compiler_params=pltpu.CompilerParams(
            dimension_semantics=("arbitrary",), vmem_limit_bytes=VMEM_LIMIT_BYTES),
        name="sample_mixer",
    )(sinks, z3, cache_k, cache_v, state_conv, state_h, cw, cb, wg, bg, lam)


def _gate_weight_tiles(w_a, w_x):
    per = GATE_TILE // LRU_BLOCK
    row_block = lax.broadcasted_iota(jnp.int32, (GATE_TILE, GATE_TILE), 0) // LRU_BLOCK
    col_block = lax.broadcasted_iota(jnp.int32, (GATE_TILE, GATE_TILE), 1) // LRU_BLOCK

    def block_diagonal(w):
        rows = w.reshape(N_GATE_TILES, GATE_TILE, LRU_BLOCK)
        return jnp.where(row_block == col_block, jnp.concatenate([rows] * per, axis=-1), 0.0)

    return jnp.concatenate([block_diagonal(w_a), block_diagonal(w_x)], axis=-1).astype(BF16)


def kernel(x_prompt, x_sample, cache_k, cache_v, state_conv, state_h, w_in, b_in, conv_w, conv_b,
           w_gate_a, b_gate_a, w_gate_x, b_gate_x, lru_lambda, attn_sinks, w_out, ln_g, ln_b):
    assert w_in.shape[0] == DEPTH == 1
    bsz, seq, _ = x_prompt.shape
    dbsz, dseq, _ = x_sample.shape
    n_cache = cache_k.shape[2]
    n_p, n_s = bsz * seq, dbsz * dseq
    assert seq % ROW_TILE == 0 and n_p % OUT_ROW_TILE == 0

    wg = _gate_weight_tiles(w_gate_a[0], w_gate_x[0])
    bg = jnp.concatenate([b_gate_a[0], b_gate_x[0]], axis=0).reshape(2, D_LRU)
    cw = conv_w[0]
    sinks = attn_sinks[0]

    xs2 = x_sample.reshape(n_s, D_MODEL)
    zs, w_in_bf = _in_proj(xs2, w_in.reshape(D_MODEL, D_IN), b_in)
    zs = zs.reshape(dbsz, dseq, D_IN)

    uap, ulp, kp, vp, cp, hp, w_out_bf = _prompt_mixer(
        x_prompt, sinks, w_in_bf, b_in, cw, conv_b, wg, bg, lru_lambda, w_out.reshape(D_MODEL, D_MODEL))
    yp = _out_proj(uap.reshape(n_p, D_ATTN), ulp.reshape(n_p, D_LRU), x_prompt.reshape(n_p, D_MODEL),
                   w_out_bf, ln_g, ln_b).reshape(bsz, seq, D_MODEL)

    uas, uls, ks, vs, cs, hs = _sample_mixer(
        zs, cache_k[0].reshape(dbsz, n_cache, D_KV), cache_v[0].reshape(dbsz, n_cache, D_KV),
        state_conv[0].transpose(1, 0, 2), state_h[0],
        sinks, cw, conv_b, wg, bg, lru_lambda)
    ys = _out_proj(uas.reshape(n_s, D_ATTN), uls.reshape(n_s, D_LRU), xs2,
                   w_out_bf, ln_g, ln_b).reshape(dbsz, dseq, D_MODEL)

    def from_channel_major(w):
        return w.reshape(bsz, N_KV_HEADS, HEAD_DIM, WINDOW).transpose(0, 3, 1, 2)[None]

    def from_row_major(w):
        return w.reshape(1, dbsz, WINDOW, N_KV_HEADS, HEAD_DIM)

    return (yp, ys,
            from_channel_major(kp), from_channel_major(vp), cp.transpose(1, 0, 2)[None], hp[None],
            from_row_major(ks), from_row_major(vs), cs.transpose(1, 0, 2)[None], hs[None])
```

```python
import functools

import jax
import jax.numpy as jnp
from jax import lax
from jax.experimental import pallas as pl
from jax.experimental.pallas import tpu as pltpu

F32 = jnp.float32
BF16 = jnp.bfloat16

D_MODEL = 2048
DEPTH = 1
CHUNK = 64
WINDOW = 128
WINDOW_CHUNKS = WINDOW // CHUNK
HEAD_DIM = 64
D_ATTN = D_MODEL // 2
D_LRU = D_MODEL - D_ATTN
N_Q_HEADS = D_ATTN // HEAD_DIM
N_KV_HEADS = 4
GQA_GROUP = N_Q_HEADS // N_KV_HEADS
D_KV = N_KV_HEADS * HEAD_DIM
N_LRU_BLOCKS = 16
LRU_BLOCK = D_LRU // N_LRU_BLOCKS
CONV_WIDTH = 4
LRU_C = 8.0
D_IN = 2 * D_ATTN + 2 * D_KV + 2 * D_LRU
PAST_LEN = 2048
DEEPNORM_ALPHA = (2.0 * DEPTH) ** 0.25
LN_EPS = 1e-5
NEG_INF = -1e30
NEG_LOG2_E = -1.4426950408889634
TINY = 1e-30

OFF_Q = 0
OFF_K = D_ATTN
OFF_V = D_ATTN + D_KV
OFF_GA = D_ATTN + 2 * D_KV
OFF_XL = OFF_GA + D_ATTN
OFF_GL = OFF_XL + D_LRU

SUBLANES = 8
LANES = 128
MXU_DIM = 256
VMEM_LIMIT_BYTES = 56 * 1024 * 1024

GROUP_W = GQA_GROUP * HEAD_DIM
KEY_PAD = 2 * LANES
V_EXT_W = 2 * LANES
GATE_TILE = MXU_DIM
N_GATE_TILES = D_LRU // GATE_TILE
ROW_TILE = 256
OUT_ROW_TILE = 1024
SCORE_LOOKAHEAD = 2
SCORE_RING = SCORE_LOOKAHEAD + 2
N_TILE = 512
K_CHUNK = 512
GATE_ROW_PARTS = 4
N_CHUNKS_PER_TILE = ROW_TILE // CHUNK
KBUF_ROWS = WINDOW + ROW_TILE + (KEY_PAD - (WINDOW + CHUNK))

_SLOPES = tuple(2.0 ** (-8.0 * h / N_Q_HEADS) for h in range(1, N_Q_HEADS + 1))


def _resident(shape):
    return pl.BlockSpec(shape, lambda *_: (0,) * len(shape), pipeline_mode=pl.Buffered(1))


class _ColGroups:
    def __init__(self, groups):
        self._groups = groups

    def _locate(self, idx):
        rows, cols = idx
        for first, end, ref in self._groups:
            if first <= cols.start and cols.stop <= end:
                return ref, rows, slice(cols.start - first, cols.stop - first)
        raise ValueError(f"columns {cols} straddle projection groups")

    def __getitem__(self, idx):
        ref, rows, cols = self._locate(idx)
        return ref[rows, cols]

    def __setitem__(self, idx, value):
        ref, rows, cols = self._locate(idx)
        ref[rows, cols] = value


_ALL_ROWS = slice(None)
_PROJ_GROUPS = ((OFF_Q, OFF_K), (OFF_K, OFF_GA), (OFF_GA, OFF_XL), (OFF_XL, OFF_GL), (OFF_GL, D_IN))


def _sigmoid(x):
    return 1.0 / (1.0 + jnp.exp2(x * NEG_LOG2_E))


def _in_proj_substeps(col_starts):
    return [(c, k) for c in col_starts for k in range(0, D_MODEL, K_CHUNK)]


def _in_proj_substep(xb, w_ref, b_ref, z, col, k0):
    cols = slice(col, col + N_TILE)
    part = jnp.dot(xb[:, k0:k0 + K_CHUNK], w_ref[k0:k0 + K_CHUNK, cols], preferred_element_type=F32)
    z[_ALL_ROWS, cols] = part + (b_ref[:, cols] if k0 == 0 else z[_ALL_ROWS, cols])
    words = lax.bitcast_convert_type(part[0:1, 0:GATE_TILE], jnp.uint32)
    zero = lax.shift_right_logical(lax.shift_right_logical(words, jnp.uint32(16)), jnp.uint32(16))
    return lax.bitcast_convert_type(zero, F32)


def _bias_table(nq, kv, key0_pos, q0_pos, n_keys):
    rows = GQA_GROUP * nq
    r = lax.broadcasted_iota(jnp.int32, (rows, KEY_PAD), 0)
    c = lax.broadcasted_iota(jnp.int32, (rows, KEY_PAD), 1)
    shift = nq.bit_length() - 1
    g = jnp.right_shift(r, shift)
    qpos = q0_pos + (r & (nq - 1))
    kpos = key0_pos + c
    slope = jnp.full((rows, KEY_PAD), _SLOPES[kv * GQA_GROUP], F32)
    for gg in range(1, GQA_GROUP):
        slope = jnp.where(g == gg, _SLOPES[kv * GQA_GROUP + gg], slope)
    dist = jnp.abs(qpos - kpos).astype(F32)
    cshift = CHUNK.bit_length() - 1
    qc = jnp.right_shift(qpos, cshift)
    kc = jnp.right_shift(kpos, cshift)
    valid = (kpos >= 0) & (kc <= qc) & (kc >= qc - WINDOW_CHUNKS) & (c < n_keys)
    return jnp.where(valid, -(slope * dist), NEG_INF)


def _tile_heads(x, kv, copies):
    pair = x[:, (kv // 2) * LANES:(kv // 2 + 1) * LANES]
    rolled = pltpu.roll(pair, HEAD_DIM, axis=1)
    lane = lax.broadcasted_iota(jnp.int32, pair.shape, 1)
    low = lane < HEAD_DIM
    both = jnp.where(low, pair, rolled) if kv % 2 == 0 else jnp.where(low, rolled, pair)
    both = both.astype(BF16)
    return jnp.concatenate([both] * (copies * HEAD_DIM // LANES), axis=1)


def _sink_table(nq, kv, sinks_ref):
    return jnp.concatenate(
        [jnp.full((nq, LANES), sinks_ref[kv * GQA_GROUP + g], F32) for g in range(GQA_GROUP)], axis=0)


def _head_of_lane(nq):
    lane = lax.broadcasted_iota(jnp.int32, (nq, GROUP_W), 1)
    return jnp.right_shift(lane, HEAD_DIM.bit_length() - 1)


def _attn_scores(q, k4):
    nq = q.shape[0]
    qb = (q * (HEAD_DIM ** -0.5)).astype(BF16)
    head_of_lane = _head_of_lane(nq)
    zero = jnp.zeros_like(qb)
    qs = jnp.concatenate([jnp.where(head_of_lane == g, qb, zero) for g in range(GQA_GROUP)], axis=0)
    return lax.dot_general(qs, k4, (((1,), (1,)), ((), ())), preferred_element_type=F32)


def _attn_finish(scores, v4e, bias, sink):
    nq = scores.shape[0] // GQA_GROUP
    s = scores + bias
    m = jnp.maximum(jnp.broadcast_to(jnp.max(s, axis=-1, keepdims=True), sink.shape), sink)
    e = jnp.concatenate(
        [jnp.exp(s[:, j * LANES:(j + 1) * LANES] - m) for j in range(KEY_PAD // LANES)], axis=1)
    o4e = jnp.dot(e.astype(BF16), v4e, preferred_element_type=F32)
    o2 = o4e[:, :LANES] * (1.0 / (o4e[:, LANES:] + jnp.exp(sink - m)))
    low = lax.broadcasted_iota(jnp.int32, (nq, LANES), 1) < HEAD_DIM
    return jnp.concatenate(
        [jnp.where(low, o2[g * nq:(g + 1) * nq], o2[(g + 1) * nq:(g + 2) * nq]) for g in range(0, GQA_GROUP, 2)],
        axis=1)


def _lru(z, u_out, t_rows, xbuf, a_s, b_s, hc, cw_ref, cb_ref, wg_ref, bg_ref, lam_ref, fill=None):
    interleaved = fill is not None
    fill = fill or (lambda: None)
    neg = -lam_ref[...]
    softplus = jnp.maximum(neg, 0.0) + jnp.log1p(jnp.exp(-jnp.abs(neg)))
    coef = (LRU_C * NEG_LOG2_E) * softplus
    tiles = [slice(j * GATE_TILE, (j + 1) * GATE_TILE) for j in range(N_GATE_TILES)]

    def conv_and_gate_matmul(cols, j):
        xl = z[_ALL_ROWS, OFF_XL + cols.start:OFF_XL + cols.stop]
        xbuf[SUBLANES:SUBLANES + t_rows, cols] = xl
        xc = cb_ref[:, cols] + cw_ref[CONV_WIDTH - 1:CONV_WIDTH, cols] * xl
        for k in range(CONV_WIDTH - 1):
            off = SUBLANES - (CONV_WIDTH - 1) + k
            xc = xc + cw_ref[k:k + 1, cols] * xbuf[off:off + t_rows, cols]
        pre = jnp.dot(xc.astype(BF16), wg_ref[j], preferred_element_type=F32)
        a_s[:, cols] = pre[:, :GATE_TILE]
        b_s[:, cols] = pre[:, GATE_TILE:]
        return xc

    part_rows = t_rows // GATE_ROW_PARTS if interleaved else t_rows

    def gate_math(cols, xc, pace):
        for r0 in range(0, t_rows, part_rows):
            rows = slice(r0, r0 + part_rows)
            bias_r, bias_i = bg_ref[0:1, cols], bg_ref[1:2, cols]
            if pace is not None:
                bias_r, bias_i = bias_r + pace, bias_i + pace
            r = _sigmoid(a_s[rows, cols] + bias_r)
            i = _sigmoid(b_s[rows, cols] + bias_i)
            a = jnp.exp2(coef[:, cols] * r)
            gap = 1.0 - a * a
            mult = gap * lax.rsqrt(jnp.maximum(gap, TINY))
            a_s[rows, cols] = a
            b_s[rows, cols] = mult * (i * xc[rows])
            pace = fill()
        return pace

    pace = fill()
    xc_prev = None
    for j, cols in enumerate(tiles):
        xc = conv_and_gate_matmul(cols, j)
        pace = fill()
        if xc_prev is not None:
            pace = gate_math(tiles[j - 1], xc_prev, pace)
        xc_prev = xc
    gate_math(tiles[-1], xc_prev, pace)

    row = lax.broadcasted_iota(jnp.int32, (SUBLANES, D_LRU), 0)
    h_prev = hc[...]
    for gi in range(t_rows // SUBLANES):
        rows = slice(gi * SUBLANES, (gi + 1) * SUBLANES)
        a8 = a_s[rows, :]
        b8 = b_s[rows, :]
        for d in (1, 2, 4):
            keep = row >= d
            a_sh = pltpu.roll(a8, d, axis=0)
            b_sh = pltpu.roll(b8, d, axis=0)
            b8 = jnp.where(keep, a8 * b_sh + b8, b8)
            a8 = jnp.where(keep, a8 * a_sh, a8)
        h8 = a8 * h_prev + b8
        b_s[rows, :] = h8
        h_prev = jnp.broadcast_to(h8[SUBLANES - 1:SUBLANES, :], (SUBLANES, D_LRU))
    hc[...] = h_prev
    gl = z[_ALL_ROWS, OFF_GL:OFF_GL + D_LRU]
    u_out[...] = (b_s[...] * (gl * _sigmoid(gl))).astype(u_out.dtype)
    xbuf[0:SUBLANES, :] = xbuf[t_rows:t_rows + SUBLANES, :]


def _prompt_kernel(sinks_ref, x_ref, win_ref, bin_ref, cw_ref, cb_ref, wg_ref, bg_ref, lam_ref, wout_ref,
                   ua_ref, ul_ref, kwin_ref, vwin_ref, conv_ref, hout_ref, wout_bf_ref,
                   zq, zkv, zga, zxl, zgl, bias_s, mask_s, sink_s, s_ring, k4_s, v4_s, xbuf, a_s, b_s, hc,
                   *, n_t, n_tiles):
    wout_bf_ref[...] = wout_ref[...].astype(BF16)
    z = _ColGroups([(first, end, ref) for (first, end), ref in zip(_PROJ_GROUPS, (zq, zkv, zga, zxl, zgl))])
    ua = ua_ref.at[0]
    s = pl.program_id(0)
    t_a = lax.rem(jnp.minimum(s, n_tiles - 1), n_t)
    t_l = lax.rem(jnp.maximum(s - 1, 0), n_t)

    @pl.when(s == 0)
    def _init():
        for kv in range(N_KV_HEADS):
            bias_s[kv] = _bias_table(CHUNK, kv, 0, WINDOW, WINDOW + CHUNK)
            sink_s[kv] = _sink_table(CHUNK, kv, sinks_ref)
        col = lax.broadcasted_iota(jnp.int32, (SUBLANES, KEY_PAD), 1)
        for ci in range(WINDOW_CHUNKS):
            mask_s[ci] = jnp.where(col < WINDOW - ci * CHUNK, NEG_INF, 0.0)
        k4_s[...] = jnp.zeros_like(k4_s)
        v4_s[:, :, 0:LANES] = jnp.zeros((N_KV_HEADS, KBUF_ROWS, LANES), BF16)
        v4_s[:, :, LANES:] = jnp.ones((N_KV_HEADS, KBUF_ROWS, LANES), BF16)
        zxl[...] = jnp.zeros_like(zxl)
        zgl[...] = jnp.zeros_like(zgl)

    @pl.when(t_a == 0)
    def _reset_attention_carry():
        k4_s[:, 0:WINDOW, :] = jnp.zeros((N_KV_HEADS, WINDOW, GROUP_W), BF16)
        v4_s[:, 0:WINDOW, 0:LANES] = jnp.zeros((N_KV_HEADS, WINDOW, LANES), BF16)

    @pl.when(t_l == 0)
    def _reset_lru_carry():
        xbuf[0:SUBLANES, :] = jnp.zeros((SUBLANES, D_LRU), F32)
        hc[...] = jnp.zeros_like(hc)

    xb = x_ref[0].astype(BF16)
    early_cols = _in_proj_substeps(list(range(OFF_K, OFF_GA, N_TILE)) + list(range(OFF_Q, OFF_K, N_TILE)) + list(
        range(OFF_GA, OFF_XL, N_TILE)))
    late_cols = _in_proj_substeps(range(OFF_XL, D_IN, N_TILE))

    def project(steps):
        return _in_proj_substep(xb, win_ref, bin_ref, z, *steps.pop(0)) if steps else None

    _lru(z, ul_ref.at[0], ROW_TILE, xbuf, a_s, b_s, hc, cw_ref, cb_ref, wg_ref, bg_ref, lam_ref,
         fill=lambda: project(early_cols))
    while early_cols:
        project(early_cols)

    k_new = z[_ALL_ROWS, OFF_K:OFF_K + D_KV]
    v_new = z[_ALL_ROWS, OFF_V:OFF_V + D_KV]
    for kv in range(N_KV_HEADS):
        k4_s[kv, WINDOW:WINDOW + ROW_TILE, :] = _tile_heads(k_new, kv, GQA_GROUP)
        v4_s[kv, WINDOW:WINDOW + ROW_TILE, 0:LANES] = _tile_heads(v_new, kv, 2)

    units = [(ci, kv) for ci in range(N_CHUNKS_PER_TILE) for kv in range(N_KV_HEADS)]

    def scores_into_ring(idx):
        ci, kv = units[idx]
        s_ring[idx % SCORE_RING] = _attn_scores(
            z[ci * CHUNK:(ci + 1) * CHUNK, OFF_Q + kv * GROUP_W:OFF_Q + (kv + 1) * GROUP_W],
            k4_s[kv, ci * CHUNK:ci * CHUNK + KEY_PAD, :])

    mask_rows = [jnp.where(t_a == 0, mask_s[ci, 0:1, :], 0.0) for ci in range(WINDOW_CHUNKS)]
    late_per_unit = -(-len(late_cols) // len(units))
    for idx in range(SCORE_LOOKAHEAD):
        scores_into_ring(idx)
    for idx, (ci, kv) in enumerate(units):
        if idx + SCORE_LOOKAHEAD < len(units):
            scores_into_ring(idx + SCORE_LOOKAHEAD)
        for _ in range(late_per_unit):
            project(late_cols)
        s_cur = s_ring[idx % SCORE_RING]
        rows = slice(ci * CHUNK, (ci + 1) * CHUNK)
        bias = bias_s[kv] + mask_rows[ci] if ci < WINDOW_CHUNKS else bias_s[kv]
        o = _attn_finish(s_cur, v4_s[kv, ci * CHUNK:ci * CHUNK + KEY_PAD, :], bias, sink_s[kv])
        ga = z[rows, OFF_GA + kv * GROUP_W:OFF_GA + (kv + 1) * GROUP_W]
        ua[rows, kv * GROUP_W:(kv + 1) * GROUP_W] = (o * (ga * _sigmoid(ga))).astype(ua.dtype)
    assert not early_cols and not late_cols

    @pl.when(t_l == n_t - 1)
    def _emit_lru_state():
        seq_l = lax.div(jnp.maximum(s - 1, 0), n_t)
        for j in range(CONV_WIDTH - 1):
            row = SUBLANES - (CONV_WIDTH - 1) + j
            conv_ref[j, pl.ds(seq_l, 1), :] = xbuf[row:row + 1, :]
        hout_ref[pl.ds(seq_l, 1), :] = hc[0:1, :]

    @pl.when(t_a == n_t - 1)
    def _emit_kv_window():
        kwin_ref[0] = z[slice(ROW_TILE - WINDOW, ROW_TILE), OFF_K:OFF_K + D_KV].T
        vwin_ref[0] = z[slice(ROW_TILE - WINDOW, ROW_TILE), OFF_V:OFF_V + D_KV].T

    @pl.when(s < n_tiles - 1)
    def _carry_kv():
        k4_s[:, 0:WINDOW, :] = k4_s[:, ROW_TILE:ROW_TILE + WINDOW, :]
        v4_s[:, 0:WINDOW, 0:LANES] = v4_s[:, ROW_TILE:ROW_TILE + WINDOW, 0:LANES]


def _prompt_mixer(x3, sinks, w_in_bf, b_in_row, cw, cb, wg, bg, lam, w_out):
    bsz, seq, _ = x3.shape
    n_t = seq // ROW_TILE
    n_tiles = bsz * n_t
    smem = pl.BlockSpec(memory_space=pltpu.SMEM)
    assert D_MODEL % n_tiles == 0 and (D_MODEL // n_tiles) % (2 * SUBLANES) == 0
    w_out_slab = pl.BlockSpec((D_MODEL // n_tiles, D_MODEL), lambda s: (jnp.minimum(s, n_tiles - 1), 0))

    def attn_tile(s):
        ta = jnp.minimum(s, n_tiles - 1)
        return ta // n_t, ta % n_t

    def lru_tile(s):
        tl = jnp.maximum(s - 1, 0)
        return tl // n_t, tl % n_t

    return pl.pallas_call(
        functools.partial(_prompt_kernel, n_t=n_t, n_tiles=n_tiles),
        grid=(n_tiles + 1,),
        in_specs=[smem,
                  pl.BlockSpec((1, ROW_TILE, D_MODEL), lambda s: (*attn_tile(s), 0)),
                  _resident((D_MODEL, D_IN)), _resident((1, D_IN)),
                  _resident((CONV_WIDTH, D_LRU)), _resident((1, D_LRU)),
                  _resident((N_GATE_TILES, GATE_TILE, 2 * GATE_TILE)),
                  _resident((2, D_LRU)), _resident((1, D_LRU)), w_out_slab],
        out_specs=[pl.BlockSpec((1, ROW_TILE, D_ATTN), lambda s: (*attn_tile(s), 0)),
                   pl.BlockSpec((1, ROW_TILE, D_LRU), lambda s: (*lru_tile(s), 0)),
                   pl.BlockSpec((1, D_KV, WINDOW), lambda s: (attn_tile(s)[0], 0, 0)),
                   pl.BlockSpec((1, D_KV, WINDOW), lambda s: (attn_tile(s)[0], 0, 0)),
                   pl.BlockSpec((CONV_WIDTH - 1, bsz, D_LRU), lambda s: (0, 0, 0)),
                   pl.BlockSpec((bsz, D_LRU), lambda s: (0, 0)), w_out_slab],
        out_shape=[jax.ShapeDtypeStruct((bsz, seq, D_ATTN), BF16),
                   jax.ShapeDtypeStruct((bsz, seq, D_LRU), BF16),
                   jax.ShapeDtypeStruct((bsz, D_KV, WINDOW), F32),
                   jax.ShapeDtypeStruct((bsz, D_KV, WINDOW), F32),
                   jax.ShapeDtypeStruct((CONV_WIDTH - 1, bsz, D_LRU), F32),
                   jax.ShapeDtypeStruct((bsz, D_LRU), F32),
                   jax.ShapeDtypeStruct((D_MODEL, D_MODEL), BF16)],
        scratch_shapes=[pltpu.VMEM((ROW_TILE, end - first), F32) for first, end in _PROJ_GROUPS] + [
                        pltpu.VMEM((N_KV_HEADS, GQA_GROUP * CHUNK, KEY_PAD), F32),
                        pltpu.VMEM((WINDOW_CHUNKS, SUBLANES, KEY_PAD), F32),
                        pltpu.VMEM((N_KV_HEADS, GQA_GROUP * CHUNK, LANES), F32),
                        pltpu.VMEM((SCORE_RING, GQA_GROUP * CHUNK, KEY_PAD), F32),
                        pltpu.VMEM((N_KV_HEADS, KBUF_ROWS, GROUP_W), BF16),
                        pltpu.VMEM((N_KV_HEADS, KBUF_ROWS, V_EXT_W), BF16),
                        pltpu.VMEM((SUBLANES + ROW_TILE, D_LRU), F32),
                        pltpu.VMEM((ROW_TILE, D_LRU), F32),
                        pltpu.VMEM((ROW_TILE, D_LRU), F32),
                        pltpu.VMEM((SUBLANES, D_LRU), F32)],
        compiler_params=pltpu.CompilerParams(
            dimension_semantics=("arbitrary",), vmem_limit_bytes=VMEM_LIMIT_BYTES),
        name="prompt_mixer",
    )(sinks, x3, w_in_bf, b_in_row, cw, cb, wg, bg, lam, w_out)


def _out_proj_kernel(ua_ref, ul_ref, x_ref, w_ref, g_ref, b_ref, y_ref):
    tm = y_ref.shape[0]
    part = min(MXU_DIM, tm)
    n_parts = tm // part
    n_chunks = D_MODEL // N_TILE
    ln_rows = part // n_chunks

    def matmul_chunk(p, n):
        rows = slice(p * part, (p + 1) * part)
        cols = slice(n * N_TILE, (n + 1) * N_TILE)
        pre = (
            DEEPNORM_ALPHA * x_ref[rows, cols]
            + jnp.dot(ua_ref[rows, :], w_ref[0:D_ATTN, cols], preferred_element_type=F32)
            + jnp.dot(ul_ref[rows, :], w_ref[D_ATTN:, cols], preferred_element_type=F32))
        y_ref[rows, cols] = pre
        words = lax.bitcast_convert_type(pre[0:SUBLANES, 0:1], jnp.uint32)
        zero = lax.shift_right_logical(lax.shift_right_logical(words, jnp.uint32(16)), jnp.uint32(16))
        return lax.bitcast_convert_type(zero, F32)

    def layer_norm_rows(p, r, zero_col=None):
        rows = slice(p * part + r * ln_rows, p * part + (r + 1) * ln_rows)
        y = y_ref[rows, :]
        mu = jnp.mean(y, axis=-1, keepdims=True)
        if zero_col is not None:
            mu = mu + jnp.concatenate([zero_col] * (ln_rows // SUBLANES), axis=0)
        yc = y - mu
        var = jnp.mean(yc * yc, axis=-1, keepdims=True)
        y_ref[rows, :] = yc * lax.rsqrt(var + LN_EPS) * g_ref[...] + b_ref[...]

    for n in range(n_chunks):
        matmul_chunk(0, n)
    for p in range(1, n_parts):
        for n in range(n_chunks):
            layer_norm_rows(p - 1, n, matmul_chunk(p, n))
    for r in range(n_chunks):
        layer_norm_rows(n_parts - 1, r)


def _out_proj(ua2d, ul2d, x2d, w_bf, g_row, b_row):
    m = x2d.shape[0]
    tm = min(OUT_ROW_TILE, m)
    return pl.pallas_call(
        _out_proj_kernel,
        grid=(m // tm,),
        in_specs=[pl.BlockSpec((tm, D_ATTN), lambda i: (i, 0)),
                  pl.BlockSpec((tm, D_LRU), lambda i: (i, 0)),
                  pl.BlockSpec((tm, D_MODEL), lambda i: (i, 0)),
                  _resident((D_MODEL, D_MODEL)),
                  _resident((1, D_MODEL)),
                  _resident((1, D_MODEL))],
        out_specs=pl.BlockSpec((tm, D_MODEL), lambda i: (i, 0)),
        out_shape=jax.ShapeDtypeStruct((m, D_MODEL), F32),
        compiler_params=pltpu.CompilerParams(
            dimension_semantics=("arbitrary",), vmem_limit_bytes=VMEM_LIMIT_BYTES),
        name="out_proj",
    )(ua2d, ul2d, x2d, w_bf, g_row, b_row)


def _in_proj_kernel(x_ref, w_ref, b_ref, z_ref, w_bf_ref):
    @pl.when(pl.program_id(0) == 0)
    def _start_from_bias():
        z_ref[...] = jnp.broadcast_to(b_ref[...], z_ref.shape)

    xb = x_ref[...].astype(BF16)
    for n in range(0, D_IN, N_TILE):
        wb = w_ref[:, n:n + N_TILE].astype(BF16)
        w_bf_ref[:, n:n + N_TILE] = wb
        z_ref[:, n:n + N_TILE] += jnp.dot(xb, wb, preferred_element_type=F32)


def _in_proj(x2d, w, b_row):
    m = x2d.shape[0]
    return pl.pallas_call(
        _in_proj_kernel,
        grid=(D_MODEL // MXU_DIM,),
        in_specs=[pl.BlockSpec((m, MXU_DIM), lambda k: (0, k)),
                  pl.BlockSpec((MXU_DIM, D_IN), lambda k: (k, 0)),
                  pl.BlockSpec((1, D_IN), lambda k: (0, 0))],
        out_specs=[pl.BlockSpec((m, D_IN), lambda k: (0, 0)),
                   pl.BlockSpec((MXU_DIM, D_IN), lambda k: (k, 0))],
        out_shape=[jax.ShapeDtypeStruct((m, D_IN), F32),
                   jax.ShapeDtypeStruct((D_MODEL, D_IN), BF16)],
        compiler_params=pltpu.CompilerParams(
            dimension_semantics=("arbitrary",), vmem_limit_bytes=VMEM_LIMIT_BYTES),
        name="in_proj",
    )(x2d, w, b_row)


def _sample_mixer_kernel(sinks_ref, z_ref, ck_ref, cv_ref, sc_ref, sh_ref,
                         cw_ref, cb_ref, wg_ref, bg_ref, lam_ref,
                         ua_ref, ul_ref, kwin_ref, vwin_ref, conv_ref, hout_ref,
                         bias_s, sink_s, s_ring, k4_s, v4_s, xbuf, a_s, b_s, hc, *, t_rows, n_cache):
    b = pl.program_id(0)
    n_keys = n_cache + t_rows
    z = _ColGroups([(0, D_IN, z_ref.at[0])])
    ua = ua_ref.at[0]

    @pl.when(b == 0)
    def _init_tables():
        for kv in range(N_KV_HEADS):
            bias_s[kv] = _bias_table(t_rows, kv, PAST_LEN - n_cache, PAST_LEN, n_keys)
            sink_s[kv] = _sink_table(t_rows, kv, sinks_ref)
        k4_s[...] = jnp.zeros_like(k4_s)
        v4_s[:, :, 0:LANES] = jnp.zeros((N_KV_HEADS, KEY_PAD, LANES), BF16)
        v4_s[:, :, LANES:] = jnp.ones((N_KV_HEADS, KEY_PAD, LANES), BF16)

    k_all = jnp.concatenate([ck_ref[0], z[_ALL_ROWS, OFF_K:OFF_K + D_KV]], axis=0)
    v_all = jnp.concatenate([cv_ref[0], z[_ALL_ROWS, OFF_V:OFF_V + D_KV]], axis=0)
    for kv in range(N_KV_HEADS):
        k4_s[kv, 0:n_keys, :] = _tile_heads(k_all, kv, GQA_GROUP)
        v4_s[kv, 0:n_keys, 0:LANES] = _tile_heads(v_all, kv, 2)
    for kv in range(N_KV_HEADS):
        s_ring[kv] = _attn_scores(z[_ALL_ROWS, OFF_Q + kv * GROUP_W:OFF_Q + (kv + 1) * GROUP_W], k4_s[kv])
    for kv in range(N_KV_HEADS):
        o = _attn_finish(s_ring[kv], v4_s[kv], bias_s[kv], sink_s[kv])
        ga = z[_ALL_ROWS, OFF_GA + kv * GROUP_W:OFF_GA + (kv + 1) * GROUP_W]
        ua[:, kv * GROUP_W:(kv + 1) * GROUP_W] = (o * (ga * _sigmoid(ga))).astype(ua.dtype)
    kwin_ref[0] = k_all[n_keys - WINDOW:, :]
    vwin_ref[0] = v_all[n_keys - WINDOW:, :]

    xbuf[0:SUBLANES, :] = jnp.zeros((SUBLANES, D_LRU), F32)
    for j in range(CONV_WIDTH - 1):
        row = SUBLANES - (CONV_WIDTH - 1) + j
        xbuf[row:row + 1, :] = sc_ref[j, pl.ds(b, 1), :]
    hc[...] = jnp.broadcast_to(sh_ref[pl.ds(b, 1), :], (SUBLANES, D_LRU))
    _lru(z, ul_ref.at[0], t_rows, xbuf, a_s, b_s, hc, cw_ref, cb_ref, wg_ref, bg_ref, lam_ref)
    for j in range(CONV_WIDTH - 1):
        row = SUBLANES - (CONV_WIDTH - 1) + j
        conv_ref[j, pl.ds(b, 1), :] = xbuf[row:row + 1, :]
    hout_ref[pl.ds(b, 1), :] = hc[0:1, :]


def _sample_mixer(z3, cache_k, cache_v, state_conv, state_h, sinks, cw, cb, wg, bg, lam):
    bsz, t_rows, _ = z3.shape
    n_cache = cache_k.shape[1]
    assert n_cache + t_rows <= KEY_PAD and n_cache + t_rows >= WINDOW
    assert t_rows % SUBLANES == 0 and t_rows & (t_rows - 1) == 0 and t_rows >= CONV_WIDTH - 1
    smem = pl.BlockSpec(memory_space=pltpu.SMEM)
    per_b = lambda shape: pl.BlockSpec((1,) + shape, lambda b: (b, 0, 0))
    return pl.pallas_call(
        functools.partial(_sample_mixer_kernel, t_rows=t_rows, n_cache=n_cache),
        grid=(bsz,),
        in_specs=[smem, per_b((t_rows, D_IN)), per_b((n_cache, D_KV)), per_b((n_cache, D_KV)),
                  _resident((CONV_WIDTH - 1, bsz, D_LRU)), _resident((bsz, D_LRU)),
                  _resident((CONV_WIDTH, D_LRU)), _resident((1, D_LRU)),
                  _resident((N_GATE_TILES, GATE_TILE, 2 * GATE_TILE)),
                  _resident((2, D_LRU)), _resident((1, D_LRU))],
        out_specs=[per_b((t_rows, D_ATTN)), per_b((t_rows, D_LRU)), per_b((WINDOW, D_KV)), per_b((WINDOW, D_KV)),
                   pl.BlockSpec((CONV_WIDTH - 1, bsz, D_LRU), lambda b: (0, 0, 0)),
                   pl.BlockSpec((bsz, D_LRU), lambda b: (0, 0))],
        out_shape=[jax.ShapeDtypeStruct((bsz, t_rows, D_ATTN), BF16),
                   jax.ShapeDtypeStruct((bsz, t_rows, D_LRU), BF16),
                   jax.ShapeDtypeStruct((bsz, WINDOW, D_KV), F32),
                   jax.ShapeDtypeStruct((bsz, WINDOW, D_KV), F32),
                   jax.ShapeDtypeStruct((CONV_WIDTH - 1, bsz, D_LRU), F32),
                   jax.ShapeDtypeStruct((bsz, D_LRU), F32)],
        scratch_shapes=[pltpu.VMEM((N_KV_HEADS, GQA_GROUP * t_rows, KEY_PAD), F32),
                        pltpu.VMEM((N_KV_HEADS, GQA_GROUP * t_rows, LANES), F32),
                        pltpu.VMEM((N_KV_HEADS, GQA_GROUP * t_rows, KEY_PAD), F32),
                        pltpu.VMEM((N_KV_HEADS, KEY_PAD, GROUP_W), BF16),
                        pltpu.VMEM((N_KV_HEADS, KEY_PAD, V_EXT_W), BF16),
                        pltpu.VMEM((SUBLANES + t_rows, D_LRU), F32),
                        pltpu.VMEM((t_rows, D_LRU), F32),
                        pltpu.VMEM((t_rows, D_LRU), F32),
                        pltpu.VMEM((SUBLANES, D_LRU), F32)],
        compiler_params=pltpu.CompilerParams(
            dimension_semantics=("arbitrary",), vmem_limit_bytes=VMEM_LIMIT_BYTES),
        name="sample_mixer",
    )(sinks, z3, cache_k, cache_v, state_conv, state_h, cw, cb, wg, bg, lam)


def _gate_weight_tiles(w_a, w_x):
    per = GATE_TILE // LRU_BLOCK
    row_block = lax.broadcasted_iota(jnp.int32, (GATE_TILE, GATE_TILE), 0) // LRU_BLOCK
    col_block = lax.broadcasted_iota(jnp.int32, (GATE_TILE, GATE_TILE), 1) // LRU_BLOCK

    def block_diagonal(w):
        rows = w.reshape(N_GATE_TILES, GATE_TILE, LRU_BLOCK)
        return jnp.where(row_block == col_block, jnp.concatenate([rows] * per, axis=-1), 0.0)

    return jnp.concatenate([block_diagonal(w_a), block_diagonal(w_x)], axis=-1).astype(BF16)


def kernel(x_prompt, x_sample, cache_k, cache_v, state_conv, state_h, w_in, b_in, conv_w, conv_b,
           w_gate_a, b_gate_a, w_gate_x, b_gate_x, lru_lambda, attn_sinks, w_out, ln_g, ln_b):
    assert w_in.shape[0] == DEPTH == 1
    bsz, seq, _ = x_prompt.shape
    dbsz, dseq, _ = x_sample.shape
    n_cache = cache_k.shape[2]
    n_p, n_s = bsz * seq, dbsz * dseq
    assert seq % ROW_TILE == 0 and n_p % OUT_ROW_TILE == 0

    wg = _gate_weight_tiles(w_gate_a[0], w_gate_x[0])
    bg = jnp.concatenate([b_gate_a[0], b_gate_x[0]], axis=0).reshape(2, D_LRU)
    cw = conv_w[0]
    sinks = attn_sinks[0]

    xs2 = x_sample.reshape(n_s, D_MODEL)
    zs, w_in_bf = _in_proj(xs2, w_in.reshape(D_MODEL, D_IN), b_in)
    zs = zs.reshape(dbsz, dseq, D_IN)

    uap, ulp, kp, vp, cp, hp, w_out_bf = _prompt_mixer(
        x_prompt, sinks, w_in_bf, b_in, cw, conv_b, wg, bg, lru_lambda, w_out.reshape(D_MODEL, D_MODEL))
    yp = _out_proj(uap.reshape(n_p, D_ATTN), ulp.reshape(n_p, D_LRU), x_prompt.reshape(n_p, D_MODEL),
                   w_out_bf, ln_g, ln_b).reshape(bsz, seq, D_MODEL)

    uas, uls, ks, vs, cs, hs = _sample_mixer(
        zs, cache_k[0].reshape(dbsz, n_cache, D_KV), cache_v[0].reshape(dbsz, n_cache, D_KV),
        state_conv[0].transpose(1, 0, 2), state_h[0],
        sinks, cw, conv_b, wg, bg, lru_lambda)
    ys = _out_proj(uas.reshape(n_s, D_ATTN), uls.reshape(n_s, D_LRU), xs2,
                   w_out_bf, ln_g, ln_b).reshape(dbsz, dseq, D_MODEL)

    def from_channel_major(w):
        return w.reshape(bsz, N_KV_HEADS, HEAD_DIM, WINDOW).transpose(0, 3, 1, 2)[None]

    def from_row_major(w):
        return w.reshape(1, dbsz, WINDOW, N_KV_HEADS, HEAD_DIM)

    return (yp, ys,
            from_channel_major(kp), from_channel_major(vp), cp.transpose(1, 0, 2)[None], hp[None],
            from_row_major(ks), from_row_major(vs), cs.transpose(1, 0, 2)[None], hs[None])
```
